```python
import jax
import jax.numpy as jnp
from jax import lax
import numpy as np

D_MODEL = 1024
BATCH = 1
SEQ = 16384
DEPTH = 4
DEC_BATCH = 8
DEC_SEQ = 16
PAST_LEN = 2048

CHUNK = 64
NORM_EPS = 1e-6
L2_EPS = 1e-12
N_BRANCH = 3
D_CONV = D_MODEL
CONV_A_WIDTH = 3
RW_HEAD_DIM = 64
RW_HEADS = D_MODEL // RW_HEAD_DIM
D_RW = RW_HEADS * RW_HEAD_DIM
R_DECAY = D_MODEL // 16
R_AAA = D_MODEL // 16
R_GATE = D_MODEL // 8
RW_GN_EPS = 64e-5
P_RWKV = 3 * D_RW + R_DECAY + R_AAA + R_GATE
GDN_HEAD_DIM = 128
GDN_HEADS = D_MODEL // GDN_HEAD_DIM
D_GDN = GDN_HEADS * GDN_HEAD_DIM
GDN_CONV_WIDTH = 4
P_GDN = 4 * D_GDN + 2 * GDN_HEADS
D_FF = 4 * D_MODEL
P_IN = 3 * D_CONV + P_RWKV + P_GDN + N_BRANCH * D_MODEL

kernel_name = "hybrid_conv_rwkv7_gdn_streaming_step"


def _rms_norm(x, gain):
    xf = x.astype(jnp.float32)
    y = xf * lax.rsqrt(jnp.mean(xf * xf, axis=-1, keepdims=True) + NORM_EPS)
    return (y * gain.astype(jnp.float32)).astype(x.dtype)


def _l2_normalize(x):
    return x * lax.rsqrt(jnp.sum(x * x, axis=-1, keepdims=True) + L2_EPS)


def _causal_depthwise_conv(u, buf, w):
    width, ch = w.shape
    full = jnp.concatenate([buf.astype(u.dtype), u], axis=1)
    y = lax.conv_general_dilated(full, w.astype(u.dtype)[:, None, :], window_strides=(1,), padding="VALID",
                                 dimension_numbers=("NWC", "WIO", "NWC"), feature_group_count=ch)
    return y, full[:, full.shape[1] - (width - 1):]


def _rwkv7_time_mix(p, prev, s0, mu, w0, w2, a0, a2, g2, k_k, k_a, r_k, ln_w, ln_b):
    f32 = jnp.float32
    bsz, t_len, _ = p.shape
    p_prev = jnp.concatenate([prev[:, None, :].astype(p.dtype), p[:, :-1]], axis=1)
    ps = (p + (p_prev - p) * mu.astype(p.dtype)).astype(f32)
    o3 = 3 * D_RW
    r, k, v, xw, xa, xg = jnp.split(ps, [D_RW, 2 * D_RW, o3, o3 + R_DECAY, o3 + R_DECAY + R_AAA], axis=-1)
    w = -jax.nn.softplus(-(w0.astype(f32) + jnp.tanh(xw) @ w2.astype(f32))) - 0.5
    a = jax.nn.sigmoid(a0.astype(f32) + xa @ a2.astype(f32))
    g = jax.nn.sigmoid(xg) @ g2.astype(f32)

    def heads(t):
        return t.reshape(bsz, t_len, RW_HEADS, RW_HEAD_DIM)

    kk = _l2_normalize(heads(k * k_k.astype(f32)))
    k = k * (1.0 + (a - 1.0) * k_a.astype(f32))
    r, k, v, a = heads(r), heads(k), heads(v), heads(a)
    decay = heads(jnp.exp(-jnp.exp(w)))
    kka = kk * a

    def step(s, inp):
        r_t, k_t, v_t, d_t, kk_t, kka_t = inp
        sa = jnp.einsum("bhvk,bhk->bhv", s, kk_t)
        s = s * d_t[:, :, None, :] - sa[..., None] * kka_t[:, :, None, :] + v_t[..., None] * k_t[:, :, None, :]
        return s, jnp.einsum("bhvk,bhk->bhv", s, r_t)

    xs = tuple(jnp.swapaxes(t, 0, 1) for t in (r, k, v, decay, kk, kka))
    s_t, y = lax.scan(step, s0.astype(f32), xs)
    y = jnp.swapaxes(y, 0, 1)
    yc = y - jnp.mean(y, axis=-1, keepdims=True)
    yn = yc * lax.rsqrt(jnp.mean(yc * yc, axis=-1, keepdims=True) + RW_GN_EPS)
    yn = yn.reshape(bsz, t_len, D_RW) * ln_w.astype(f32) + ln_b.astype(f32)
    bonus = (jnp.sum(r * k * r_k.astype(f32), axis=-1, keepdims=True) * v).reshape(bsz, t_len, D_RW)
    return (yn + bonus) * g, p[:, -1, :], s_t


def _chunk_gated_delta_rule(q, k, v, g, beta, s0, chunk):
    bsz, t_len, n_h, _ = q.shape
    n_blk = t_len // chunk

    def blocks(t):
        t = t.reshape((bsz, n_blk, chunk) + t.shape[2:])
        return jnp.moveaxis(jnp.moveaxis(t, 1, 0), 2, 3)

    q, k, v, g, beta = blocks(q), blocks(k), blocks(v), blocks(g), blocks(beta)
    G = jnp.cumsum(g, axis=-1)
    idx = jnp.arange(chunk)
    incl = idx[:, None] >= idx[None, :]
    strict = idx[:, None] > idx[None, :]
    decay_mat = jnp.exp(jnp.where(incl, G[..., :, None] - G[..., None, :], -jnp.inf))
    kb = k * beta[..., None]
    a_mat = jnp.where(strict, jnp.einsum("nbhid,nbhjd->nbhij", kb, k) * decay_mat, 0.0)
    eye = jnp.eye(chunk, dtype=jnp.float32)
    t_mat = lax.linalg.triangular_solve(eye + a_mat, jnp.broadcast_to(eye, a_mat.shape), left_side=True, lower=True)
    u = t_mat @ (v * beta[..., None])
    w = t_mat @ (kb * jnp.exp(G)[..., None])
    attn_qk = jnp.einsum("nbhid,nbhjd->nbhij", q, k) * decay_mat
    q_dec = q * jnp.exp(G)[..., None]
    k_dec = k * jnp.exp(G[..., -1:] - G)[..., None]
    blk_decay = jnp.exp(G[..., -1])

    def step(s, inp):
        u_n, w_n, qd_n, kd_n, aqk_n, cd_n = inp
        v_new = u_n - jnp.einsum("bhcd,bhde->bhce", w_n, s)
        o_n = jnp.einsum("bhcd,bhde->bhce", qd_n, s) + jnp.einsum("bhij,bhje->bhie", aqk_n, v_new)
        s = s * cd_n[..., None, None] + jnp.einsum("bhcd,bhce->bhde", kd_n, v_new)
        return s, o_n

    s_t, o = lax.scan(step, s0, (u, w, q_dec, k_dec, attn_qk, blk_decay))
    o = jnp.moveaxis(jnp.moveaxis(o, 3, 2), 0, 1)
    return o.reshape(bsz, t_len, n_h, o.shape[-1]), s_t


def _gated_deltanet_mix(qkv, z, a_in, b_in, buf, s0, conv_w, a_log, dt_bias, norm_w, chunk):
    f32 = jnp.float32
    bsz, t_len, _ = qkv.shape
    c, new_buf = _causal_depthwise_conv(qkv, buf, conv_w)
    c = jax.nn.silu(c.astype(f32))
    q, k, v = jnp.split(c, [D_GDN, 2 * D_GDN], axis=-1)
    shp = (bsz, t_len, GDN_HEADS, GDN_HEAD_DIM)
    q = _l2_normalize(q.reshape(shp)) * (GDN_HEAD_DIM ** -0.5)
    k = _l2_normalize(k.reshape(shp))
    v = v.reshape(shp)
    beta = jax.nn.sigmoid(b_in.astype(f32))
    g = -jnp.exp(a_log.astype(f32)) * jax.nn.softplus(a_in.astype(f32) + dt_bias.astype(f32))
    o, s_t = _chunk_gated_delta_rule(q, k, v, g, beta, s0.astype(f32), chunk)
    on = o * lax.rsqrt(jnp.mean(o * o, axis=-1, keepdims=True) + NORM_EPS) * norm_w.astype(f32)
    out = on * jax.nn.silu(z.astype(f32).reshape(shp))
    return out.reshape(bsz, t_len, D_GDN), new_buf, s_t


def _trunk_layer(x, conv_a_buf, rw_prev, rw_state, gdn_buf, gdn_state, prm):
    chunk = min(CHUNK, x.shape[1])
    h = _rms_norm(x, prm["norm_mix_pre"])
    proj = h @ prm["w_in"]
    c3 = 3 * D_CONV
    c4 = c3 + P_RWKV
    c5 = c4 + 3 * D_GDN
    c6 = c5 + D_GDN
    c7 = c6 + GDN_HEADS
    c8 = c7 + GDN_HEADS
    cb, cc, cx, prw, qkv, z, ga, gb, gates = jnp.split(proj, [D_CONV, 2 * D_CONV, c3, c4, c5, c6, c7, c8], axis=-1)
    ya, conv_a_new = _causal_depthwise_conv(cc * cx, conv_a_buf, prm["conv_a_w"])
    ya = cb * ya
    yb, rw_prev_new, rw_state_new = _rwkv7_time_mix(prw, rw_prev, rw_state, prm["rwkv_mu"], prm["rwkv_w0"], prm["rwkv_w2"],
                                                    prm["rwkv_a0"], prm["rwkv_a2"], prm["rwkv_g2"], prm["rwkv_k_k"],
                                                    prm["rwkv_k_a"], prm["rwkv_r_k"], prm["rwkv_ln_w"], prm["rwkv_ln_b"])
    yc, gdn_buf_new, gdn_state_new = _gated_deltanet_mix(qkv, z, ga, gb, gdn_buf, gdn_state, prm["gdn_conv_w"],
                                                         prm["gdn_a_log"], prm["gdn_dt_bias"], prm["gdn_norm_w"], chunk)
    gate_a, gate_b, gate_c = jnp.split(jax.nn.sigmoid(gates), N_BRANCH, axis=-1)
    m = gate_a * ya + gate_b * yb.astype(x.dtype) + gate_c * yc.astype(x.dtype)
    x = x + _rms_norm(m @ prm["w_o"], prm["norm_mix_post"])
    h2 = _rms_norm(x, prm["norm_mlp_pre"])
    f = jnp.square(jax.nn.relu(h2 @ prm["w_ff1"])) @ prm["w_ff2"]
    x = x + _rms_norm(f, prm["norm_mlp_post"])
    return x, (conv_a_new, rw_prev_new, rw_state_new, gdn_buf_new, gdn_state_new)


def setup_inputs(seed: int = 0) -> dict:
    key = jax.random.key(seed)
    ks = iter(jax.random.split(key, 40))
    f32 = jnp.float32
    L = DEPTH

    def nrm(shape, scale):
        return jax.random.normal(next(ks), shape, f32) * scale

    def unif(shape, lo, hi):
        return jax.random.uniform(next(ks), shape, f32, lo, hi)

    dt = jnp.exp(unif((L, GDN_HEADS), float(np.log(1e-3)), float(np.log(1e-1))))
    return {
        "x_prompt": nrm((BATCH, SEQ, D_MODEL), 1.0),
        "x_sample": nrm((DEC_BATCH, DEC_SEQ, D_MODEL), 1.0),
        "state_conv_a": nrm((L, DEC_BATCH, CONV_A_WIDTH - 1, D_CONV), 1.0),
        "state_rwkv_shift": nrm((L, DEC_BATCH, P_RWKV), 1.0),
        "state_rwkv_wkv": nrm((L, DEC_BATCH, RW_HEADS, RW_HEAD_DIM, RW_HEAD_DIM), 0.5),
        "state_gdn_conv": nrm((L, DEC_BATCH, GDN_CONV_WIDTH - 1, 3 * D_GDN), 1.0),
        "state_gdn_ssm": nrm((L, DEC_BATCH, GDN_HEADS, GDN_HEAD_DIM, GDN_HEAD_DIM), 0.1),
        "norm_mix_pre": 1.0 + nrm((L, D_MODEL), 0.05),
        "norm_mix_post": 1.0 + nrm((L, D_MODEL), 0.05),
        "norm_mlp_pre": 1.0 + nrm((L, D_MODEL), 0.05),
        "norm_mlp_post": 1.0 + nrm((L, D_MODEL), 0.05),
        "w_in": nrm((L, D_MODEL, P_IN), D_MODEL ** -0.5),
        "conv_a_w": nrm((L, CONV_A_WIDTH, D_CONV), 0.5),
        "rwkv_mu": unif((L, P_RWKV), 0.0, 1.0),
        "rwkv_w0": unif((L, D_RW), -6.0, -1.0),
        "rwkv_w2": nrm((L, R_DECAY, D_RW), 0.1),
        "rwkv_a0": nrm((L, D_RW), 0.1),
        "rwkv_a2": nrm((L, R_AAA, D_RW), 0.5 * R_AAA ** -0.5),
        "rwkv_g2": nrm((L, R_GATE, D_RW), R_GATE ** -0.5),
        "rwkv_k_k": 0.85 + nrm((L, D_RW), 0.05),
        "rwkv_k_a": 1.0 + nrm((L, D_RW), 0.05),
        "rwkv_r_k": nrm((L, RW_HEADS, RW_HEAD_DIM), 0.1),
        "rwkv_ln_w": 1.0 + nrm((L, D_RW), 0.05),
        "rwkv_ln_b": nrm((L, D_RW), 0.01),
        "gdn_conv_w": nrm((L, GDN_CONV_WIDTH, 3 * D_GDN), 0.5),
        "gdn_a_log": jnp.log(unif((L, GDN_HEADS), 1.0, 16.0)),
        "gdn_dt_bias": dt + jnp.log(-jnp.expm1(-dt)),
        "gdn_norm_w": 1.0 + nrm((L, GDN_HEAD_DIM), 0.05),
        "w_o": nrm((L, D_MODEL, D_MODEL), D_MODEL ** -0.5),
        "w_ff1": nrm((L, D_MODEL, D_FF), D_MODEL ** -0.5),
        "w_ff2": nrm((L, D_FF, D_MODEL), D_FF ** -0.5),
    }


def reference(x_prompt, x_sample, state_conv_a, state_rwkv_shift, state_rwkv_wkv, state_gdn_conv, state_gdn_ssm,
              norm_mix_pre, norm_mix_post, norm_mlp_pre, norm_mlp_post, w_in, conv_a_w, rwkv_mu, rwkv_w0, rwkv_w2,
              rwkv_a0, rwkv_a2, rwkv_g2, rwkv_k_k, rwkv_k_a, rwkv_r_k, rwkv_ln_w, rwkv_ln_b, gdn_conv_w, gdn_a_log,
              gdn_dt_bias, gdn_norm_w, w_o, w_ff1, w_ff2):
    f32 = jnp.float32
    bp = x_prompt.shape[0]
    zero_states = (jnp.zeros((bp, CONV_A_WIDTH - 1, D_CONV), x_prompt.dtype),
                   jnp.zeros((bp, P_RWKV), x_prompt.dtype),
                   jnp.zeros((bp, RW_HEADS, RW_HEAD_DIM, RW_HEAD_DIM), f32),
                   jnp.zeros((bp, GDN_CONV_WIDTH - 1, 3 * D_GDN), x_prompt.dtype),
                   jnp.zeros((bp, GDN_HEADS, GDN_HEAD_DIM, GDN_HEAD_DIM), f32))
    y_p = x_prompt
    y_s = x_sample
    new_p = [[], [], [], [], []]
    new_s = [[], [], [], [], []]
    for l in range(DEPTH):
        prm = {
            "norm_mix_pre": norm_mix_pre[l], "norm_mix_post": norm_mix_post[l],
            "norm_mlp_pre": norm_mlp_pre[l], "norm_mlp_post": norm_mlp_post[l],
            "w_in": w_in[l], "conv_a_w": conv_a_w[l],
            "rwkv_mu": rwkv_mu[l], "rwkv_w0": rwkv_w0[l], "rwkv_w2": rwkv_w2[l], "rwkv_a0": rwkv_a0[l],
            "rwkv_a2": rwkv_a2[l], "rwkv_g2": rwkv_g2[l], "rwkv_k_k": rwkv_k_k[l], "rwkv_k_a": rwkv_k_a[l],
            "rwkv_r_k": rwkv_r_k[l], "rwkv_ln_w": rwkv_ln_w[l], "rwkv_ln_b": rwkv_ln_b[l],
            "gdn_conv_w": gdn_conv_w[l], "gdn_a_log": gdn_a_log[l], "gdn_dt_bias": gdn_dt_bias[l],
            "gdn_norm_w": gdn_norm_w[l], "w_o": w_o[l], "w_ff1": w_ff1[l], "w_ff2": w_ff2[l],
        }
        y_p, st_p = _trunk_layer(y_p, *zero_states, prm)
        y_s, st_s = _trunk_layer(y_s, state_conv_a[l], state_rwkv_shift[l], state_rwkv_wkv[l],
                                 state_gdn_conv[l], state_gdn_ssm[l], prm)
        for i in range(5):
            new_p[i].append(st_p[i])
            new_s[i].append(st_s[i])
    p_conv_a, p_rw_shift, p_rw_wkv, p_gdn_conv, p_gdn_ssm = [jnp.stack(t, axis=0) for t in new_p]
    s_conv_a, s_rw_shift, s_rw_wkv, s_gdn_conv, s_gdn_ssm = [jnp.stack(t, axis=0) for t in new_s]
    return (y_p, y_s, p_conv_a, p_rw_shift, p_rw_wkv, p_gdn_conv, p_gdn_ssm,
            s_conv_a, s_rw_shift, s_rw_wkv, s_gdn_conv, s_gdn_ssm)
```

```python
import functools

import jax
import jax.numpy as jnp
from jax import lax
from jax.experimental import pallas as pl
from jax.experimental.pallas import tpu as pltpu

F32 = jnp.float32
BF16 = jnp.bfloat16

D_MODEL = 1024
DEPTH = 4
CHUNK = 64
NORM_EPS = 1e-6
L2_EPS = 1e-12
RW_HEAD_DIM = 64
RW_HEADS = D_MODEL // RW_HEAD_DIM
RW_PAIRS = RW_HEADS // 2
R_DECAY = D_MODEL // 16
R_AAA = D_MODEL // 16
R_GATE = D_MODEL // 8
R_SMALL = R_DECAY + R_AAA + R_GATE
RW_GN_EPS = 64e-5
GDN_HEAD_DIM = 128
GDN_HEADS = D_MODEL // GDN_HEAD_DIM
D_FF = 4 * D_MODEL
LANES = 128
HALO = 8

COL_CONV = 0
COL_QKV = 3 * D_MODEL
COL_GATES = 6 * D_MODEL
COL_RKV = 9 * D_MODEL
COL_Z = 12 * D_MODEL
COL_SMALL = 13 * D_MODEL
COL_GAB = COL_SMALL + R_SMALL
P_USED = COL_GAB + LANES
PROJ_TN = 2048
P_PAD = ((P_USED + PROJ_TN - 1) // PROJ_TN) * PROJ_TN

VMEM_LIMIT = 56 * 1024 * 1024

_HI = lax.Precision.HIGHEST


def _dot(a, b, precision=_HI):
    return lax.dot_general(a, b, (((1,), (0,)), ((), ())), precision=precision, preferred_element_type=F32)


def _dot_nt(a, b, precision=_HI):
    return lax.dot_general(a, b, (((1,), (1,)), ((), ())), precision=precision, preferred_element_type=F32)


def _dot_tn(a, b, precision=_HI):
    return lax.dot_general(a, b, (((0,), (0,)), ((), ())), precision=precision, preferred_element_type=F32)


def _sigmoid(x):
    return 1.0 / (1.0 + jnp.exp(-x))


def _softplus(x):
    return jnp.maximum(x, 0.0) + jnp.log(1.0 + jnp.exp(-jnp.abs(x)))


def _silu(x):
    return x * _sigmoid(x)


def _rms(x, gain):
    return x * lax.rsqrt(jnp.mean(x * x, axis=-1, keepdims=True) + NORM_EPS) * gain


def _proj_kernel(x_ref, g_ref, w_ref, o_ref):
    h = _rms(x_ref[...], g_ref[...])
    o_ref[...] = jnp.dot(h.astype(BF16), w_ref[...], preferred_element_type=F32)


def _proj(x, gain, w_bf16, tm):
    rows = x.shape[0]
    return pl.pallas_call(
        _proj_kernel,
        grid=(P_PAD // PROJ_TN, rows // tm),
        in_specs=[
            pl.BlockSpec((tm, D_MODEL), lambda j, i: (i, 0)),
            pl.BlockSpec((1, D_MODEL), lambda j, i: (0, 0)),
            pl.BlockSpec((D_MODEL, PROJ_TN), lambda j, i: (0, j)),
        ],
        out_specs=pl.BlockSpec((tm, PROJ_TN), lambda j, i: (i, j)),
        out_shape=jax.ShapeDtypeStruct((rows, P_PAD), F32),
        compiler_params=pltpu.CompilerParams(
            dimension_semantics=("arbitrary", "arbitrary"), vmem_limit_bytes=VMEM_LIMIT),
        name="proj",
    )(x, gain, w_bf16)


def _outffn_kernel(m_ref, x_ref, wo_ref, w1_ref, w2_ref, g1_ref, g2_ref, g3_ref, o_ref):
    mo = jnp.dot(m_ref[...], wo_ref[...], preferred_element_type=F32)
    x = x_ref[...] + _rms(mo, g1_ref[...])
    h2 = _rms(x, g2_ref[...])
    f = jnp.dot(h2.astype(BF16), w1_ref[...], preferred_element_type=F32)
    f = jnp.square(jnp.maximum(f, 0.0))
    f = jnp.dot(f.astype(BF16), w2_ref[...], preferred_element_type=F32)
    o_ref[...] = x + _rms(f, g3_ref[...])


def _outffn(m, x, wo, w1, w2, g_post, g_pre2, g_post2, tm):
    rows = x.shape[0]
    const = lambda i: (0, 0)
    return pl.pallas_call(
        _outffn_kernel,
        grid=(rows // tm,),
        in_specs=[
            pl.BlockSpec((tm, D_MODEL), lambda i: (i, 0)),
            pl.BlockSpec((tm, D_MODEL), lambda i: (i, 0)),
            pl.BlockSpec((D_MODEL, D_MODEL), const),
            pl.BlockSpec((D_MODEL, D_FF), const),
            pl.BlockSpec((D_FF, D_MODEL), const),
            pl.BlockSpec((1, D_MODEL), const),
            pl.BlockSpec((1, D_MODEL), const),
            pl.BlockSpec((1, D_MODEL), const),
        ],
        out_specs=pl.BlockSpec((tm, D_MODEL), lambda i: (i, 0)),
        out_shape=jax.ShapeDtypeStruct((rows, D_MODEL), F32),
        compiler_params=pltpu.CompilerParams(
            dimension_semantics=("arbitrary",), vmem_limit_bytes=VMEM_LIMIT),
        name="outffn",
    )(m, x, wo, w1, w2, g_post, g_pre2, g_post2)


def _unit_lower_inverse(a, steps):
    n = a.shape[0]
    eye = (lax.broadcasted_iota(jnp.int32, (n, n), 0) == lax.broadcasted_iota(jnp.int32, (n, n), 1)).astype(F32)
    t = eye - a
    p = a
    for _ in range(steps):
        p = _dot(p, p)
        t = t + _dot(t, p)
    return t


def _mixer_kernel(chunk,
                  conv_ref, qkv_ref, gates_ref, rkv_ref, z_ref, small_ref, gab_ref,
                  st_conva_ref, st_shrkv_ref, st_shsm_ref, st_wkv_ref, st_gconv_ref, st_ssm_ref,
                  convw_ref, mu_rkv_ref, mu_sm_ref, w0_ref, w2_ref, a0_ref, a2_ref, g2_ref,
                  kk_ref, ka_ref, rk_ref, lnw_ref, lnb_ref, gconvw_ref, alog_ref, dtb_ref, gnw_ref,
                  m_ref, o_conva_ref, o_shrkv_ref, o_shsm_ref, o_wkv_ref, o_gconv_ref, o_ssm_ref,
                  ext_conv, ext_qkv, ext_rkv, ext_sm, s_wkv, s_ssm):
    C = chunk
    t_idx = pl.program_id(1)
    n_t = pl.num_programs(1)
    steps = C.bit_length() - 2

    @pl.when(t_idx == 0)
    def _init():
        ext_conv[0:HALO, :] = jnp.zeros((HALO, D_MODEL), F32)
        ext_conv[HALO - 2:HALO, :] = st_conva_ref[0]
        ext_qkv[0:HALO, :] = jnp.zeros((HALO, 3 * D_MODEL), F32)
        ext_qkv[HALO - 3:HALO, :] = st_gconv_ref[0]
        ext_rkv[0:HALO, :] = jnp.zeros((HALO, 3 * D_MODEL), F32)
        ext_rkv[HALO - 1:HALO, :] = st_shrkv_ref[0]
        ext_sm[0:HALO, :] = jnp.zeros((HALO, R_SMALL), F32)
        ext_sm[HALO - 1:HALO, :] = st_shsm_ref[0]
        s_wkv[...] = st_wkv_ref[0]
        s_ssm[...] = st_ssm_ref[0]

    row_c = lax.broadcasted_iota(jnp.int32, (C, C), 0)
    col_c = lax.broadcasted_iota(jnp.int32, (C, C), 1)
    incl_c = row_c >= col_c
    strict_c = row_c > col_c
    tri_incl = incl_c.astype(F32)

    cb = conv_ref[:, 0:D_MODEL]
    u = conv_ref[:, D_MODEL:2 * D_MODEL] * conv_ref[:, 2 * D_MODEL:3 * D_MODEL]
    ext_conv[HALO:HALO + C, :] = u
    ya = cb * (convw_ref[0:1, :] * ext_conv[HALO - 2:HALO - 2 + C, :]
               + convw_ref[1:2, :] * ext_conv[HALO - 1:HALO - 1 + C, :]
               + convw_ref[2:3, :] * u)

    p_rkv = rkv_ref[...]
    ext_rkv[HALO:HALO + C, :] = p_rkv
    ps = p_rkv + (ext_rkv[HALO - 1:HALO - 1 + C, :] - p_rkv) * mu_rkv_ref[...]
    p_sm = small_ref[...]
    ext_sm[HALO:HALO + C, :] = p_sm
    ps_sm = p_sm + (ext_sm[HALO - 1:HALO - 1 + C, :] - p_sm) * mu_sm_ref[...]
    r = ps[:, 0:D_MODEL]
    k = ps[:, D_MODEL:2 * D_MODEL]
    v = ps[:, 2 * D_MODEL:3 * D_MODEL]
    xw = ps_sm[:, 0:R_DECAY]
    xa = ps_sm[:, R_DECAY:R_DECAY + R_AAA]
    xg = ps_sm[:, R_DECAY + R_AAA:R_SMALL]
    w_log = -_softplus(-(w0_ref[...] + _dot(jnp.tanh(xw), w2_ref[...]))) - 0.5
    a = _sigmoid(a0_ref[...] + _dot(xa, a2_ref[...]))
    g = _dot(_sigmoid(xg), g2_ref[...])
    logd = -jnp.exp(w_log)

    li = lax.broadcasted_iota(jnp.int32, (LANES, LANES), 0) // RW_HEAD_DIM
    lj = lax.broadcasted_iota(jnp.int32, (LANES, LANES), 1) // RW_HEAD_DIM
    bd128 = (li == lj).astype(F32)

    def seg_sum(x):
        return jnp.concatenate(
            [_dot(x[:, j * LANES:(j + 1) * LANES], bd128) for j in range(D_MODEL // LANES)], axis=1)

    kk0 = k * kk_ref[...]
    kk = kk0 * lax.rsqrt(seg_sum(kk0 * kk0) + L2_EPS)
    k2 = k * (1.0 + (a - 1.0) * ka_ref[...])
    kka = kk * a

    cum = _dot(tri_incl, logd)
    cum_prev = cum - logd
    cum_end = cum[C - 1:C, :]
    e_inv = jnp.exp(-cum)
    e_end = jnp.exp(cum_end - cum)
    kk_g = kk * jnp.exp(cum_prev)
    r_g = r * jnp.exp(cum)
    k_inv = k2 * e_inv
    kka_inv = kka * e_inv
    k_end = k2 * e_end
    kka_end = kka * e_end
    gam_end = jnp.exp(cum_end)

    lane = lax.broadcasted_iota(jnp.int32, (1, LANES), 1)
    m0 = (lane < RW_HEAD_DIM).astype(F32)
    m1 = 1.0 - m0
    n2 = 2 * C
    row_n = lax.broadcasted_iota(jnp.int32, (n2, n2), 0)
    col_n = lax.broadcasted_iota(jnp.int32, (n2, n2), 1)
    same_n = (row_n // C) == (col_n // C)
    strict_n = (same_n & (row_n > col_n)).astype(F32)
    incl_n = (same_n & (row_n >= col_n)).astype(F32)

    def stack(x):
        return jnp.concatenate([x * m0, x * m1], axis=0)

    y_parts = []
    for p in range(RW_PAIRS):
        sl = slice(p * LANES, (p + 1) * LANES)
        x1 = stack(kk_g[:, sl])
        x2 = stack(r_g[:, sl])
        yk = stack(k_inv[:, sl])
        yq = stack(kka_inv[:, sl])
        v_st = stack(v[:, sl])
        a1 = _dot_nt(x1, yk) * strict_n
        a2 = _dot_nt(x1, yq) * strict_n
        b1 = _dot_nt(x2, yk) * incl_n
        b2 = _dot_nt(x2, yq) * incl_n
        t_inv = _unit_lower_inverse(a2, steps)
        w_mat = _dot(t_inv, x1)
        u0 = _dot(t_inv, _dot(a1, v_st))
        st = s_wkv[p]
        u_st = _dot_nt(w_mat, st) + u0
        y_st = _dot_nt(x2, st) + _dot(b1, v_st) - _dot(b2, u_st)
        y_parts.append(y_st[0:C] + y_st[C:n2])
        u_un = u_st[0:C] + u_st[C:n2]
        upd = _dot_tn(jnp.concatenate([v[:, sl], -u_un], axis=0),
                      jnp.concatenate([k_end[:, sl], kka_end[:, sl]], axis=0))
        s_wkv[p] = st * gam_end[:, sl] + upd * bd128
    y_rw = jnp.concatenate(y_parts, axis=1)
    inv_n = 1.0 / RW_HEAD_DIM
    y_c = y_rw - seg_sum(y_rw) * inv_n
    y_n = y_c * lax.rsqrt(seg_sum(y_c * y_c) * inv_n + RW_GN_EPS)
    y_n = y_n * lnw_ref[...] + lnb_ref[...]
    bonus = seg_sum(r * k2 * rk_ref[...]) * v
    yb = (y_n + bonus) * g

    qkv = qkv_ref[...]
    ext_qkv[HALO:HALO + C, :] = qkv
    cq = (gconvw_ref[0:1, :] * ext_qkv[HALO - 3:HALO - 3 + C, :]
          + gconvw_ref[1:2, :] * ext_qkv[HALO - 2:HALO - 2 + C, :]
          + gconvw_ref[2:3, :] * ext_qkv[HALO - 1:HALO - 1 + C, :]
          + gconvw_ref[3:4, :] * qkv)
    cq = _silu(cq)
    gab = gab_ref[...]
    g_log = -jnp.exp(alog_ref[...]) * _softplus(gab + dtb_ref[...])
    beta_all = _sigmoid(gab)
    g_cum = _dot(tri_incl, g_log)
    g_cum_t = g_cum.T
    z = z_ref[...]
    yc_parts = []
    for h in range(GDN_HEADS):
        sl = slice(h * LANES, (h + 1) * LANES)
        qh = cq[:, sl]
        kh = cq[:, D_MODEL + h * LANES:D_MODEL + (h + 1) * LANES]
        vh = cq[:, 2 * D_MODEL + h * LANES:2 * D_MODEL + (h + 1) * LANES]
        qn = qh * lax.rsqrt(jnp.sum(qh * qh, axis=-1, keepdims=True) + L2_EPS) * (GDN_HEAD_DIM ** -0.5)
        kn = kh * lax.rsqrt(jnp.sum(kh * kh, axis=-1, keepdims=True) + L2_EPS)
        gc = g_cum[:, h:h + 1]
        gr = g_cum_t[h:h + 1, :]
        decay = jnp.where(incl_c, jnp.exp(jnp.where(incl_c, gc - gr, 0.0)), 0.0)
        beta = beta_all[:, GDN_HEADS + h:GDN_HEADS + h + 1]
        kb = kn * beta
        a_mat = jnp.where(strict_c, _dot_nt(kb, kn) * decay, 0.0)
        t_inv = _unit_lower_inverse(a_mat, steps)
        e_g = jnp.exp(gc)
        u_mat = _dot(t_inv, vh * beta)
        w_mat = _dot(t_inv, kb * e_g)
        attn = _dot_nt(qn, kn) * decay
        g_last = g_cum[C - 1:C, h:h + 1]
        k_dec = kn * jnp.exp(g_last - gc)
        s = s_ssm[h]
        v_new = u_mat - _dot(w_mat, s)
        o = _dot(qn * e_g, s) + _dot(attn, v_new)
        s_ssm[h] = s * jnp.exp(g_last) + _dot_tn(k_dec, v_new)
        on = o * lax.rsqrt(jnp.mean(o * o, axis=-1, keepdims=True) + NORM_EPS) * gnw_ref[...]
        yc_parts.append(on * _silu(z[:, sl]))
    yc = jnp.concatenate(yc_parts, axis=1)

    m = (_sigmoid(gates_ref[:, 0:D_MODEL]) * ya
         + _sigmoid(gates_ref[:, D_MODEL:2 * D_MODEL]) * yb
         + _sigmoid(gates_ref[:, 2 * D_MODEL:3 * D_MODEL]) * yc)
    m_ref[...] = m.astype(m_ref.dtype)

    ext_conv[0:HALO, :] = ext_conv[C:C + HALO, :]
    ext_qkv[0:HALO, :] = ext_qkv[C:C + HALO, :]
    ext_rkv[0:HALO, :] = ext_rkv[C:C + HALO, :]
    ext_sm[0:HALO, :] = ext_sm[C:C + HALO, :]

    @pl.when(t_idx == n_t - 1)
    def _finish():
        o_conva_ref[0] = ext_conv[HALO - 2:HALO, :]
        o_gconv_ref[0] = ext_qkv[HALO - 3:HALO, :]
        o_shrkv_ref[0] = ext_rkv[HALO - 1:HALO, :]
        o_shsm_ref[0] = ext_sm[HALO - 1:HALO, :]
        o_wkv_ref[0] = s_wkv[...]
        o_ssm_ref[0] = s_ssm[...]


def _mixer(proj, states, prm, nb, t_len, chunk):
    n_t = t_len // chunk
    rows = nb * t_len

    def pspec(width, col):
        blk = col // width
        return pl.BlockSpec((chunk, width), lambda b, t: (b * n_t + t, blk))

    def sspec(shape):
        nd = len(shape)
        return pl.BlockSpec((1,) + tuple(shape[1:]), lambda b, t: (b,) + (0,) * (nd - 1))

    def wspec(arr):
        return pl.BlockSpec(arr.shape, lambda b, t: (0,) * arr.ndim)

    st_names = ("conva", "shrkv", "shsm", "wkv", "gconv", "ssm")
    st_list = [states[n] for n in st_names]
    w_names = ("conv_a_w", "mu_rkv", "mu_sm", "w0", "w2", "a0", "a2", "g2", "k_k", "k_a", "r_k",
               "ln_w", "ln_b", "gdn_conv_w", "a_log", "dt_bias", "gdn_norm_w")
    w_list = [prm[n] for n in w_names]
    in_specs = [
        pspec(3 * D_MODEL, COL_CONV), pspec(3 * D_MODEL, COL_QKV), pspec(3 * D_MODEL, COL_GATES),
        pspec(3 * D_MODEL, COL_RKV), pspec(D_MODEL, COL_Z), pspec(R_SMALL, COL_SMALL), pspec(LANES, COL_GAB),
    ] + [sspec(s.shape) for s in st_list] + [wspec(w) for w in w_list]
    out_shapes = [jax.ShapeDtypeStruct((rows, D_MODEL), BF16)] + [
        jax.ShapeDtypeStruct(s.shape, F32) for s in st_list]
    out_specs = [pl.BlockSpec((chunk, D_MODEL), lambda b, t: (b * n_t + t, 0))] + [
        sspec(s.shape) for s in st_list]
    scratch = [
        pltpu.VMEM((HALO + chunk, D_MODEL), F32),
        pltpu.VMEM((HALO + chunk, 3 * D_MODEL), F32),
        pltpu.VMEM((HALO + chunk, 3 * D_MODEL), F32),
        pltpu.VMEM((HALO + chunk, R_SMALL), F32),
        pltpu.VMEM((RW_PAIRS, LANES, LANES), F32),
        pltpu.VMEM((GDN_HEADS, GDN_HEAD_DIM, GDN_HEAD_DIM), F32),
    ]
    outs = pl.pallas_call(
        functools.partial(_mixer_kernel, chunk),
        grid=(nb, n_t),
        in_specs=in_specs,
        out_specs=out_specs,
        out_shape=out_shapes,
        scratch_shapes=scratch,
        compiler_params=pltpu.CompilerParams(
            dimension_semantics=("arbitrary", "arbitrary"), vmem_limit_bytes=VMEM_LIMIT),
        name="mixer",
    )(*([proj] * 7), *st_list, *w_list)
    return outs[0], dict(zip(st_names, outs[1:]))


def _prep_layer(l, norm_mix_pre, norm_mix_post, norm_mlp_pre, norm_mlp_post, w_in, conv_a_w, rwkv_mu, rwkv_w0,
                rwkv_w2, rwkv_a0, rwkv_a2, rwkv_g2, rwkv_k_k, rwkv_k_a, rwkv_r_k, rwkv_ln_w, rwkv_ln_b,
                gdn_conv_w, gdn_a_log, gdn_dt_bias, gdn_norm_w, w_o, w_ff1, w_ff2):
    d = D_MODEL
    wi = w_in[l]
    c_prw = 3 * d
    c_qkv = c_prw + 3 * d + R_SMALL
    c_z = c_qkv + 3 * d
    c_ga = c_z + d
    c_gates = c_ga + 2 * GDN_HEADS
    w_perm = jnp.concatenate([
        wi[:, 0:3 * d], wi[:, c_qkv:c_qkv + 3 * d], wi[:, c_gates:c_gates + 3 * d], wi[:, c_prw:c_prw + 3 * d],
        wi[:, c_z:c_z + d], wi[:, c_prw + 3 * d:c_prw + 3 * d + R_SMALL], wi[:, c_ga:c_ga + 2 * GDN_HEADS],
        jnp.zeros((d, P_PAD - COL_GAB - 2 * GDN_HEADS), wi.dtype)], axis=1).astype(BF16)
    row = lambda a: a.reshape(1, -1).astype(F32)
    pad_heads = lambda a: jnp.pad(a.astype(F32), (0, LANES - GDN_HEADS)).reshape(1, LANES)
    mu = rwkv_mu[l]
    return {
        "g_pre": row(norm_mix_pre[l]), "g_post": row(norm_mix_post[l]),
        "g_pre2": row(norm_mlp_pre[l]), "g_post2": row(norm_mlp_post[l]),
        "w_in": w_perm, "w_o": w_o[l].astype(BF16), "w_ff1": w_ff1[l].astype(BF16), "w_ff2": w_ff2[l].astype(BF16),
        "conv_a_w": conv_a_w[l], "mu_rkv": row(mu[0:3 * d]), "mu_sm": row(mu[3 * d:]),
        "w0": row(rwkv_w0[l]), "w2": rwkv_w2[l], "a0": row(rwkv_a0[l]), "a2": rwkv_a2[l], "g2": rwkv_g2[l],
        "k_k": row(rwkv_k_k[l]), "k_a": row(rwkv_k_a[l]), "r_k": row(rwkv_r_k[l]),
        "ln_w": row(rwkv_ln_w[l]), "ln_b": row(rwkv_ln_b[l]), "gdn_conv_w": gdn_conv_w[l],
        "a_log": pad_heads(gdn_a_log[l]), "dt_bias": pad_heads(gdn_dt_bias[l]), "gdn_norm_w": row(gdn_norm_w[l]),
    }


def _pack_wkv(s):
    b = s.shape[0]
    s = s.reshape(b, RW_PAIRS, 2, RW_HEAD_DIM, RW_HEAD_DIM)
    zero = jnp.zeros_like(s[:, :, 0])
    top = jnp.concatenate([s[:, :, 0], zero], axis=-1)
    bot = jnp.concatenate([zero, s[:, :, 1]], axis=-1)
    return jnp.concatenate([top, bot], axis=-2)


def _unpack_wkv(s):
    b = s.shape[0]
    h = RW_HEAD_DIM
    return jnp.stack([s[:, :, 0:h, 0:h], s[:, :, h:, h:]], axis=2).reshape(b, RW_HEADS, h, h)


def _layer(x, nb, t_len, chunk, st, prm, tm_proj, tm_ffn):
    conv_a, rw_shift, rw_wkv, gdn_conv, gdn_ssm = st
    states = {
        "conva": conv_a, "shrkv": rw_shift[:, None, 0:3 * D_MODEL], "shsm": rw_shift[:, None, 3 * D_MODEL:],
        "wkv": _pack_wkv(rw_wkv), "gconv": gdn_conv, "ssm": gdn_ssm,
    }
    proj = _proj(x, prm["g_pre"], prm["w_in"], tm_proj)
    m, new = _mixer(proj, states, prm, nb, t_len, chunk)
    x = _outffn(m, x, prm["w_o"], prm["w_ff1"], prm["w_ff2"], prm["g_post"], prm["g_pre2"], prm["g_post2"], tm_ffn)
    new_st = (new["conva"], jnp.concatenate([new["shrkv"][:, 0], new["shsm"][:, 0]], axis=-1),
              _unpack_wkv(new["wkv"]), new["gconv"], new["ssm"])
    return x, new_st


def _row_tile(rows, want):
    t = min(rows, want)
    while rows % t:
        t //= 2
    return t


def _forward(x_prompt, x_sample, sample_states, weights):
    bp, tp, d = x_prompt.shape
    bs, ts, _ = x_sample.shape
    zero_states = (jnp.zeros((bp, 2, d), F32), jnp.zeros((bp, 3 * d + R_SMALL), F32),
                   jnp.zeros((bp, RW_HEADS, RW_HEAD_DIM, RW_HEAD_DIM), F32),
                   jnp.zeros((bp, 3, 3 * d), F32), jnp.zeros((bp, GDN_HEADS, GDN_HEAD_DIM, GDN_HEAD_DIM), F32))
    y_p = x_prompt.reshape(bp * tp, d)
    y_s = x_sample.reshape(bs * ts, d)
    new_p = [[] for _ in range(5)]
    new_s = [[] for _ in range(5)]
    n_layers = weights[0].shape[0]
    for l in range(n_layers):
        prm = _prep_layer(l, *weights)
        y_p, st_p = _layer(y_p, bp, tp, min(CHUNK, tp), zero_states, prm,
                           _row_tile(bp * tp, 1024), _row_tile(bp * tp, 256))
        y_s, st_s = _layer(y_s, bs, ts, min(CHUNK, ts), tuple(s[l] for s in sample_states), prm,
                           _row_tile(bs * ts, 1024), _row_tile(bs * ts, 256))
        for i in range(5):
            new_p[i].append(st_p[i])
            new_s[i].append(st_s[i])
    outs_p = [jnp.stack(t, axis=0) for t in new_p]
    outs_s = [jnp.stack(t, axis=0) for t in new_s]
    return (y_p.reshape(bp, tp, d), y_s.reshape(bs, ts, d), *outs_p, *outs_s)


def kernel(x_prompt, x_sample, state_conv_a, state_rwkv_shift, state_rwkv_wkv, state_gdn_conv, state_gdn_ssm,
           norm_mix_pre, norm_mix_post, norm_mlp_pre, norm_mlp_post, w_in, conv_a_w, rwkv_mu, rwkv_w0, rwkv_w2,
           rwkv_a0, rwkv_a2, rwkv_g2, rwkv_k_k, rwkv_k_a, rwkv_r_k, rwkv_ln_w, rwkv_ln_b, gdn_conv_w, gdn_a_log,
           gdn_dt_bias, gdn_norm_w, w_o, w_ff1, w_ff2):
    sample_states = (state_conv_a, state_rwkv_shift, state_rwkv_wkv, state_gdn_conv, state_gdn_ssm)
    weights = (norm_mix_pre, norm_mix_post, norm_mlp_pre, norm_mlp_post, w_in, conv_a_w, rwkv_mu, rwkv_w0, rwkv_w2,
               rwkv_a0, rwkv_a2, rwkv_g2, rwkv_k_k, rwkv_k_a, rwkv_r_k, rwkv_ln_w, rwkv_ln_b, gdn_conv_w,
               gdn_a_log, gdn_dt_bias, gdn_norm_w, w_o, w_ff1, w_ff2)
    return _forward(x_prompt, x_sample, sample_states, weights)
```

```python
import functools

import jax
import jax.numpy as jnp
from jax import lax
from jax.experimental import pallas as pl
from jax.experimental.pallas import tpu as pltpu

F32 = jnp.float32
BF16 = jnp.bfloat16

D_MODEL = 1024
DEPTH = 4
CHUNK = 64
NORM_EPS = 1e-6
L2_EPS = 1e-12
RW_HEAD_DIM = 64
RW_HEADS = D_MODEL // RW_HEAD_DIM
RW_PAIRS = RW_HEADS // 2
R_DECAY = D_MODEL // 16
R_AAA = D_MODEL // 16
R_GATE = D_MODEL // 8
R_SMALL = R_DECAY + R_AAA + R_GATE
RW_GN_EPS = 64e-5
GDN_HEAD_DIM = 128
GDN_HEADS = D_MODEL // GDN_HEAD_DIM
D_FF = 4 * D_MODEL
LANES = 128
HALO = 8

COL_CONV = 0
COL_QKV = 3 * D_MODEL
COL_GATES = 6 * D_MODEL
COL_RKV = 9 * D_MODEL
COL_Z = 12 * D_MODEL
COL_SMALL = 13 * D_MODEL
COL_GAB = COL_SMALL + R_SMALL
P_USED = COL_GAB + LANES
PROJ_TN = 2048
P_PAD = ((P_USED + PROJ_TN - 1) // PROJ_TN) * PROJ_TN

VMEM_LIMIT = 56 * 1024 * 1024

_NN = (((1,), (0,)), ((), ()))
_NT = (((1,), (1,)), ((), ()))
_TN = (((0,), (0,)), ((), ()))


def _mm(a, b, dims=_NN):
    return lax.dot_general(a.astype(BF16), b.astype(BF16), dims, preferred_element_type=F32)


def _split2(x):
    hi = x.astype(BF16)
    lo = (x - hi.astype(F32)).astype(BF16)
    return hi, lo


def _split3(x):
    hi = x.astype(BF16)
    r1 = x - hi.astype(F32)
    mid = r1.astype(BF16)
    lo = (r1 - mid.astype(F32)).astype(BF16)
    return hi, mid, lo


def _mm3(a, b, dims=_NN):
    a_hi, a_lo = _split2(a)
    b_hi, b_lo = _split2(b)
    d = lambda x, y: lax.dot_general(x, y, dims, preferred_element_type=F32)
    return d(a_hi, b_hi) + (d(a_lo, b_hi) + d(a_hi, b_lo))


def _mm_exact_lhs(a_bf16, x, terms):
    parts = _split3(x) if terms == 3 else _split2(x)
    out = None
    for q in parts:
        t = lax.dot_general(a_bf16, q, _NN, preferred_element_type=F32)
        out = t if out is None else out + t
    return out


def _mm_exact_rhs(x, b_bf16):
    hi, lo = _split2(x)
    return (lax.dot_general(hi, b_bf16, _NN, preferred_element_type=F32)
            + lax.dot_general(lo, b_bf16, _NN, preferred_element_type=F32))


def _sigmoid(x):
    return 1.0 / (1.0 + jnp.exp(-x))


def _softplus(x):
    return jnp.maximum(x, 0.0) + jnp.log(1.0 + jnp.exp(-jnp.abs(x)))


def _silu(x):
    return x * _sigmoid(x)


def _rms(x, gain):
    return x * lax.rsqrt(jnp.mean(x * x, axis=-1, keepdims=True) + NORM_EPS) * gain


def _proj_kernel(x_ref, g_ref, w_ref, o_ref):
    h = _rms(x_ref[...], g_ref[...])
    o_ref[...] = jnp.dot(h.astype(BF16), w_ref[...], preferred_element_type=F32)


def _proj(x, gain, w_bf16, tm):
    rows = x.shape[0]
    return pl.pallas_call(
        _proj_kernel,
        grid=(P_PAD // PROJ_TN, rows // tm),
        in_specs=[
            pl.BlockSpec((tm, D_MODEL), lambda j, i: (i, 0)),
            pl.BlockSpec((1, D_MODEL), lambda j, i: (0, 0)),
            pl.BlockSpec((D_MODEL, PROJ_TN), lambda j, i: (0, j)),
        ],
        out_specs=pl.BlockSpec((tm, PROJ_TN), lambda j, i: (i, j)),
        out_shape=jax.ShapeDtypeStruct((rows, P_PAD), F32),
        compiler_params=pltpu.CompilerParams(
            dimension_semantics=("arbitrary", "arbitrary"), vmem_limit_bytes=VMEM_LIMIT),
        name="proj",
    )(x, gain, w_bf16)


def _outffn_kernel(m_ref, x_ref, wo_ref, w1_ref, w2_ref, g1_ref, g2_ref, g3_ref, o_ref):
    mo = jnp.dot(m_ref[...], wo_ref[...], preferred_element_type=F32)
    x = x_ref[...] + _rms(mo, g1_ref[...])
    h2 = _rms(x, g2_ref[...])
    f = jnp.dot(h2.astype(BF16), w1_ref[...], preferred_element_type=F32)
    f = jnp.square(jnp.maximum(f, 0.0))
    f = jnp.dot(f.astype(BF16), w2_ref[...], preferred_element_type=F32)
    o_ref[...] = x + _rms(f, g3_ref[...])


def _outffn(m, x, wo, w1, w2, g_post, g_pre2, g_post2, tm):
    rows = x.shape[0]
    const = lambda i: (0, 0)
    return pl.pallas_call(
        _outffn_kernel,
        grid=(rows // tm,),
        in_specs=[
            pl.BlockSpec((tm, D_MODEL), lambda i: (i, 0)),
            pl.BlockSpec((tm, D_MODEL), lambda i: (i, 0)),
            pl.BlockSpec((D_MODEL, D_MODEL), const),
            pl.BlockSpec((D_MODEL, D_FF), const),
            pl.BlockSpec((D_FF, D_MODEL), const),
            pl.BlockSpec((1, D_MODEL), const),
            pl.BlockSpec((1, D_MODEL), const),
            pl.BlockSpec((1, D_MODEL), const),
        ],
        out_specs=pl.BlockSpec((tm, D_MODEL), lambda i: (i, 0)),
        out_shape=jax.ShapeDtypeStruct((rows, D_MODEL), F32),
        compiler_params=pltpu.CompilerParams(
            dimension_semantics=("arbitrary",), vmem_limit_bytes=VMEM_LIMIT),
        name="outffn",
    )(m, x, wo, w1, w2, g_post, g_pre2, g_post2)


INV_BASE = 16


def _unit_lower_inverse(mats, size):
    base = min(size, INV_BASE)
    iotas = {}

    def rc(n):
        if n not in iotas:
            iotas[n] = (lax.broadcasted_iota(jnp.int32, (n, n), 0), lax.broadcasted_iota(jnp.int32, (n, n), 1))
        return iotas[n]

    def diag_blocks(a):
        row, col = rc(a.shape[0])
        return jnp.where((row // base) == (col // base), a, 0.0) if base < size else a

    def eye(a):
        row, col = rc(a.shape[0])
        return (row == col).astype(F32)

    ps = [diag_blocks(a) for a in mats]
    ts = [eye(p) - p for p in ps]
    for _ in range(base.bit_length() - 2):
        ps = [_mm(p, p) for p in ps]
        ts = [t + _mm(t, p) for t, p in zip(ts, ps)]
    b = base
    while b < size:
        def off_blocks(a):
            row, col = rc(a.shape[0])
            return jnp.where(((row // (2 * b)) == (col // (2 * b))) & ((row // b) != (col // b)), a, 0.0)
        lt = [_mm(off_blocks(a), t) for a, t in zip(mats, ts)]
        ts = [t - _mm(t, x) for t, x in zip(ts, lt)]
        b *= 2
    return ts


def _mixer_kernel(chunk,
                  conv_ref, qkv_ref, gates_ref, rkv_ref, z_ref, small_ref, gab_ref,
                  st_conva_ref, st_shrkv_ref, st_shsm_ref, st_wkv_ref, st_gconv_ref, st_ssm_ref,
                  convw_ref, mu_rkv_ref, mu_sm_ref, w0_ref, w2_ref, a0_ref, a2_ref, g2_ref,
                  kk_ref, ka_ref, rk_ref, lnw_ref, lnb_ref, gconvw_ref, alog_ref, dtb_ref, gnw_ref,
                  m_ref, o_conva_ref, o_shrkv_ref, o_shsm_ref, o_wkv_ref, o_gconv_ref, o_ssm_ref,
                  ext_conv, ext_qkv, ext_rkv, ext_sm, s_wkv, s_ssm):
    C = chunk
    t_idx = pl.program_id(1)
    n_t = pl.num_programs(1)

    @pl.when(t_idx == 0)
    def _init():
        ext_conv[0:HALO, :] = jnp.zeros((HALO, D_MODEL), F32)
        ext_conv[HALO - 2:HALO, :] = st_conva_ref[0]
        ext_qkv[0:HALO, :] = jnp.zeros((HALO, 3 * D_MODEL), F32)
        ext_qkv[HALO - 3:HALO, :] = st_gconv_ref[0]
        ext_rkv[0:HALO, :] = jnp.zeros((HALO, 3 * D_MODEL), F32)
        ext_rkv[HALO - 1:HALO, :] = st_shrkv_ref[0]
        ext_sm[0:HALO, :] = jnp.zeros((HALO, R_SMALL), F32)
        ext_sm[HALO - 1:HALO, :] = st_shsm_ref[0]
        s_wkv[...] = st_wkv_ref[0]
        s_ssm[...] = st_ssm_ref[0]

    row_c = lax.broadcasted_iota(jnp.int32, (C, C), 0)
    col_c = lax.broadcasted_iota(jnp.int32, (C, C), 1)
    incl_c = row_c >= col_c
    strict_c = row_c > col_c
    tri_incl = incl_c.astype(BF16)

    cb = conv_ref[:, 0:D_MODEL]
    u = conv_ref[:, D_MODEL:2 * D_MODEL] * conv_ref[:, 2 * D_MODEL:3 * D_MODEL]
    ext_conv[HALO:HALO + C, :] = u
    ya = cb * (convw_ref[0:1, :] * ext_conv[HALO - 2:HALO - 2 + C, :]
               + convw_ref[1:2, :] * ext_conv[HALO - 1:HALO - 1 + C, :]
               + convw_ref[2:3, :] * u)

    p_rkv = rkv_ref[...]
    ext_rkv[HALO:HALO + C, :] = p_rkv
    ps = p_rkv + (ext_rkv[HALO - 1:HALO - 1 + C, :] - p_rkv) * mu_rkv_ref[...]
    p_sm = small_ref[...]
    ext_sm[HALO:HALO + C, :] = p_sm
    ps_sm = p_sm + (ext_sm[HALO - 1:HALO - 1 + C, :] - p_sm) * mu_sm_ref[...]
    r = ps[:, 0:D_MODEL]
    k = ps[:, D_MODEL:2 * D_MODEL]
    v = ps[:, 2 * D_MODEL:3 * D_MODEL]
    xw = ps_sm[:, 0:R_DECAY]
    xa = ps_sm[:, R_DECAY:R_DECAY + R_AAA]
    xg = ps_sm[:, R_DECAY + R_AAA:R_SMALL]
    w_log = -_softplus(-(w0_ref[...] + _mm(jnp.tanh(xw), w2_ref[...]))) - 0.5
    a = _sigmoid(a0_ref[...] + _mm(xa, a2_ref[...]))
    g = _mm(_sigmoid(xg), g2_ref[...])
    logd = -jnp.exp(w_log)

    li = lax.broadcasted_iota(jnp.int32, (LANES, LANES), 0) // RW_HEAD_DIM
    lj = lax.broadcasted_iota(jnp.int32, (LANES, LANES), 1) // RW_HEAD_DIM
    bd128 = (li == lj).astype(F32)
    bd128_bf = bd128.astype(BF16)

    def seg_sum(x):
        return jnp.concatenate(
            [_mm_exact_rhs(x[:, j * LANES:(j + 1) * LANES], bd128_bf) for j in range(D_MODEL // LANES)], axis=1)

    kk0 = k * kk_ref[...]
    kk = kk0 * lax.rsqrt(seg_sum(kk0 * kk0) + L2_EPS)
    k2 = k * (1.0 + (a - 1.0) * ka_ref[...])
    kka = kk * a

    cum = _mm_exact_lhs(tri_incl, logd, 3)
    cum_prev = cum - logd
    cum_end = cum[C - 1:C, :]
    e_inv = jnp.exp(-cum)
    e_end = jnp.exp(cum_end - cum)
    kk_g = kk * jnp.exp(cum_prev)
    r_g = r * jnp.exp(cum)
    k_inv = k2 * e_inv
    kka_inv = kka * e_inv
    k_end = k2 * e_end
    kka_end = kka * e_end
    gam_end = jnp.exp(cum_end)

    lane = lax.broadcasted_iota(jnp.int32, (1, LANES), 1)
    m0 = (lane < RW_HEAD_DIM).astype(F32)
    m1 = 1.0 - m0
    n2 = 2 * C
    row_n = lax.broadcasted_iota(jnp.int32, (n2, n2), 0)
    col_n = lax.broadcasted_iota(jnp.int32, (n2, n2), 1)
    same_n = (row_n // C) == (col_n // C)
    strict_n = (same_n & (row_n > col_n)).astype(F32)
    incl_n = (same_n & (row_n >= col_n)).astype(F32)

    def stack(x):
        return jnp.concatenate([x * m0, x * m1], axis=0)

    qkv = qkv_ref[...]
    ext_qkv[HALO:HALO + C, :] = qkv
    cq = (gconvw_ref[0:1, :] * ext_qkv[HALO - 3:HALO - 3 + C, :]
          + gconvw_ref[1:2, :] * ext_qkv[HALO - 2:HALO - 2 + C, :]
          + gconvw_ref[2:3, :] * ext_qkv[HALO - 1:HALO - 1 + C, :]
          + gconvw_ref[3:4, :] * qkv)
    cq = _silu(cq)
    gab = gab_ref[...]
    g_log = -jnp.exp(alog_ref[...]) * _softplus(gab + dtb_ref[...])
    beta_all = _sigmoid(gab)
    g_cum = _mm_exact_lhs(tri_incl, g_log, 3)
    g_cum_t = g_cum.T
    z = z_ref[...]

    PR = range(RW_PAIRS)
    HD = range(GDN_HEADS)
    psl = [slice(p * LANES, (p + 1) * LANES) for p in PR]
    x1 = [stack(kk_g[:, s]) for s in psl]
    x2 = [stack(r_g[:, s]) for s in psl]
    yk = [stack(k_inv[:, s]) for s in psl]
    yq = [stack(kka_inv[:, s]) for s in psl]
    v_st = [stack(v[:, s]) for s in psl]

    qn, kn, vb, kb, decay, e_g, k_dec, blk_dec = [], [], [], [], [], [], [], []
    for h in HD:
        qh = cq[:, h * LANES:(h + 1) * LANES]
        kh = cq[:, D_MODEL + h * LANES:D_MODEL + (h + 1) * LANES]
        vh = cq[:, 2 * D_MODEL + h * LANES:2 * D_MODEL + (h + 1) * LANES]
        qn.append(qh * lax.rsqrt(jnp.sum(qh * qh, axis=-1, keepdims=True) + L2_EPS) * (GDN_HEAD_DIM ** -0.5))
        kn.append(kh * lax.rsqrt(jnp.sum(kh * kh, axis=-1, keepdims=True) + L2_EPS))
        gc = g_cum[:, h:h + 1]
        gr = g_cum_t[h:h + 1, :]
        decay.append(jnp.where(incl_c, jnp.exp(jnp.where(incl_c, gc - gr, 0.0)), 0.0))
        beta = beta_all[:, GDN_HEADS + h:GDN_HEADS + h + 1]
        kb.append(kn[h] * beta)
        vb.append(vh * beta)
        e_g.append(jnp.exp(gc))
        g_last = g_cum[C - 1:C, h:h + 1]
        k_dec.append(kn[h] * jnp.exp(g_last - gc))
        blk_dec.append(jnp.exp(g_last))

    a1 = [_mm(x1[p], yk[p], _NT) * strict_n for p in PR]
    a2 = [_mm(x1[p], yq[p], _NT) * strict_n for p in PR]
    a_g = [jnp.where(strict_c, _mm(kb[h], kn[h], _NT) * decay[h], 0.0) for h in HD]
    b1 = [_mm(x2[p], yk[p], _NT) * incl_n for p in PR]
    b2 = [_mm(x2[p], yq[p], _NT) * incl_n for p in PR]
    attn = [_mm(qn[h], kn[h], _NT) * decay[h] for h in HD]
    t_all = _unit_lower_inverse(a2 + a_g, C)
    t_rw, t_g = t_all[:RW_PAIRS], t_all[RW_PAIRS:]
    w_rw = [_mm(t_rw[p], x1[p]) for p in PR]
    av = [_mm(a1[p], v_st[p]) for p in PR]
    u_g = [_mm(t_g[h], vb[h]) for h in HD]
    w_g = [_mm(t_g[h], kb[h] * e_g[h]) for h in HD]
    u0 = [_mm(t_rw[p], av[p]) for p in PR]
    st_rw = [s_wkv[p] for p in PR]
    st_g = [s_ssm[h] for h in HD]
    u_st = [_mm(w_rw[p], st_rw[p], _NT) + u0[p] for p in PR]
    rs = [_mm(x2[p], st_rw[p], _NT) for p in PR]
    v_new = [u_g[h] - _mm(w_g[h], st_g[h]) for h in HD]
    qs = [_mm(qn[h] * e_g[h], st_g[h]) for h in HD]
    y_st = [rs[p] + _mm(b1[p], v_st[p]) - _mm(b2[p], u_st[p]) for p in PR]
    o_g = [qs[h] + _mm(attn[h], v_new[h]) for h in HD]
    for p in PR:
        u_un = u_st[p][0:C] + u_st[p][C:n2]
        upd = _mm(jnp.concatenate([v[:, psl[p]], -u_un], axis=0),
                  jnp.concatenate([k_end[:, psl[p]], kka_end[:, psl[p]]], axis=0), _TN)
        s_wkv[p] = st_rw[p] * gam_end[:, psl[p]] + upd * bd128
    for h in HD:
        s_ssm[h] = st_g[h] * blk_dec[h] + _mm(k_dec[h], v_new[h], _TN)

    y_rw = jnp.concatenate([y[0:C] + y[C:n2] for y in y_st], axis=1)
    inv_n = 1.0 / RW_HEAD_DIM
    y_c = y_rw - seg_sum(y_rw) * inv_n
    y_n = y_c * lax.rsqrt(seg_sum(y_c * y_c) * inv_n + RW_GN_EPS)
    y_n = y_n * lnw_ref[...] + lnb_ref[...]
    bonus = seg_sum(r * k2 * rk_ref[...]) * v
    yb = (y_n + bonus) * g

    yc = jnp.concatenate(
        [o * lax.rsqrt(jnp.mean(o * o, axis=-1, keepdims=True) + NORM_EPS) * gnw_ref[...]
         * _silu(z[:, h * LANES:(h + 1) * LANES]) for h, o in enumerate(o_g)], axis=1)

    m = (_sigmoid(gates_ref[:, 0:D_MODEL]) * ya
         + _sigmoid(gates_ref[:, D_MODEL:2 * D_MODEL]) * yb
         + _sigmoid(gates_ref[:, 2 * D_MODEL:3 * D_MODEL]) * yc)
    m_ref[...] = m.astype(m_ref.dtype)

    ext_conv[0:HALO, :] = ext_conv[C:C + HALO, :]
    ext_qkv[0:HALO, :] = ext_qkv[C:C + HALO, :]
    ext_rkv[0:HALO, :] = ext_rkv[C:C + HALO, :]
    ext_sm[0:HALO, :] = ext_sm[C:C + HALO, :]

    @pl.when(t_idx == n_t - 1)
    def _finish():
        o_conva_ref[0] = ext_conv[HALO - 2:HALO, :]
        o_gconv_ref[0] = ext_qkv[HALO - 3:HALO, :]
        o_shrkv_ref[0] = ext_rkv[HALO - 1:HALO, :]
        o_shsm_ref[0] = ext_sm[HALO - 1:HALO, :]
        o_wkv_ref[0] = s_wkv[...]
        o_ssm_ref[0] = s_ssm[...]


def _mixer(proj, states, prm, nb, t_len, chunk):
    n_t = t_len // chunk
    rows = nb * t_len

    def pspec(width, col):
        blk = col // width
        return pl.BlockSpec((chunk, width), lambda b, t: (b * n_t + t, blk))

    def sspec(shape):
        nd = len(shape)
        return pl.BlockSpec((1,) + tuple(shape[1:]), lambda b, t: (b,) + (0,) * (nd - 1))

    def wspec(arr):
        return pl.BlockSpec(arr.shape, lambda b, t: (0,) * arr.ndim)

    st_names = ("conva", "shrkv", "shsm", "wkv", "gconv", "ssm")
    st_list = [states[n] for n in st_names]
    w_names = ("conv_a_w", "mu_rkv", "mu_sm", "w0", "w2", "a0", "a2", "g2", "k_k", "k_a", "r_k",
               "ln_w", "ln_b", "gdn_conv_w", "a_log", "dt_bias", "gdn_norm_w")
    w_list = [prm[n] for n in w_names]
    in_specs = [
        pspec(3 * D_MODEL, COL_CONV), pspec(3 * D_MODEL, COL_QKV), pspec(3 * D_MODEL, COL_GATES),
        pspec(3 * D_MODEL, COL_RKV), pspec(D_MODEL, COL_Z), pspec(R_SMALL, COL_SMALL), pspec(LANES, COL_GAB),
    ] + [sspec(s.shape) for s in st_list] + [wspec(w) for w in w_list]
    out_shapes = [jax.ShapeDtypeStruct((rows, D_MODEL), BF16)] + [
        jax.ShapeDtypeStruct(s.shape, F32) for s in st_list]
    out_specs = [pl.BlockSpec((chunk, D_MODEL), lambda b, t: (b * n_t + t, 0))] + [
        sspec(s.shape) for s in st_list]
    scratch = [
        pltpu.VMEM((HALO + chunk, D_MODEL), F32),
        pltpu.VMEM((HALO + chunk, 3 * D_MODEL), F32),
        pltpu.VMEM((HALO + chunk, 3 * D_MODEL), F32),
        pltpu.VMEM((HALO + chunk, R_SMALL), F32),
        pltpu.VMEM((RW_PAIRS, LANES, LANES), F32),
        pltpu.VMEM((GDN_HEADS, GDN_HEAD_DIM, GDN_HEAD_DIM), F32),
    ]
    outs = pl.pallas_call(
        functools.partial(_mixer_kernel, chunk),
        grid=(nb, n_t),
        in_specs=in_specs,
        out_specs=out_specs,
        out_shape=out_shapes,
        scratch_shapes=scratch,
        compiler_params=pltpu.CompilerParams(
            dimension_semantics=("arbitrary", "arbitrary"), vmem_limit_bytes=VMEM_LIMIT),
        name="mixer",
    )(*([proj] * 7), *st_list, *w_list)
    return outs[0], dict(zip(st_names, outs[1:]))


def _prep_layer(l, norm_mix_pre, norm_mix_post, norm_mlp_pre, norm_mlp_post, w_in, conv_a_w, rwkv_mu, rwkv_w0,
                rwkv_w2, rwkv_a0, rwkv_a2, rwkv_g2, rwkv_k_k, rwkv_k_a, rwkv_r_k, rwkv_ln_w, rwkv_ln_b,
                gdn_conv_w, gdn_a_log, gdn_dt_bias, gdn_norm_w, w_o, w_ff1, w_ff2):
    d = D_MODEL
    wi = w_in[l]
    c_prw = 3 * d
    c_qkv = c_prw + 3 * d + R_SMALL
    c_z = c_qkv + 3 * d
    c_ga = c_z + d
    c_gates = c_ga + 2 * GDN_HEADS
    w_perm = jnp.concatenate([
        wi[:, 0:3 * d], wi[:, c_qkv:c_qkv + 3 * d], wi[:, c_gates:c_gates + 3 * d], wi[:, c_prw:c_prw + 3 * d],
        wi[:, c_z:c_z + d], wi[:, c_prw + 3 * d:c_prw + 3 * d + R_SMALL], wi[:, c_ga:c_ga + 2 * GDN_HEADS],
        jnp.zeros((d, P_PAD - COL_GAB - 2 * GDN_HEADS), wi.dtype)], axis=1).astype(BF16)
    row = lambda a: a.reshape(1, -1).astype(F32)
    pad_heads = lambda a: jnp.pad(a.astype(F32), (0, LANES - GDN_HEADS)).reshape(1, LANES)
    mu = rwkv_mu[l]
    return {
        "g_pre": row(norm_mix_pre[l]), "g_post": row(norm_mix_post[l]),
        "g_pre2": row(norm_mlp_pre[l]), "g_post2": row(norm_mlp_post[l]),
        "w_in": w_perm, "w_o": w_o[l].astype(BF16), "w_ff1": w_ff1[l].astype(BF16), "w_ff2": w_ff2[l].astype(BF16),
        "conv_a_w": conv_a_w[l], "mu_rkv": row(mu[0:3 * d]), "mu_sm": row(mu[3 * d:]),
        "w0": row(rwkv_w0[l]), "w2": rwkv_w2[l], "a0": row(rwkv_a0[l]), "a2": rwkv_a2[l], "g2": rwkv_g2[l],
        "k_k": row(rwkv_k_k[l]), "k_a": row(rwkv_k_a[l]), "r_k": row(rwkv_r_k[l]),
        "ln_w": row(rwkv_ln_w[l]), "ln_b": row(rwkv_ln_b[l]), "gdn_conv_w": gdn_conv_w[l],
        "a_log": pad_heads(gdn_a_log[l]), "dt_bias": pad_heads(gdn_dt_bias[l]), "gdn_norm_w": row(gdn_norm_w[l]),
    }


def _pack_wkv(s):
    b = s.shape[0]
    s = s.reshape(b, RW_PAIRS, 2, RW_HEAD_DIM, RW_HEAD_DIM)
    zero = jnp.zeros_like(s[:, :, 0])
    top = jnp.concatenate([s[:, :, 0], zero], axis=-1)
    bot = jnp.concatenate([zero, s[:, :, 1]], axis=-1)
    return jnp.concatenate([top, bot], axis=-2)


def _unpack_wkv(s):
    b = s.shape[0]
    h = RW_HEAD_DIM
    return jnp.stack([s[:, :, 0:h, 0:h], s[:, :, h:, h:]], axis=2).reshape(b, RW_HEADS, h, h)


def _layer(x, nb, t_len, chunk, st, prm, tm_proj, tm_ffn):
    conv_a, rw_shift, rw_wkv, gdn_conv, gdn_ssm = st
    states = {
        "conva": conv_a, "shrkv": rw_shift[:, None, 0:3 * D_MODEL], "shsm": rw_shift[:, None, 3 * D_MODEL:],
        "wkv": _pack_wkv(rw_wkv), "gconv": gdn_conv, "ssm": gdn_ssm,
    }
    proj = _proj(x, prm["g_pre"], prm["w_in"], tm_proj)
    m, new = _mixer(proj, states, prm, nb, t_len, chunk)
    x = _outffn(m, x, prm["w_o"], prm["w_ff1"], prm["w_ff2"], prm["g_post"], prm["g_pre2"], prm["g_post2"], tm_ffn)
    new_st = (new["conva"], jnp.concatenate([new["shrkv"][:, 0], new["shsm"][:, 0]], axis=-1),
              _unpack_wkv(new["wkv"]), new["gconv"], new["ssm"])
    return x, new_st


def _row_tile(rows, want):
    t = min(rows, want)
    while rows % t:
        t //= 2
    return t


def _forward(x_prompt, x_sample, sample_states, weights):
    bp, tp, d = x_prompt.shape
    bs, ts, _ = x_sample.shape
    zero_states = (jnp.zeros((bp, 2, d), F32), jnp.zeros((bp, 3 * d + R_SMALL), F32),
                   jnp.zeros((bp, RW_HEADS, RW_HEAD_DIM, RW_HEAD_DIM), F32),
                   jnp.zeros((bp, 3, 3 * d), F32), jnp.zeros((bp, GDN_HEADS, GDN_HEAD_DIM, GDN_HEAD_DIM), F32))
    y_p = x_prompt.reshape(bp * tp, d)
    y_s = x_sample.reshape(bs * ts, d)
    new_p = [[] for _ in range(5)]
    new_s = [[] for _ in range(5)]
    n_layers = weights[0].shape[0]
    for l in range(n_layers):
        prm = _prep_layer(l, *weights)
        y_p, st_p = _layer(y_p, bp, tp, min(CHUNK, tp), zero_states, prm,
                           _row_tile(bp * tp, 1024), _row_tile(bp * tp, 256))
        y_s, st_s = _layer(y_s, bs, ts, min(CHUNK, ts), tuple(s[l] for s in sample_states), prm,
                           _row_tile(bs * ts, 1024), _row_tile(bs * ts, 256))
        for i in range(5):
            new_p[i].append(st_p[i])
            new_s[i].append(st_s[i])
    outs_p = [jnp.stack(t, axis=0) for t in new_p]
    outs_s = [jnp.stack(t, axis=0) for t in new_s]
    return (y_p.reshape(bp, tp, d), y_s.reshape(bs, ts, d), *outs_p, *outs_s)


def kernel(x_prompt, x_sample, state_conv_a, state_rwkv_shift, state_rwkv_wkv, state_gdn_conv, state_gdn_ssm,
           norm_mix_pre, norm_mix_post, norm_mlp_pre, norm_mlp_post, w_in, conv_a_w, rwkv_mu, rwkv_w0, rwkv_w2,
           rwkv_a0, rwkv_a2, rwkv_g2, rwkv_k_k, rwkv_k_a, rwkv_r_k, rwkv_ln_w, rwkv_ln_b, gdn_conv_w, gdn_a_log,
           gdn_dt_bias, gdn_norm_w, w_o, w_ff1, w_ff2):
    sample_states = (state_conv_a, state_rwkv_shift, state_rwkv_wkv, state_gdn_conv, state_gdn_ssm)
    weights = (norm_mix_pre, norm_mix_post, norm_mlp_pre, norm_mlp_post, w_in, conv_a_w, rwkv_mu, rwkv_w0, rwkv_w2,
               rwkv_a0, rwkv_a2, rwkv_g2, rwkv_k_k, rwkv_k_a, rwkv_r_k, rwkv_ln_w, rwkv_ln_b, gdn_conv_w,
               gdn_a_log, gdn_dt_bias, gdn_norm_w, w_o, w_ff1, w_ff2)
    return _forward(x_prompt, x_sample, sample_states, weights)
```

```python
import functools

import jax
import jax.numpy as jnp
from jax import lax
from jax.experimental import pallas as pl
from jax.experimental.pallas import tpu as pltpu

F32 = jnp.float32
BF16 = jnp.bfloat16

D_MODEL = 1024
DEPTH = 4
CHUNK = 64
NORM_EPS = 1e-6
L2_EPS = 1e-12
RW_HEAD_DIM = 64
RW_HEADS = D_MODEL // RW_HEAD_DIM
RW_PAIRS = RW_HEADS // 2
R_DECAY = D_MODEL // 16
R_AAA = D_MODEL // 16
R_GATE = D_MODEL // 8
R_SMALL = R_DECAY + R_AAA + R_GATE
RW_GN_EPS = 64e-5
GDN_HEAD_DIM = 128
GDN_HEADS = D_MODEL // GDN_HEAD_DIM
D_FF = 4 * D_MODEL
LANES = 128
HALO = 8
MIXER_SUB = 4

COL_CONV = 0
COL_QKV = 3 * D_MODEL
COL_GATES = 6 * D_MODEL
COL_RKV = 9 * D_MODEL
COL_Z = 12 * D_MODEL
COL_SMALL = 13 * D_MODEL
COL_GAB = COL_SMALL + R_SMALL
P_USED = COL_GAB + LANES
PROJ_TN = 2048
P_PAD = ((P_USED + PROJ_TN - 1) // PROJ_TN) * PROJ_TN

VMEM_LIMIT = 56 * 1024 * 1024

_NN = (((1,), (0,)), ((), ()))
_NT = (((1,), (1,)), ((), ()))
_TN = (((0,), (0,)), ((), ()))


def _mm(a, b, dims=_NN):
    return lax.dot_general(a.astype(BF16), b.astype(BF16), dims, preferred_element_type=F32)


def _split2(x):
    hi = x.astype(BF16)
    lo = (x - hi.astype(F32)).astype(BF16)
    return hi, lo


def _split3(x):
    hi = x.astype(BF16)
    r1 = x - hi.astype(F32)
    mid = r1.astype(BF16)
    lo = (r1 - mid.astype(F32)).astype(BF16)
    return hi, mid, lo


def _mm_exact_lhs(a_bf16, x, terms):
    parts = _split3(x) if terms == 3 else _split2(x)
    out = None
    for q in parts:
        t = lax.dot_general(a_bf16, q, _NN, preferred_element_type=F32)
        out = t if out is None else out + t
    return out


def _mm_exact_rhs(x, b_bf16):
    hi, lo = _split2(x)
    return (lax.dot_general(hi, b_bf16, _NN, preferred_element_type=F32)
            + lax.dot_general(lo, b_bf16, _NN, preferred_element_type=F32))


def _sigmoid(x):
    return 1.0 / (1.0 + jnp.exp(-x))


def _softplus(x):
    return jnp.maximum(x, 0.0) + jnp.log(1.0 + jnp.exp(-jnp.abs(x)))


def _silu(x):
    return x * _sigmoid(x)


def _rms(x, gain):
    return x * lax.rsqrt(jnp.mean(x * x, axis=-1, keepdims=True) + NORM_EPS) * gain


def _proj_kernel(x_ref, g_ref, w_ref, o_ref):
    h = _rms(x_ref[...], g_ref[...])
    o_ref[...] = jnp.dot(h.astype(BF16), w_ref[...], preferred_element_type=F32)


def _proj(x, gain, w_bf16, tm):
    rows = x.shape[0]
    return pl.pallas_call(
        _proj_kernel,
        grid=(P_PAD // PROJ_TN, rows // tm),
        in_specs=[
            pl.BlockSpec((tm, D_MODEL), lambda j, i: (i, 0)),
            pl.BlockSpec((1, D_MODEL), lambda j, i: (0, 0)),
            pl.BlockSpec((D_MODEL, PROJ_TN), lambda j, i: (0, j)),
        ],
        out_specs=pl.BlockSpec((tm, PROJ_TN), lambda j, i: (i, j)),
        out_shape=jax.ShapeDtypeStruct((rows, P_PAD), F32),
        compiler_params=pltpu.CompilerParams(
            dimension_semantics=("arbitrary", "arbitrary"), vmem_limit_bytes=VMEM_LIMIT),
        name="proj",
    )(x, gain, w_bf16)


def _outffn_kernel(m_ref, x_ref, wo_ref, w1_ref, w2_ref, g1_ref, g2_ref, g3_ref, o_ref):
    mo = jnp.dot(m_ref[...], wo_ref[...], preferred_element_type=F32)
    x = x_ref[...] + _rms(mo, g1_ref[...])
    h2 = _rms(x, g2_ref[...])
    f = jnp.dot(h2.astype(BF16), w1_ref[...], preferred_element_type=F32)
    f = jnp.square(jnp.maximum(f, 0.0))
    f = jnp.dot(f.astype(BF16), w2_ref[...], preferred_element_type=F32)
    o_ref[...] = x + _rms(f, g3_ref[...])


def _outffn(m, x, wo, w1, w2, g_post, g_pre2, g_post2, tm):
    rows = x.shape[0]
    const = lambda i: (0, 0)
    return pl.pallas_call(
        _outffn_kernel,
        grid=(rows // tm,),
        in_specs=[
            pl.BlockSpec((tm, D_MODEL), lambda i: (i, 0)),
            pl.BlockSpec((tm, D_MODEL), lambda i: (i, 0)),
            pl.BlockSpec((D_MODEL, D_MODEL), const),
            pl.BlockSpec((D_MODEL, D_FF), const),
            pl.BlockSpec((D_FF, D_MODEL), const),
            pl.BlockSpec((1, D_MODEL), const),
            pl.BlockSpec((1, D_MODEL), const),
            pl.BlockSpec((1, D_MODEL), const),
        ],
        out_specs=pl.BlockSpec((tm, D_MODEL), lambda i: (i, 0)),
        out_shape=jax.ShapeDtypeStruct((rows, D_MODEL), F32),
        compiler_params=pltpu.CompilerParams(
            dimension_semantics=("arbitrary",), vmem_limit_bytes=VMEM_LIMIT),
        name="outffn",
    )(m, x, wo, w1, w2, g_post, g_pre2, g_post2)


INV_BASE = 16


def _wide_unit_lower_inverse(mats, size, stack_w, out):
    base = min(size, INV_BASE)
    row = lax.broadcasted_iota(jnp.int32, (size, 2 * size), 0)
    col = lax.broadcasted_iota(jnp.int32, (size, 2 * size), 1) % size
    eye = (row == col).astype(F32)
    ps = [jnp.where((row // base) == (col // base), a, 0.0) if base < size else a for a in mats]
    ts = [eye - p for p in ps]
    sps = [stack_w(p) for p in ps]
    for step in range(base.bit_length() - 2):
        ps = [_mm(p, sp) for p, sp in zip(ps, sps)]
        yield
        sps = [stack_w(p) for p in ps]
        ts = [t + _mm(t, sp) for t, sp in zip(ts, sps)]
        yield
    b = base
    while b < size:
        off = ((row // (2 * b)) == (col // (2 * b))) & ((row // b) != (col // b))
        lt = [_mm(jnp.where(off, a, 0.0), stack_w(t)) for a, t in zip(mats, ts)]
        yield
        ts = [t - _mm(t, stack_w(x)) for t, x in zip(ts, lt)]
        yield
        b *= 2
    out.extend(ts)


def _mixer_kernel(chunk, n_sub,
                  conv_ref, qkv_ref, gates_ref, rkv_ref, z_ref, small_ref, gab_ref,
                  st_conva_ref, st_shrkv_ref, st_shsm_ref, st_wkv_ref, st_gconv_ref, st_ssm_ref,
                  convw_ref, mu_rkv_ref, mu_sm_ref, w0_ref, w2_ref, a0_ref, a2_ref, g2_ref,
                  kk_ref, ka_ref, rk_ref, lnw_ref, lnb_ref, gconvw_ref, alog_ref, dtb_ref, gnw_ref,
                  m_ref, o_conva_ref, o_shrkv_ref, o_shsm_ref, o_wkv_ref, o_gconv_ref, o_ssm_ref,
                  ext_conv, ext_qkv, ext_rkv, ext_sm, s_wkv, s_ssm):
    C = chunk
    R = n_sub * C
    W2 = 2 * C
    t_idx = pl.program_id(1)
    n_t = pl.num_programs(1)

    @pl.when(t_idx == 0)
    def _init():
        ext_conv[0:HALO, :] = jnp.zeros((HALO, D_MODEL), F32)
        ext_conv[HALO - 2:HALO, :] = st_conva_ref[0]
        ext_qkv[0:HALO, :] = jnp.zeros((HALO, 3 * D_MODEL), F32)
        ext_qkv[HALO - 3:HALO, :] = st_gconv_ref[0]
        ext_rkv[0:HALO, :] = jnp.zeros((HALO, 3 * D_MODEL), F32)
        ext_rkv[HALO - 1:HALO, :] = st_shrkv_ref[0]
        ext_sm[0:HALO, :] = jnp.zeros((HALO, R_SMALL), F32)
        ext_sm[HALO - 1:HALO, :] = st_shsm_ref[0]
        s_wkv[...] = st_wkv_ref[0]
        s_ssm[...] = st_ssm_ref[0]

    tri_incl = (lax.broadcasted_iota(jnp.int32, (C, C), 0) >= lax.broadcasted_iota(jnp.int32, (C, C), 1)).astype(BF16)

    li = lax.broadcasted_iota(jnp.int32, (LANES, LANES), 0) // RW_HEAD_DIM
    lj = lax.broadcasted_iota(jnp.int32, (LANES, LANES), 1) // RW_HEAD_DIM
    bd128 = (li == lj).astype(F32)
    bd128_bf = bd128.astype(BF16)

    def seg_sum(x):
        return jnp.concatenate(
            [_mm_exact_rhs(x[:, j * LANES:(j + 1) * LANES], bd128_bf) for j in range(D_MODEL // LANES)], axis=1)

    t_w = lax.broadcasted_iota(jnp.int32, (C, W2), 0)
    s_w = lax.broadcasted_iota(jnp.int32, (C, W2), 1) % C
    strict_w = t_w > s_w
    incl_w = t_w >= s_w
    lane_h = lax.broadcasted_iota(jnp.int32, (1, LANES), 1)
    mh0 = (lane_h < RW_HEAD_DIM).astype(BF16)
    mh1 = (lane_h >= RW_HEAD_DIM).astype(BF16)
    lane_w = lax.broadcasted_iota(jnp.int32, (1, W2), 1)
    mw0 = (lane_w < C).astype(BF16)
    mw1 = (lane_w >= C).astype(BF16)

    def stack_h(x):
        xb = x.astype(BF16)
        return jnp.concatenate([xb * mh0, xb * mh1], axis=0)

    def stack_w(x):
        xb = x.astype(BF16)
        return jnp.concatenate([xb * mw0, xb * mw1], axis=0)

    def blockdiag2(x0, x1):
        z = jnp.zeros(x0.shape, BF16)
        return jnp.concatenate([jnp.concatenate([x0.astype(BF16), z], axis=1),
                                jnp.concatenate([z, x1.astype(BF16)], axis=1)], axis=0)

    def pair(xs, j):
        return jnp.concatenate([xs[2 * j], xs[2 * j + 1]], axis=1)

    PR = range(RW_PAIRS)
    HD = range(GDN_HEADS)
    GP = range(GDN_HEADS // 2)
    psl = [slice(p * LANES, (p + 1) * LANES) for p in PR]

    ext_conv[HALO:HALO + R, :] = conv_ref[:, D_MODEL:2 * D_MODEL] * conv_ref[:, 2 * D_MODEL:3 * D_MODEL]
    ext_rkv[HALO:HALO + R, :] = rkv_ref[...]
    ext_sm[HALO:HALO + R, :] = small_ref[...]
    ext_qkv[HALO:HALO + R, :] = qkv_ref[...]

    def prep(sub, pre):
        r0 = sub * C
        e0 = HALO + r0
        rows = slice(r0, r0 + C)

        def back(ext, k, c0, width):
            blk = ext[e0 - HALO:e0 + C, c0:c0 + width]
            return pltpu.roll(blk, k, 0)[HALO:HALO + C]

        pre["ya"] = conv_ref[rows, 0:D_MODEL] * (
            convw_ref[0:1, :] * back(ext_conv, 2, 0, D_MODEL)
            + convw_ref[1:2, :] * back(ext_conv, 1, 0, D_MODEL)
            + convw_ref[2:3, :] * ext_conv[e0:e0 + C, :]) * _sigmoid(gates_ref[rows, 0:D_MODEL])
        yield

        p_sm = small_ref[rows, :]
        ps_sm = p_sm + (back(ext_sm, 1, 0, R_SMALL) - p_sm) * mu_sm_ref[...]
        xw = ps_sm[:, 0:R_DECAY]
        xa = ps_sm[:, R_DECAY:R_DECAY + R_AAA]
        xg = ps_sm[:, R_DECAY + R_AAA:R_SMALL]
        w_log = -_softplus(-(w0_ref[...] + _mm(jnp.tanh(xw), w2_ref[...]))) - 0.5
        logd = -jnp.exp(w_log)
        yield
        cum = _mm_exact_lhs(tri_incl, logd, 3)
        a = _sigmoid(a0_ref[...] + _mm(xa, a2_ref[...]))
        pre["g"] = _mm(_sigmoid(xg), g2_ref[...])
        yield

        def shifted(c0):
            p = rkv_ref[rows, c0:c0 + D_MODEL]
            return p + (back(ext_rkv, 1, c0, D_MODEL) - p) * mu_rkv_ref[:, c0:c0 + D_MODEL]

        k = shifted(D_MODEL)
        yield
        kk0 = k * kk_ref[...]
        kk = kk0 * lax.rsqrt(seg_sum(kk0 * kk0) + L2_EPS)
        yield
        k2 = k * (1.0 + (a - 1.0) * ka_ref[...])
        kka = kk * a
        e_inv = jnp.exp(-cum)
        yield
        kk_g = (kk * jnp.exp(cum - logd)).astype(BF16)
        pre["kk_g"] = [kk_g[:, s] for s in psl]
        pre["kk_g_st"] = [stack_h(kk_g[:, s]) for s in psl]
        yield
        k_inv = (k2 * e_inv).astype(BF16)
        pre["k_inv"] = [k_inv[:, s] for s in psl]
        pre["k_inv_st"] = [stack_h(k_inv[:, s]) for s in psl]
        yield
        kka_inv = (kka * e_inv).astype(BF16)
        pre["kka_inv"] = [kka_inv[:, s] for s in psl]
        pre["kka_inv_st"] = [stack_h(kka_inv[:, s]) for s in psl]
        yield
        r = shifted(0)
        r_g = (r * jnp.exp(cum)).astype(BF16)
        pre["r_g"] = [r_g[:, s] for s in psl]
        yield
        v = shifted(2 * D_MODEL)
        pre["v"] = [v[:, s].astype(BF16) for s in psl]
        pre["v_st"] = [stack_h(v[:, s]) for s in psl]
        pre["gam_end"] = jnp.exp(cum[C - 1:C, :])
        yield
        pre["bonus"] = seg_sum(r * k2 * rk_ref[...]) * v
        yield

        gab = gab_ref[rows, :]
        g_log = -jnp.exp(alog_ref[...]) * _softplus(gab + dtb_ref[...])
        beta_all = _sigmoid(gab)
        g_cum = _mm_exact_lhs(tri_incl, g_log, 3)
        g_cum_t = g_cum.T
        yield

        def conv_silu(c0):
            cq = (gconvw_ref[0:1, c0:c0 + LANES] * back(ext_qkv, 3, c0, LANES)
                  + gconvw_ref[1:2, c0:c0 + LANES] * back(ext_qkv, 2, c0, LANES)
                  + gconvw_ref[2:3, c0:c0 + LANES] * back(ext_qkv, 1, c0, LANES)
                  + gconvw_ref[3:4, c0:c0 + LANES] * ext_qkv[e0:e0 + C, c0:c0 + LANES])
            return _silu(cq)

        qn, kn, vb, kb, kbe, qe, k_dec, blk_dec = [], [], [], [], [], [], [], []
        pre.update(qn=qn, kn=kn, vb=vb, kb=kb, kbe=kbe, qe=qe, k_dec=k_dec, blk_dec=blk_dec, decay_w=[])
        for h in HD:
            qh = conv_silu(h * LANES)
            kh = conv_silu(D_MODEL + h * LANES)
            vh = conv_silu(2 * D_MODEL + h * LANES)
            qn_h = qh * lax.rsqrt(jnp.sum(qh * qh, axis=-1, keepdims=True) + L2_EPS) * (GDN_HEAD_DIM ** -0.5)
            kn_h = kh * lax.rsqrt(jnp.sum(kh * kh, axis=-1, keepdims=True) + L2_EPS)
            gc = g_cum[:, h:h + 1]
            beta = beta_all[:, GDN_HEADS + h:GDN_HEADS + h + 1]
            e_g = jnp.exp(gc)
            g_last = g_cum[C - 1:C, h:h + 1]
            kb_h = kn_h * beta
            qn.append(qn_h.astype(BF16))
            kn.append(kn_h.astype(BF16))
            kb.append(kb_h.astype(BF16))
            vb.append((vh * beta).astype(BF16))
            kbe.append((kb_h * e_g).astype(BF16))
            qe.append((qn_h * e_g).astype(BF16))
            k_dec.append((kn_h * jnp.exp(g_last - gc)).astype(BF16))
            blk_dec.append(jnp.exp(g_last))
            if h % 2 == 1:
                gc_w = jnp.where(lane_w < C, g_cum[:, h - 1:h], gc)
                gr_w = jnp.concatenate([g_cum_t[h - 1:h, :], g_cum_t[h:h + 1, :]], axis=1)
                pre["decay_w"].append(jnp.where(incl_w, jnp.exp(jnp.where(incl_w, gc_w - gr_w, 0.0)), 0.0))
            yield
        pre["zg"] = _silu(z_ref[rows, :]) * _sigmoid(gates_ref[rows, 2 * D_MODEL:3 * D_MODEL])
        yield
        pre["gate_b"] = _sigmoid(gates_ref[rows, D_MODEL:2 * D_MODEL])
        yield

    def independent(pre, mid):
        kn_bd = [blockdiag2(pre["kn"][2 * j], pre["kn"][2 * j + 1]) for j in GP]
        a1 = [jnp.where(strict_w, _mm(pre["kk_g"][p], pre["k_inv_st"][p], _NT), 0.0) for p in PR]
        yield
        a2 = [jnp.where(strict_w, _mm(pre["kk_g"][p], pre["kka_inv_st"][p], _NT), 0.0) for p in PR]
        yield
        a_g = [jnp.where(strict_w, _mm(pair(pre["kb"], j), kn_bd[j], _NT) * pre["decay_w"][j], 0.0) for j in GP]
        yield
        mid["b1"] = [jnp.where(incl_w, _mm(pre["r_g"][p], pre["k_inv_st"][p], _NT), 0.0) for p in PR]
        yield
        mid["b2"] = [jnp.where(incl_w, _mm(pre["r_g"][p], pre["kka_inv_st"][p], _NT), 0.0) for p in PR]
        yield
        mid["attn"] = [_mm(pair(pre["qn"], j), kn_bd[j], _NT) * pre["decay_w"][j] for j in GP]
        yield
        t_all = []
        yield from _wide_unit_lower_inverse(a2 + a_g, C, stack_w, t_all)
        t_rw, t_g = t_all[:RW_PAIRS], t_all[RW_PAIRS:]
        mid["w_rw"] = [_mm(t_rw[p], pre["kk_g_st"][p]) for p in PR]
        yield
        av = [_mm(a1[p], pre["v_st"][p]) for p in PR]
        yield
        mid["u_g"] = [_mm(t_g[j], blockdiag2(pre["vb"][2 * j], pre["vb"][2 * j + 1])) for j in GP]
        yield
        mid["w_g"] = [_mm(t_g[j], blockdiag2(pre["kbe"][2 * j], pre["kbe"][2 * j + 1])) for j in GP]
        yield
        mid["u0"] = [_mm(t_rw[p], stack_h(av[p])) for p in PR]
        yield

    def dependent(pre, mid, res):
        st_rw = [s_wkv[p] for p in PR]
        st_g = [s_ssm[h] for h in HD]
        st_gbd = [blockdiag2(st_g[2 * j], st_g[2 * j + 1]) for j in GP]
        u = [_mm(mid["w_rw"][p], st_rw[p], _NT) + mid["u0"][p] for p in PR]
        yield
        rs = [_mm(pre["r_g"][p], st_rw[p], _NT) for p in PR]
        yield
        v_new = [mid["u_g"][j] - _mm(mid["w_g"][j], st_gbd[j]) for j in GP]
        yield
        qs = [_mm(pair(pre["qe"], j), st_gbd[j]) for j in GP]
        yield
        res["y"] = [rs[p] + _mm(mid["b1"][p], pre["v_st"][p]) - _mm(mid["b2"][p], stack_h(u[p])) for p in PR]
        yield
        res["o"] = [qs[j] + _mm(mid["attn"][j], blockdiag2(v_new[j][:, 0:LANES], v_new[j][:, LANES:2 * LANES]))
                    for j in GP]
        yield
        for p in PR:
            upd = _mm(jnp.concatenate([pre["v"][p], (-u[p]).astype(BF16)], axis=0),
                      jnp.concatenate([pre["k_inv"][p], pre["kka_inv"][p]], axis=0), _TN)
            s_wkv[p] = (st_rw[p] + upd * bd128) * pre["gam_end"][:, psl[p]]
        yield
        for h in HD:
            vn = v_new[h // 2][:, (h % 2) * LANES:(h % 2 + 1) * LANES]
            s_ssm[h] = st_g[h] * pre["blk_dec"][h] + _mm(pre["k_dec"][h], vn, _TN)
        yield

    def post(sub, pre, res):
        y_rw = jnp.concatenate(res["y"], axis=1)
        inv_n = 1.0 / RW_HEAD_DIM
        y_c = y_rw - seg_sum(y_rw) * inv_n
        yield
        y_n = y_c * lax.rsqrt(seg_sum(y_c * y_c) * inv_n + RW_GN_EPS)
        yield
        y_n = y_n * lnw_ref[...] + lnb_ref[...]
        yb = (y_n + pre["bonus"]) * pre["g"]
        yield
        o_all = jnp.concatenate(res["o"], axis=1)
        yc = jnp.concatenate(
            [o_all[:, s] * lax.rsqrt(jnp.mean(o_all[:, s] * o_all[:, s], axis=-1, keepdims=True) + NORM_EPS)
             * gnw_ref[...] for s in psl], axis=1)
        yield
        m = pre["ya"] + pre["gate_b"] * yb + yc * pre["zg"]
        m_ref[sub * C:(sub + 1) * C, :] = m.astype(m_ref.dtype)
        yield

    def weave(*gens):
        gens = [g for g in gens if g is not None]
        while gens:
            for g in list(gens):
                try:
                    next(g)
                except StopIteration:
                    gens.remove(g)

    pres = [dict() for _ in range(n_sub)]
    mids = [dict() for _ in range(n_sub)]
    ress = [dict() for _ in range(n_sub)]
    def chain(*gens):
        for g in gens:
            yield from g

    weave(prep(0, pres[0]))
    for sub in range(n_sub):
        matmuls = [independent(pres[sub], mids[sub])]
        vector = []
        if sub:
            matmuls.insert(0, dependent(pres[sub - 1], mids[sub - 1], ress[sub - 1]))
        if sub + 1 < n_sub:
            vector.append(prep(sub + 1, pres[sub + 1]))
        elif sub:
            weave(matmuls.pop(0))
        if sub:
            vector.append(post(sub - 1, pres[sub - 1], ress[sub - 1]))
        weave(chain(*matmuls), chain(*vector))
    weave(dependent(pres[-1], mids[-1], ress[-1]))
    weave(post(n_sub - 1, pres[-1], ress[-1]))

    ext_conv[0:HALO, :] = ext_conv[R:R + HALO, :]
    ext_qkv[0:HALO, :] = ext_qkv[R:R + HALO, :]
    ext_rkv[0:HALO, :] = ext_rkv[R:R + HALO, :]
    ext_sm[0:HALO, :] = ext_sm[R:R + HALO, :]

    @pl.when(t_idx == n_t - 1)
    def _finish():
        o_conva_ref[0] = ext_conv[HALO - 2:HALO, :]
        o_gconv_ref[0] = ext_qkv[HALO - 3:HALO, :]
        o_shrkv_ref[0] = ext_rkv[HALO - 1:HALO, :]
        o_shsm_ref[0] = ext_sm[HALO - 1:HALO, :]
        o_wkv_ref[0] = s_wkv[...]
        o_ssm_ref[0] = s_ssm[...]


def _mixer(proj, states, prm, nb, t_len, chunk):
    n_sub = MIXER_SUB if t_len % (MIXER_SUB * chunk) == 0 else 1
    blk_rows = n_sub * chunk
    n_t = t_len // blk_rows
    rows = nb * t_len

    def pspec(width, col):
        blk = col // width
        return pl.BlockSpec((blk_rows, width), lambda b, t: (b * n_t + t, blk))

    def sspec(shape):
        nd = len(shape)
        return pl.BlockSpec((1,) + tuple(shape[1:]), lambda b, t: (b,) + (0,) * (nd - 1))

    def wspec(arr):
        return pl.BlockSpec(arr.shape, lambda b, t: (0,) * arr.ndim)

    st_names = ("conva", "shrkv", "shsm", "wkv", "gconv", "ssm")
    st_list = [states[n] for n in st_names]
    w_names = ("conv_a_w", "mu_rkv", "mu_sm", "w0", "w2", "a0", "a2", "g2", "k_k", "k_a", "r_k",
               "ln_w", "ln_b", "gdn_conv_w", "a_log", "dt_bias", "gdn_norm_w")
    w_list = [prm[n] for n in w_names]
    in_specs = [
        pspec(3 * D_MODEL, COL_CONV), pspec(3 * D_MODEL, COL_QKV), pspec(3 * D_MODEL, COL_GATES),
        pspec(3 * D_MODEL, COL_RKV), pspec(D_MODEL, COL_Z), pspec(R_SMALL, COL_SMALL), pspec(LANES, COL_GAB),
    ] + [sspec(s.shape) for s in st_list] + [wspec(w) for w in w_list]
    out_shapes = [jax.ShapeDtypeStruct((rows, D_MODEL), BF16)] + [
        jax.ShapeDtypeStruct(s.shape, F32) for s in st_list]
    out_specs = [pl.BlockSpec((blk_rows, D_MODEL), lambda b, t: (b * n_t + t, 0))] + [
        sspec(s.shape) for s in st_list]
    scratch = [
        pltpu.VMEM((HALO + blk_rows, D_MODEL), F32),
        pltpu.VMEM((HALO + blk_rows, 3 * D_MODEL), F32),
        pltpu.VMEM((HALO + blk_rows, 3 * D_MODEL), F32),
        pltpu.VMEM((HALO + blk_rows, R_SMALL), F32),
        pltpu.VMEM((RW_PAIRS, LANES, LANES), F32),
        pltpu.VMEM((GDN_HEADS, GDN_HEAD_DIM, GDN_HEAD_DIM), F32),
    ]
    outs = pl.pallas_call(
        functools.partial(_mixer_kernel, chunk, n_sub),
        grid=(nb, n_t),
        in_specs=in_specs,
        out_specs=out_specs,
        out_shape=out_shapes,
        scratch_shapes=scratch,
        compiler_params=pltpu.CompilerParams(
            dimension_semantics=("arbitrary", "arbitrary"), vmem_limit_bytes=VMEM_LIMIT),
        name="mixer",
    )(*([proj] * 7), *st_list, *w_list)
    return outs[0], dict(zip(st_names, outs[1:]))


def _prep_layer(l, norm_mix_pre, norm_mix_post, norm_mlp_pre, norm_mlp_post, w_in, conv_a_w, rwkv_mu, rwkv_w0,
                rwkv_w2, rwkv_a0, rwkv_a2, rwkv_g2, rwkv_k_k, rwkv_k_a, rwkv_r_k, rwkv_ln_w, rwkv_ln_b,
                gdn_conv_w, gdn_a_log, gdn_dt_bias, gdn_norm_w, w_o, w_ff1, w_ff2):
    d = D_MODEL
    wi = w_in[l].astype(BF16)
    c_prw = 3 * d
    c_qkv = c_prw + 3 * d + R_SMALL
    c_z = c_qkv + 3 * d
    c_ga = c_z + d
    c_gates = c_ga + 2 * GDN_HEADS
    w_perm = jnp.concatenate([
        wi[:, 0:3 * d], wi[:, c_qkv:c_qkv + 3 * d], wi[:, c_gates:c_gates + 3 * d], wi[:, c_prw:c_prw + 3 * d],
        wi[:, c_z:c_z + d], wi[:, c_prw + 3 * d:c_prw + 3 * d + R_SMALL], wi[:, c_ga:c_ga + 2 * GDN_HEADS],
        jnp.zeros((d, P_PAD - COL_GAB - 2 * GDN_HEADS), BF16)], axis=1)
    row = lambda a: a.reshape(1, -1).astype(F32)
    pad_heads = lambda a: jnp.pad(a.astype(F32), (0, LANES - GDN_HEADS)).reshape(1, LANES)
    mu = rwkv_mu[l]
    return {
        "g_pre": row(norm_mix_pre[l]), "g_post": row(norm_mix_post[l]),
        "g_pre2": row(norm_mlp_pre[l]), "g_post2": row(norm_mlp_post[l]),
        "w_in": w_perm, "w_o": w_o[l].astype(BF16), "w_ff1": w_ff1[l].astype(BF16), "w_ff2": w_ff2[l].astype(BF16),
        "conv_a_w": conv_a_w[l], "mu_rkv": row(mu[0:3 * d]), "mu_sm": row(mu[3 * d:]),
        "w0": row(rwkv_w0[l]), "w2": rwkv_w2[l].astype(BF16), "a0": row(rwkv_a0[l]),
        "a2": rwkv_a2[l].astype(BF16), "g2": rwkv_g2[l].astype(BF16),
        "k_k": row(rwkv_k_k[l]), "k_a": row(rwkv_k_a[l]), "r_k": row(rwkv_r_k[l]),
        "ln_w": row(rwkv_ln_w[l]), "ln_b": row(rwkv_ln_b[l]), "gdn_conv_w": gdn_conv_w[l],
        "a_log": pad_heads(gdn_a_log[l]), "dt_bias": pad_heads(gdn_dt_bias[l]), "gdn_norm_w": row(gdn_norm_w[l]),
    }


def _pack_wkv(s):
    b = s.shape[0]
    s = s.reshape(b, RW_PAIRS, 2, RW_HEAD_DIM, RW_HEAD_DIM)
    zero = jnp.zeros_like(s[:, :, 0])
    top = jnp.concatenate([s[:, :, 0], zero], axis=-1)
    bot = jnp.concatenate([zero, s[:, :, 1]], axis=-1)
    return jnp.concatenate([top, bot], axis=-2)


def _unpack_wkv(s):
    b = s.shape[0]
    h = RW_HEAD_DIM
    return jnp.stack([s[:, :, 0:h, 0:h], s[:, :, h:, h:]], axis=2).reshape(b, RW_HEADS, h, h)


def _layer(x, nb, t_len, chunk, st, prm, tm_proj, tm_ffn):
    conv_a, rw_shift, rw_wkv, gdn_conv, gdn_ssm = st
    states = {
        "conva": conv_a, "shrkv": rw_shift[:, None, 0:3 * D_MODEL], "shsm": rw_shift[:, None, 3 * D_MODEL:],
        "wkv": _pack_wkv(rw_wkv), "gconv": gdn_conv, "ssm": gdn_ssm,
    }
    proj = _proj(x, prm["g_pre"], prm["w_in"], tm_proj)
    m, new = _mixer(proj, states, prm, nb, t_len, chunk)
    x = _outffn(m, x, prm["w_o"], prm["w_ff1"], prm["w_ff2"], prm["g_post"], prm["g_pre2"], prm["g_post2"], tm_ffn)
    new_st = (new["conva"], jnp.concatenate([new["shrkv"][:, 0], new["shsm"][:, 0]], axis=-1),
              _unpack_wkv(new["wkv"]), new["gconv"], new["ssm"])
    return x, new_st


def _row_tile(rows, want):
    t = min(rows, want)
    while rows % t:
        t //= 2
    return t


def _forward(x_prompt, x_sample, sample_states, weights):
    bp, tp, d = x_prompt.shape
    bs, ts, _ = x_sample.shape
    zero_states = (jnp.zeros((bp, 2, d), F32), jnp.zeros((bp, 3 * d + R_SMALL), F32),
                   jnp.zeros((bp, RW_HEADS, RW_HEAD_DIM, RW_HEAD_DIM), F32),
                   jnp.zeros((bp, 3, 3 * d), F32), jnp.zeros((bp, GDN_HEADS, GDN_HEAD_DIM, GDN_HEAD_DIM), F32))
    y_p = x_prompt.reshape(bp * tp, d)
    y_s = x_sample.reshape(bs * ts, d)
    new_p = [[] for _ in range(5)]
    new_s = [[] for _ in range(5)]
    n_layers = weights[0].shape[0]
    for l in range(n_layers):
        prm = _prep_layer(l, *weights)
        y_p, st_p = _layer(y_p, bp, tp, min(CHUNK, tp), zero_states, prm,
                           _row_tile(bp * tp, 1024), _row_tile(bp * tp, 256))
        y_s, st_s = _layer(y_s, bs, ts, min(CHUNK, ts), tuple(s[l] for s in sample_states), prm,
                           _row_tile(bs * ts, 1024), _row_tile(bs * ts, 256))
        for i in range(5):
            new_p[i].append(st_p[i])
            new_s[i].append(st_s[i])
    outs_p = [jnp.stack(t, axis=0) for t in new_p]
    outs_s = [jnp.stack(t, axis=0) for t in new_s]
    return (y_p.reshape(bp, tp, d), y_s.reshape(bs, ts, d), *outs_p, *outs_s)


def kernel(x_prompt, x_sample, state_conv_a, state_rwkv_shift, state_rwkv_wkv, state_gdn_conv, state_gdn_ssm,
           norm_mix_pre, norm_mix_post, norm_mlp_pre, norm_mlp_post, w_in, conv_a_w, rwkv_mu, rwkv_w0, rwkv_w2,
           rwkv_a0, rwkv_a2, rwkv_g2, rwkv_k_k, rwkv_k_a, rwkv_r_k, rwkv_ln_w, rwkv_ln_b, gdn_conv_w, gdn_a_log,
           gdn_dt_bias, gdn_norm_w, w_o, w_ff1, w_ff2):
    sample_states = (state_conv_a, state_rwkv_shift, state_rwkv_wkv, state_gdn_conv, state_gdn_ssm)
    weights = (norm_mix_pre, norm_mix_post, norm_mlp_pre, norm_mlp_post, w_in, conv_a_w, rwkv_mu, rwkv_w0, rwkv_w2,
               rwkv_a0, rwkv_a2, rwkv_g2, rwkv_k_k, rwkv_k_a, rwkv_r_k, rwkv_ln_w, rwkv_ln_b, gdn_conv_w,
               gdn_a_log, gdn_dt_bias, gdn_norm_w, w_o, w_ff1, w_ff2)
    return _forward(x_prompt, x_sample, sample_states, weights)
```

```python
import functools

import jax
import jax.numpy as jnp
from jax import lax
from jax.experimental import pallas as pl
from jax.experimental.pallas import tpu as pltpu

F32 = jnp.float32
BF16 = jnp.bfloat16

D_MODEL = 1024
DEPTH = 4
CHUNK = 64
NORM_EPS = 1e-6
L2_EPS = 1e-12
RW_HEAD_DIM = 64
RW_HEADS = D_MODEL // RW_HEAD_DIM
RW_PAIRS = RW_HEADS // 2
R_DECAY = D_MODEL // 16
R_AAA = D_MODEL // 16
R_GATE = D_MODEL // 8
R_SMALL = R_DECAY + R_AAA + R_GATE
RW_GN_EPS = 64e-5
GDN_HEAD_DIM = 128
GDN_HEADS = D_MODEL // GDN_HEAD_DIM
D_FF = 4 * D_MODEL
LANES = 128
HALO = 8
MIXER_SUB = 4

COL_CONV = 0
COL_QKV = 3 * D_MODEL
COL_GATES = 6 * D_MODEL
COL_RKV = 9 * D_MODEL
COL_Z = 12 * D_MODEL
COL_SMALL = 13 * D_MODEL
COL_GAB = COL_SMALL + R_SMALL
P_USED = COL_GAB + LANES
PROJ_TN = 1536
P_PAD = ((P_USED + PROJ_TN - 1) // PROJ_TN) * PROJ_TN

VMEM_LIMIT = 56 * 1024 * 1024

_NN = (((1,), (0,)), ((), ()))
_NT = (((1,), (1,)), ((), ()))
_TN = (((0,), (0,)), ((), ()))


def _mm(a, b, dims=_NN):
    return lax.dot_general(a.astype(BF16), b.astype(BF16), dims, preferred_element_type=F32)


def _split3(x):
    hi = x.astype(BF16)
    r1 = x - hi.astype(F32)
    mid = r1.astype(BF16)
    lo = (r1 - mid.astype(F32)).astype(BF16)
    return hi, mid, lo


def _mm_exact_lhs(a_bf16, x):
    out = None
    for q in _split3(x):
        t = lax.dot_general(a_bf16, q, _NN, preferred_element_type=F32)
        out = t if out is None else out + t
    return out


def _sigmoid(x):
    return 1.0 / (1.0 + jnp.exp(-x))


def _softplus(x):
    return jnp.maximum(x, 0.0) + jnp.log(1.0 + jnp.exp(-jnp.abs(x)))


def _silu(x):
    return x * _sigmoid(x)


def _rms(x, gain):
    return x * lax.rsqrt(jnp.mean(x * x, axis=-1, keepdims=True) + NORM_EPS) * gain


def _proj_kernel(x_ref, g_ref, w_ref, o_ref):
    h = _rms(x_ref[...], g_ref[...])
    o_ref[...] = jnp.dot(h.astype(BF16), w_ref[...], preferred_element_type=F32)


def _proj(x, gain, w_bf16, tm):
    rows = x.shape[0]
    return pl.pallas_call(
        _proj_kernel,
        grid=(P_PAD // PROJ_TN, rows // tm),
        in_specs=[
            pl.BlockSpec((tm, D_MODEL), lambda j, i: (i, 0)),
            pl.BlockSpec((1, D_MODEL), lambda j, i: (0, 0)),
            pl.BlockSpec((D_MODEL, PROJ_TN), lambda j, i: (0, j)),
        ],
        out_specs=pl.BlockSpec((tm, PROJ_TN), lambda j, i: (i, j)),
        out_shape=jax.ShapeDtypeStruct((rows, P_PAD), F32),
        compiler_params=pltpu.CompilerParams(
            dimension_semantics=("arbitrary", "arbitrary"), vmem_limit_bytes=VMEM_LIMIT),
        name="proj",
    )(x, gain, w_bf16)


def _outffn_kernel(m_ref, x_ref, wo_ref, w1_ref, w2_ref, g1_ref, g2_ref, g3_ref, o_ref):
    mo = jnp.dot(m_ref[...], wo_ref[...], preferred_element_type=F32)
    x = x_ref[...] + _rms(mo, g1_ref[...])
    h2 = _rms(x, g2_ref[...])
    f = jnp.dot(h2.astype(BF16), w1_ref[...], preferred_element_type=F32)
    f = jnp.square(jnp.maximum(f, 0.0))
    f = jnp.dot(f.astype(BF16), w2_ref[...], preferred_element_type=F32)
    o_ref[...] = x + _rms(f, g3_ref[...])


def _outffn(m, x, wo, w1, w2, g_post, g_pre2, g_post2, tm):
    rows = x.shape[0]
    const = lambda i: (0, 0)
    return pl.pallas_call(
        _outffn_kernel,
        grid=(rows // tm,),
        in_specs=[
            pl.BlockSpec((tm, D_MODEL), lambda i: (i, 0)),
            pl.BlockSpec((tm, D_MODEL), lambda i: (i, 0)),
            pl.BlockSpec((D_MODEL, D_MODEL), const),
            pl.BlockSpec((D_MODEL, D_FF), const),
            pl.BlockSpec((D_FF, D_MODEL), const),
            pl.BlockSpec((1, D_MODEL), const),
            pl.BlockSpec((1, D_MODEL), const),
            pl.BlockSpec((1, D_MODEL), const),
        ],
        out_specs=pl.BlockSpec((tm, D_MODEL), lambda i: (i, 0)),
        out_shape=jax.ShapeDtypeStruct((rows, D_MODEL), F32),
        compiler_params=pltpu.CompilerParams(
            dimension_semantics=("arbitrary",), vmem_limit_bytes=VMEM_LIMIT),
        name="outffn",
    )(m, x, wo, w1, w2, g_post, g_pre2, g_post2)


INV_BASE = 16


def _wide_unit_lower_inverse(mats, size, stack_w, out):
    base = min(size, INV_BASE)
    row = lax.broadcasted_iota(jnp.int32, (size, 2 * size), 0)
    col = lax.broadcasted_iota(jnp.int32, (size, 2 * size), 1) % size
    eye = (row == col).astype(F32)
    ps = [jnp.where((row // base) == (col // base), a, 0.0) if base < size else a for a in mats]
    ts = [eye - p for p in ps]
    steps = base.bit_length() - 2
    if steps:
        ps = [_mm(p, stack_w(p)) for p in ps]
        yield
    for step in range(steps):
        sps = [stack_w(p) for p in ps]
        ts = [t + _mm(t, sp) for t, sp in zip(ts, sps)]
        if step + 1 < steps:
            ps = [_mm(p, sp) for p, sp in zip(ps, sps)]
        yield
    b = base
    while b < size:
        off = ((row // (2 * b)) == (col // (2 * b))) & ((row // b) != (col // b))
        lt = [_mm(jnp.where(off, a, 0.0), stack_w(t)) for a, t in zip(mats, ts)]
        yield
        ts = [t - _mm(t, stack_w(x)) for t, x in zip(ts, lt)]
        yield
        b *= 2
    out.extend(ts)


def _mixer_kernel(chunk, n_sub,
                  conv_ref, qkv_ref, gates_ref, rkv_ref, z_ref, small_ref, gab_ref,
                  st_conva_ref, st_shrkv_ref, st_shsm_ref, st_wkv_ref, st_gconv_ref, st_ssm_ref,
                  convw_ref, mu_rkv_ref, mu_sm_ref, w0_ref, w2_ref, a0_ref, a2_ref, g2_ref,
                  kk_ref, ka_ref, rk_ref, lnw_ref, lnb_ref, gconvw_ref, alog_ref, dtb_ref, gnw_ref,
                  m_ref, o_conva_ref, o_shrkv_ref, o_shsm_ref, o_wkv_ref, o_gconv_ref, o_ssm_ref,
                  ext_conv, ext_qkv, ext_rkv, ext_sm, s_wkv, s_ssm):
    C = chunk
    R = n_sub * C
    W2 = 2 * C
    t_idx = pl.program_id(1)
    n_t = pl.num_programs(1)

    @pl.when(t_idx == 0)
    def _init():
        ext_conv[0:HALO, :] = jnp.zeros((HALO, D_MODEL), F32)
        ext_conv[HALO - 2:HALO, :] = st_conva_ref[0]
        ext_qkv[0:HALO, :] = jnp.zeros((HALO, 3 * D_MODEL), F32)
        ext_qkv[HALO - 3:HALO, :] = st_gconv_ref[0]
        ext_rkv[0:HALO, :] = jnp.zeros((HALO, 3 * D_MODEL), F32)
        ext_rkv[HALO - 1:HALO, :] = st_shrkv_ref[0]
        ext_sm[0:HALO, :] = jnp.zeros((HALO, R_SMALL), F32)
        ext_sm[HALO - 1:HALO, :] = st_shsm_ref[0]
        s_wkv[...] = st_wkv_ref[0]
        s_ssm[...] = st_ssm_ref[0]

    tri_incl = (lax.broadcasted_iota(jnp.int32, (C, C), 0) >= lax.broadcasted_iota(jnp.int32, (C, C), 1)).astype(BF16)

    li = lax.broadcasted_iota(jnp.int32, (LANES, LANES), 0) // RW_HEAD_DIM
    lj = lax.broadcasted_iota(jnp.int32, (LANES, LANES), 1) // RW_HEAD_DIM
    bd128 = (li == lj).astype(F32)
    bd128_bf = bd128.astype(BF16)

    def seg_sum(x):
        xb = x.astype(BF16)
        return jnp.concatenate(
            [lax.dot_general(xb[:, j * LANES:(j + 1) * LANES], bd128_bf, _NN, preferred_element_type=F32)
             for j in range(D_MODEL // LANES)], axis=1)

    t_w = lax.broadcasted_iota(jnp.int32, (C, W2), 0)
    s_w = lax.broadcasted_iota(jnp.int32, (C, W2), 1) % C
    strict_w = t_w > s_w
    incl_w = t_w >= s_w
    lane_h = lax.broadcasted_iota(jnp.int32, (1, LANES), 1)
    mh0 = (lane_h < RW_HEAD_DIM).astype(BF16)
    mh1 = (lane_h >= RW_HEAD_DIM).astype(BF16)
    lane_w = lax.broadcasted_iota(jnp.int32, (1, W2), 1)
    mw0 = (lane_w < C).astype(BF16)
    mw1 = (lane_w >= C).astype(BF16)

    def stack_h(x):
        xb = x.astype(BF16)
        return jnp.concatenate([xb * mh0, xb * mh1], axis=0)

    def stack_w(x):
        xb = x.astype(BF16)
        return jnp.concatenate([xb * mw0, xb * mw1], axis=0)

    def blockdiag2(x0, x1):
        z = jnp.zeros(x0.shape, BF16)
        return jnp.concatenate([jnp.concatenate([x0.astype(BF16), z], axis=1),
                                jnp.concatenate([z, x1.astype(BF16)], axis=1)], axis=0)

    def pair(xs, j):
        return jnp.concatenate([xs[2 * j], xs[2 * j + 1]], axis=1)

    PR = range(RW_PAIRS)
    HD = range(GDN_HEADS)
    GP = range(GDN_HEADS // 2)
    psl = [slice(p * LANES, (p + 1) * LANES) for p in PR]

    ext_conv[HALO:HALO + R, :] = conv_ref[:, D_MODEL:2 * D_MODEL] * conv_ref[:, 2 * D_MODEL:3 * D_MODEL]
    ext_rkv[HALO:HALO + R, :] = rkv_ref[...]
    ext_sm[HALO:HALO + R, :] = small_ref[...]
    ext_qkv[HALO:HALO + R, :] = qkv_ref[...]

    def prep(sub, pre):
        r0 = sub * C
        e0 = HALO + r0
        rows = slice(r0, r0 + C)

        def back(ext, k, c0, width):
            blk = ext[e0 - HALO:e0 + C, c0:c0 + width]
            return pltpu.roll(blk, k, 0)[HALO:HALO + C]

        pre["ya"] = conv_ref[rows, 0:D_MODEL] * (
            convw_ref[0:1, :] * back(ext_conv, 2, 0, D_MODEL)
            + convw_ref[1:2, :] * back(ext_conv, 1, 0, D_MODEL)
            + convw_ref[2:3, :] * ext_conv[e0:e0 + C, :]) * _sigmoid(gates_ref[rows, 0:D_MODEL])
        yield

        p_sm = small_ref[rows, :]
        ps_sm = p_sm + (back(ext_sm, 1, 0, R_SMALL) - p_sm) * mu_sm_ref[...]
        xw = ps_sm[:, 0:R_DECAY]
        xa = ps_sm[:, R_DECAY:R_DECAY + R_AAA]
        xg = ps_sm[:, R_DECAY + R_AAA:R_SMALL]
        w_log = -_softplus(-(w0_ref[...] + _mm(jnp.tanh(xw), w2_ref[...]))) - 0.5
        logd = -jnp.exp(w_log)
        yield
        cum = _mm_exact_lhs(tri_incl, logd)
        a = _sigmoid(a0_ref[...] + _mm(xa, a2_ref[...]))
        pre["g"] = _mm(_sigmoid(xg), g2_ref[...])
        yield

        def shifted(c0):
            p = rkv_ref[rows, c0:c0 + D_MODEL]
            return p + (back(ext_rkv, 1, c0, D_MODEL) - p) * mu_rkv_ref[:, c0:c0 + D_MODEL]

        k = shifted(D_MODEL)
        yield
        kk0 = k * kk_ref[...]
        kk = kk0 * lax.rsqrt(seg_sum(kk0 * kk0) + L2_EPS)
        yield
        k2 = k * (1.0 + (a - 1.0) * ka_ref[...])
        kka = kk * a
        e_inv = jnp.exp(-cum)
        yield
        kk_g = (kk * jnp.exp(cum - logd)).astype(BF16)
        pre["kk_g"] = [kk_g[:, s] for s in psl]
        pre["kk_g_st"] = [stack_h(kk_g[:, s]) for s in psl]
        yield
        k_inv = (k2 * e_inv).astype(BF16)
        pre["k_inv"] = [k_inv[:, s] for s in psl]
        pre["k_inv_st"] = [stack_h(k_inv[:, s]) for s in psl]
        yield
        kka_inv = (kka * e_inv).astype(BF16)
        pre["kka_inv"] = [kka_inv[:, s] for s in psl]
        pre["kka_inv_st"] = [stack_h(kka_inv[:, s]) for s in psl]
        yield
        r = shifted(0)
        r_g = (r * jnp.exp(cum)).astype(BF16)
        pre["r_g"] = [r_g[:, s] for s in psl]
        yield
        v = shifted(2 * D_MODEL)
        pre["v"] = [v[:, s].astype(BF16) for s in psl]
        pre["v_st"] = [stack_h(v[:, s]) for s in psl]
        pre["gam_end"] = jnp.exp(cum[C - 1:C, :])
        yield
        pre["bonus"] = seg_sum(r * k2 * rk_ref[...]) * v
        yield

        gab = gab_ref[rows, :]
        g_log = -jnp.exp(alog_ref[...]) * _softplus(gab + dtb_ref[...])
        beta_all = _sigmoid(gab)
        g_cum = _mm_exact_lhs(tri_incl, g_log)
        g_cum_t = g_cum.T
        yield

        def conv_silu(c0):
            cq = (gconvw_ref[0:1, c0:c0 + LANES] * back(ext_qkv, 3, c0, LANES)
                  + gconvw_ref[1:2, c0:c0 + LANES] * back(ext_qkv, 2, c0, LANES)
                  + gconvw_ref[2:3, c0:c0 + LANES] * back(ext_qkv, 1, c0, LANES)
                  + gconvw_ref[3:4, c0:c0 + LANES] * ext_qkv[e0:e0 + C, c0:c0 + LANES])
            return _silu(cq)

        qn, kn, vb, kb, kbe, qe, k_dec, blk_dec = [], [], [], [], [], [], [], []
        pre.update(qn=qn, kn=kn, vb=vb, kb=kb, kbe=kbe, qe=qe, k_dec=k_dec, blk_dec=blk_dec, decay_w=[])
        for h in HD:
            qh = conv_silu(h * LANES)
            kh = conv_silu(D_MODEL + h * LANES)
            vh = conv_silu(2 * D_MODEL + h * LANES)
            qn_h = qh * lax.rsqrt(jnp.sum(qh * qh, axis=-1, keepdims=True) + L2_EPS) * (GDN_HEAD_DIM ** -0.5)
            kn_h = kh * lax.rsqrt(jnp.sum(kh * kh, axis=-1, keepdims=True) + L2_EPS)
            gc = g_cum[:, h:h + 1]
            beta = beta_all[:, GDN_HEADS + h:GDN_HEADS + h + 1]
            e_g = jnp.exp(gc)
            g_last = g_cum[C - 1:C, h:h + 1]
            kb_h = kn_h * beta
            qn.append(qn_h.astype(BF16))
            kn.append(kn_h.astype(BF16))
            kb.append(kb_h.astype(BF16))
            vb.append((vh * beta).astype(BF16))
            kbe.append((kb_h * e_g).astype(BF16))
            qe.append((qn_h * e_g).astype(BF16))
            k_dec.append((kn_h * jnp.exp(g_last - gc)).astype(BF16))
            blk_dec.append(jnp.exp(g_last))
            if h % 2 == 1:
                gc_w = jnp.where(lane_w < C, g_cum[:, h - 1:h], gc)
                gr_w = jnp.concatenate([g_cum_t[h - 1:h, :], g_cum_t[h:h + 1, :]], axis=1)
                pre["decay_w"].append(jnp.where(incl_w, jnp.exp(jnp.where(incl_w, gc_w - gr_w, 0.0)), 0.0))
            yield
        pre["zg"] = _silu(z_ref[rows, :]) * _sigmoid(gates_ref[rows, 2 * D_MODEL:3 * D_MODEL])
        yield
        pre["gate_b"] = _sigmoid(gates_ref[rows, D_MODEL:2 * D_MODEL])
        yield

    def independent(pre, mid):
        kn_bd = [blockdiag2(pre["kn"][2 * j], pre["kn"][2 * j + 1]) for j in GP]
        a1 = [jnp.where(strict_w, _mm(pre["kk_g"][p], pre["k_inv_st"][p], _NT), 0.0) for p in PR]
        yield
        a2 = [jnp.where(strict_w, _mm(pre["kk_g"][p], pre["kka_inv_st"][p], _NT), 0.0) for p in PR]
        yield
        a_g = [jnp.where(strict_w, _mm(pair(pre["kb"], j), kn_bd[j], _NT) * pre["decay_w"][j], 0.0) for j in GP]
        yield
        mid["b1"] = [jnp.where(incl_w, _mm(pre["r_g"][p], pre["k_inv_st"][p], _NT), 0.0) for p in PR]
        yield
        mid["b2"] = [jnp.where(incl_w, _mm(pre["r_g"][p], pre["kka_inv_st"][p], _NT), 0.0) for p in PR]
        yield
        mid["attn"] = [_mm(pair(pre["qn"], j), kn_bd[j], _NT) * pre["decay_w"][j] for j in GP]
        yield
        t_all = []
        yield from _wide_unit_lower_inverse(a2 + a_g, C, stack_w, t_all)
        t_rw, t_g = t_all[:RW_PAIRS], t_all[RW_PAIRS:]
        mid["w_rw"] = [_mm(t_rw[p], pre["kk_g_st"][p]) for p in PR]
        yield
        av = [_mm(a1[p], pre["v_st"][p]) for p in PR]
        yield
        mid["u_g"] = [_mm(t_g[j], blockdiag2(pre["vb"][2 * j], pre["vb"][2 * j + 1])) for j in GP]
        yield
        mid["w_g"] = [_mm(t_g[j], blockdiag2(pre["kbe"][2 * j], pre["kbe"][2 * j + 1])) for j in GP]
        yield
        mid["u0"] = [_mm(t_rw[p], stack_h(av[p])) for p in PR]
        yield

    def dependent(pre, mid, res):
        st_rw = [s_wkv[p] for p in PR]
        st_g = [s_ssm[h] for h in HD]
        st_gbd = [blockdiag2(st_g[2 * j], st_g[2 * j + 1]) for j in GP]
        u = [_mm(mid["w_rw"][p], st_rw[p], _NT) + mid["u0"][p] for p in PR]
        yield
        rs = [_mm(pre["r_g"][p], st_rw[p], _NT) for p in PR]
        yield
        v_new = [mid["u_g"][j] - _mm(mid["w_g"][j], st_gbd[j]) for j in GP]
        yield
        qs = [_mm(pair(pre["qe"], j), st_gbd[j]) for j in GP]
        yield
        res["y"] = [rs[p] + _mm(mid["b1"][p], pre["v_st"][p]) - _mm(mid["b2"][p], stack_h(u[p])) for p in PR]
        yield
        res["o"] = [qs[j] + _mm(mid["attn"][j], blockdiag2(v_new[j][:, 0:LANES], v_new[j][:, LANES:2 * LANES]))
                    for j in GP]
        yield
        for p in PR:
            upd = _mm(jnp.concatenate([pre["v"][p], (-u[p]).astype(BF16)], axis=0),
                      jnp.concatenate([pre["k_inv"][p], pre["kka_inv"][p]], axis=0), _TN)
            s_wkv[p] = (st_rw[p] + upd * bd128) * pre["gam_end"][:, psl[p]]
        yield
        for h in HD:
            vn = v_new[h // 2][:, (h % 2) * LANES:(h % 2 + 1) * LANES]
            s_ssm[h] = st_g[h] * pre["blk_dec"][h] + _mm(pre["k_dec"][h], vn, _TN)
        yield

    def post(sub, pre, res):
        y_rw = jnp.concatenate(res["y"], axis=1)
        inv_n = 1.0 / RW_HEAD_DIM
        y_c = y_rw - seg_sum(y_rw) * inv_n
        yield
        y_n = y_c * lax.rsqrt(seg_sum(y_c * y_c) * inv_n + RW_GN_EPS)
        yield
        y_n = y_n * lnw_ref[...] + lnb_ref[...]
        yb = (y_n + pre["bonus"]) * pre["g"]
        yield
        o_all = jnp.concatenate(res["o"], axis=1)
        yc = jnp.concatenate(
            [o_all[:, s] * lax.rsqrt(jnp.mean(o_all[:, s] * o_all[:, s], axis=-1, keepdims=True) + NORM_EPS)
             * gnw_ref[...] for s in psl], axis=1)
        yield
        m = pre["ya"] + pre["gate_b"] * yb + yc * pre["zg"]
        m_ref[sub * C:(sub + 1) * C, :] = m.astype(m_ref.dtype)
        yield

    def weave(*gens):
        gens = [g for g in gens if g is not None]
        while gens:
            for g in list(gens):
                try:
                    next(g)
                except StopIteration:
                    gens.remove(g)

    pres = [dict() for _ in range(n_sub)]
    mids = [dict() for _ in range(n_sub)]
    ress = [dict() for _ in range(n_sub)]

    def chain(*gens):
        for g in gens:
            yield from g

    groups = [list(range(i, min(i + 2, n_sub))) for i in range(0, n_sub, 2)]
    for s in groups[0]:
        weave(prep(s, pres[s]))
    for gi, grp in enumerate(groups):
        fill = []
        if gi + 1 < len(groups):
            fill += [prep(s, pres[s]) for s in groups[gi + 1]]
        if gi:
            fill += [post(s, pres[s], ress[s]) for s in groups[gi - 1]]
        weave(*[independent(pres[s], mids[s]) for s in grp], chain(*fill))
        weave(chain(*[dependent(pres[s], mids[s], ress[s]) for s in grp]))
    weave(chain(*[post(s, pres[s], ress[s]) for s in groups[-1]]))

    ext_conv[0:HALO, :] = ext_conv[R:R + HALO, :]
    ext_qkv[0:HALO, :] = ext_qkv[R:R + HALO, :]
    ext_rkv[0:HALO, :] = ext_rkv[R:R + HALO, :]
    ext_sm[0:HALO, :] = ext_sm[R:R + HALO, :]

    @pl.when(t_idx == n_t - 1)
    def _finish():
        o_conva_ref[0] = ext_conv[HALO - 2:HALO, :]
        o_gconv_ref[0] = ext_qkv[HALO - 3:HALO, :]
        o_shrkv_ref[0] = ext_rkv[HALO - 1:HALO, :]
        o_shsm_ref[0] = ext_sm[HALO - 1:HALO, :]
        o_wkv_ref[0] = s_wkv[...]
        o_ssm_ref[0] = s_ssm[...]


def _mixer(proj, states, prm, nb, t_len, chunk):
    n_sub = MIXER_SUB if t_len % (MIXER_SUB * chunk) == 0 else 1
    blk_rows = n_sub * chunk
    n_t = t_len // blk_rows
    rows = nb * t_len

    def pspec(width, col):
        blk = col // width
        return pl.BlockSpec((blk_rows, width), lambda b, t: (b * n_t + t, blk))

    def sspec(shape):
        nd = len(shape)
        return pl.BlockSpec((1,) + tuple(shape[1:]), lambda b, t: (b,) + (0,) * (nd - 1))

    def wspec(arr):
        return pl.BlockSpec(arr.shape, lambda b, t: (0,) * arr.ndim)

    st_names = ("conva", "shrkv", "shsm", "wkv", "gconv", "ssm")
    st_list = [states[n] for n in st_names]
    w_names = ("conv_a_w", "mu_rkv", "mu_sm", "w0", "w2", "a0", "a2", "g2", "k_k", "k_a", "r_k",
               "ln_w", "ln_b", "gdn_conv_w", "a_log", "dt_bias", "gdn_norm_w")
    w_list = [prm[n] for n in w_names]
    in_specs = [
        pspec(3 * D_MODEL, COL_CONV), pspec(3 * D_MODEL, COL_QKV), pspec(3 * D_MODEL, COL_GATES),
        pspec(3 * D_MODEL, COL_RKV), pspec(D_MODEL, COL_Z), pspec(R_SMALL, COL_SMALL), pspec(LANES, COL_GAB),
    ] + [sspec(s.shape) for s in st_list] + [wspec(w) for w in w_list]
    out_shapes = [jax.ShapeDtypeStruct((rows, D_MODEL), BF16)] + [
        jax.ShapeDtypeStruct(s.shape, F32) for s in st_list]
    out_specs = [pl.BlockSpec((blk_rows, D_MODEL), lambda b, t: (b * n_t + t, 0))] + [
        sspec(s.shape) for s in st_list]
    scratch = [
        pltpu.VMEM((HALO + blk_rows, D_MODEL), F32),
        pltpu.VMEM((HALO + blk_rows, 3 * D_MODEL), F32),
        pltpu.VMEM((HALO + blk_rows, 3 * D_MODEL), F32),
        pltpu.VMEM((HALO + blk_rows, R_SMALL), F32),
        pltpu.VMEM((RW_PAIRS, LANES, LANES), F32),
        pltpu.VMEM((GDN_HEADS, GDN_HEAD_DIM, GDN_HEAD_DIM), F32),
    ]
    outs = pl.pallas_call(
        functools.partial(_mixer_kernel, chunk, n_sub),
        grid=(nb, n_t),
        in_specs=in_specs,
        out_specs=out_specs,
        out_shape=out_shapes,
        scratch_shapes=scratch,
        compiler_params=pltpu.CompilerParams(
            dimension_semantics=("arbitrary", "arbitrary"), vmem_limit_bytes=VMEM_LIMIT),
        name="mixer",
    )(*([proj] * 7), *st_list, *w_list)
    return outs[0], dict(zip(st_names, outs[1:]))


def _prep_layer(l, norm_mix_pre, norm_mix_post, norm_mlp_pre, norm_mlp_post, w_in, conv_a_w, rwkv_mu, rwkv_w0,
                rwkv_w2, rwkv_a0, rwkv_a2, rwkv_g2, rwkv_k_k, rwkv_k_a, rwkv_r_k, rwkv_ln_w, rwkv_ln_b,
                gdn_conv_w, gdn_a_log, gdn_dt_bias, gdn_norm_w, w_o, w_ff1, w_ff2):
    d = D_MODEL
    wi = w_in[l].astype(BF16)
    c_prw = 3 * d
    c_qkv = c_prw + 3 * d + R_SMALL
    c_z = c_qkv + 3 * d
    c_ga = c_z + d
    c_gates = c_ga + 2 * GDN_HEADS
    w_perm = jnp.concatenate([
        wi[:, 0:3 * d], wi[:, c_qkv:c_qkv + 3 * d], wi[:, c_gates:c_gates + 3 * d], wi[:, c_prw:c_prw + 3 * d],
        wi[:, c_z:c_z + d], wi[:, c_prw + 3 * d:c_prw + 3 * d + R_SMALL], wi[:, c_ga:c_ga + 2 * GDN_HEADS],
        jnp.zeros((d, P_PAD - COL_GAB - 2 * GDN_HEADS), BF16)], axis=1)
    row = lambda a: a.reshape(1, -1).astype(F32)
    pad_heads = lambda a: jnp.pad(a.astype(F32), (0, LANES - GDN_HEADS)).reshape(1, LANES)
    mu = rwkv_mu[l]
    return {
        "g_pre": row(norm_mix_pre[l]), "g_post": row(norm_mix_post[l]),
        "g_pre2": row(norm_mlp_pre[l]), "g_post2": row(norm_mlp_post[l]),
        "w_in": w_perm, "w_o": w_o[l].astype(BF16), "w_ff1": w_ff1[l].astype(BF16), "w_ff2": w_ff2[l].astype(BF16),
        "conv_a_w": conv_a_w[l], "mu_rkv": row(mu[0:3 * d]), "mu_sm": row(mu[3 * d:]),
        "w0": row(rwkv_w0[l]), "w2": rwkv_w2[l].astype(BF16), "a0": row(rwkv_a0[l]),
        "a2": rwkv_a2[l].astype(BF16), "g2": rwkv_g2[l].astype(BF16),
        "k_k": row(rwkv_k_k[l]), "k_a": row(rwkv_k_a[l]), "r_k": row(rwkv_r_k[l]),
        "ln_w": row(rwkv_ln_w[l]), "ln_b": row(rwkv_ln_b[l]), "gdn_conv_w": gdn_conv_w[l],
        "a_log": pad_heads(gdn_a_log[l]), "dt_bias": pad_heads(gdn_dt_bias[l]), "gdn_norm_w": row(gdn_norm_w[l]),
    }


def _pack_wkv(s):
    b = s.shape[0]
    s = s.reshape(b, RW_PAIRS, 2, RW_HEAD_DIM, RW_HEAD_DIM)
    zero = jnp.zeros_like(s[:, :, 0])
    top = jnp.concatenate([s[:, :, 0], zero], axis=-1)
    bot = jnp.concatenate([zero, s[:, :, 1]], axis=-1)
    return jnp.concatenate([top, bot], axis=-2)


def _unpack_wkv(s):
    b = s.shape[0]
    h = RW_HEAD_DIM
    return jnp.stack([s[:, :, 0:h, 0:h], s[:, :, h:, h:]], axis=2).reshape(b, RW_HEADS, h, h)


def _layer(x, nb, t_len, chunk, st, prm, tm_proj, tm_ffn):
    conv_a, rw_shift, rw_wkv, gdn_conv, gdn_ssm = st
    states = {
        "conva": conv_a, "shrkv": rw_shift[:, None, 0:3 * D_MODEL], "shsm": rw_shift[:, None, 3 * D_MODEL:],
        "wkv": _pack_wkv(rw_wkv), "gconv": gdn_conv, "ssm": gdn_ssm,
    }
    proj = _proj(x, prm["g_pre"], prm["w_in"], tm_proj)
    m, new = _mixer(proj, states, prm, nb, t_len, chunk)
    x = _outffn(m, x, prm["w_o"], prm["w_ff1"], prm["w_ff2"], prm["g_post"], prm["g_pre2"], prm["g_post2"], tm_ffn)
    new_st = (new["conva"], jnp.concatenate([new["shrkv"][:, 0], new["shsm"][:, 0]], axis=-1),
              _unpack_wkv(new["wkv"]), new["gconv"], new["ssm"])
    return x, new_st


def _row_tile(rows, want):
    t = min(rows, want)
    while rows % t:
        t //= 2
    return t


def _forward(x_prompt, x_sample, sample_states, weights):
    bp, tp, d = x_prompt.shape
    bs, ts, _ = x_sample.shape
    zero_states = (jnp.zeros((bp, 2, d), F32), jnp.zeros((bp, 3 * d + R_SMALL), F32),
                   jnp.zeros((bp, RW_HEADS, RW_HEAD_DIM, RW_HEAD_DIM), F32),
                   jnp.zeros((bp, 3, 3 * d), F32), jnp.zeros((bp, GDN_HEADS, GDN_HEAD_DIM, GDN_HEAD_DIM), F32))
    y_p = x_prompt.reshape(bp * tp, d)
    y_s = x_sample.reshape(bs * ts, d)
    new_p = [[] for _ in range(5)]
    new_s = [[] for _ in range(5)]
    n_layers = weights[0].shape[0]
    for l in range(n_layers):
        prm = _prep_layer(l, *weights)
        y_p, st_p = _layer(y_p, bp, tp, min(CHUNK, tp), zero_states, prm,
                           _row_tile(bp * tp, 1024), _row_tile(bp * tp, 256))
        y_s, st_s = _layer(y_s, bs, ts, min(CHUNK, ts), tuple(s[l] for s in sample_states), prm,
                           _row_tile(bs * ts, 1024), _row_tile(bs * ts, 256))
        for i in range(5):
            new_p[i].append(st_p[i])
            new_s[i].append(st_s[i])
    outs_p = [jnp.stack(t, axis=0) for t in new_p]
    outs_s = [jnp.stack(t, axis=0) for t in new_s]
    return (y_p.reshape(bp, tp, d), y_s.reshape(bs, ts, d), *outs_p, *outs_s)


def kernel(x_prompt, x_sample, state_conv_a, state_rwkv_shift, state_rwkv_wkv, state_gdn_conv, state_gdn_ssm,
           norm_mix_pre, norm_mix_post, norm_mlp_pre, norm_mlp_post, w_in, conv_a_w, rwkv_mu, rwkv_w0, rwkv_w2,
           rwkv_a0, rwkv_a2, rwkv_g2, rwkv_k_k, rwkv_k_a, rwkv_r_k, rwkv_ln_w, rwkv_ln_b, gdn_conv_w, gdn_a_log,
           gdn_dt_bias, gdn_norm_w, w_o, w_ff1, w_ff2):
    sample_states = (state_conv_a, state_rwkv_shift, state_rwkv_wkv, state_gdn_conv, state_gdn_ssm)
    weights = (norm_mix_pre, norm_mix_post, norm_mlp_pre, norm_mlp_post, w_in, conv_a_w, rwkv_mu, rwkv_w0, rwkv_w2,
               rwkv_a0, rwkv_a2, rwkv_g2, rwkv_k_k, rwkv_k_a, rwkv_r_k, rwkv_ln_w, rwkv_ln_b, gdn_conv_w,
               gdn_a_log, gdn_dt_bias, gdn_norm_w, w_o, w_ff1, w_ff2)
    return _forward(x_prompt, x_sample, sample_states, weights)
```

```python
import functools

import jax
import jax.numpy as jnp
from jax import lax
from jax.experimental import pallas as pl
from jax.experimental.pallas import tpu as pltpu

F32 = jnp.float32
BF16 = jnp.bfloat16

D_MODEL = 1024
DEPTH = 4
CHUNK = 64
NORM_EPS = 1e-6
L2_EPS = 1e-12
RW_HEAD_DIM = 64
RW_HEADS = D_MODEL // RW_HEAD_DIM
RW_PAIRS = RW_HEADS // 2
R_DECAY = D_MODEL // 16
R_AAA = D_MODEL // 16
R_GATE = D_MODEL // 8
R_SMALL = R_DECAY + R_AAA + R_GATE
RW_GN_EPS = 64e-5
GDN_HEAD_DIM = 128
GDN_HEADS = D_MODEL // GDN_HEAD_DIM
D_FF = 4 * D_MODEL
LANES = 128
HALO = 8
MIXER_SUB = 4

COL_CONV = 0
COL_RKV = 3 * D_MODEL
COL_SMALL = 6 * D_MODEL
COL_QKV = COL_SMALL + R_SMALL
COL_Z = COL_QKV + 3 * D_MODEL
COL_GAB = COL_Z + D_MODEL
COL_GATES = COL_GAB + 2 * GDN_HEADS
P_IN = COL_GATES + 3 * D_MODEL
PROJ_TN = 1024
P_MAIN = ((COL_GAB + LANES + PROJ_TN - 1) // PROJ_TN) * PROJ_TN

VMEM_LIMIT = 56 * 1024 * 1024

_NN = (((1,), (0,)), ((), ()))
_NT = (((1,), (1,)), ((), ()))
_TN = (((0,), (0,)), ((), ()))


def _mm(a, b, dims=_NN):
    return lax.dot_general(a.astype(BF16), b.astype(BF16), dims, preferred_element_type=F32)


def _split3(x):
    hi = x.astype(BF16)
    r1 = x - hi.astype(F32)
    mid = r1.astype(BF16)
    lo = (r1 - mid.astype(F32)).astype(BF16)
    return hi, mid, lo


def _mm_exact_lhs(a_bf16, x):
    out = None
    for q in _split3(x):
        t = lax.dot_general(a_bf16, q, _NN, preferred_element_type=F32)
        out = t if out is None else out + t
    return out


def _sigmoid(x):
    return 1.0 / (1.0 + jnp.exp(-x))


def _softplus(x):
    return jnp.maximum(x, 0.0) + jnp.log(1.0 + jnp.exp(-jnp.abs(x)))


def _silu(x):
    return x * _sigmoid(x)


def _rms(x, gain):
    return x * lax.rsqrt(jnp.mean(x * x, axis=-1, keepdims=True) + NORM_EPS) * gain


def _proj_kernel(x_ref, g_ref, w_ref, o_ref):
    h = _rms(x_ref[...], g_ref[...])
    o_ref[...] = jnp.dot(h.astype(BF16), w_ref[...], preferred_element_type=F32)


def _proj(x, gain, w_all, layer, n_cols, tm):
    rows = x.shape[0]
    return pl.pallas_call(
        _proj_kernel,
        grid=(rows // tm, n_cols // PROJ_TN),
        in_specs=[
            pl.BlockSpec((tm, D_MODEL), lambda i, j: (i, 0)),
            pl.BlockSpec((1, D_MODEL), lambda i, j: (0, 0)),
            pl.BlockSpec((None, D_MODEL, PROJ_TN), lambda i, j: (layer, 0, j)),
        ],
        out_specs=pl.BlockSpec((tm, PROJ_TN), lambda i, j: (i, j)),
        out_shape=jax.ShapeDtypeStruct((rows, n_cols), F32),
        compiler_params=pltpu.CompilerParams(
            dimension_semantics=("arbitrary", "arbitrary"), vmem_limit_bytes=VMEM_LIMIT),
        name="proj",
    )(x, gain, w_all)


def _outffn_kernel(m_ref, x_ref, wo_ref, w1_ref, w2_ref, g1_ref, g2_ref, g3_ref, o_ref):
    mo = jnp.dot(m_ref[...], wo_ref[...], preferred_element_type=F32)
    x = x_ref[...] + _rms(mo, g1_ref[...])
    h2 = _rms(x, g2_ref[...])
    f = jnp.dot(h2.astype(BF16), w1_ref[...], preferred_element_type=F32)
    f = jnp.square(jnp.maximum(f, 0.0))
    f = jnp.dot(f.astype(BF16), w2_ref[...], preferred_element_type=F32)
    o_ref[...] = x + _rms(f, g3_ref[...])


def _outffn(m, x, wo, w1, w2, g_post, g_pre2, g_post2, tm):
    rows = x.shape[0]
    const = lambda i: (0, 0)
    return pl.pallas_call(
        _outffn_kernel,
        grid=(rows // tm,),
        in_specs=[
            pl.BlockSpec((tm, D_MODEL), lambda i: (i, 0)),
            pl.BlockSpec((tm, D_MODEL), lambda i: (i, 0)),
            pl.BlockSpec((D_MODEL, D_MODEL), const),
            pl.BlockSpec((D_MODEL, D_FF), const),
            pl.BlockSpec((D_FF, D_MODEL), const),
            pl.BlockSpec((1, D_MODEL), const),
            pl.BlockSpec((1, D_MODEL), const),
            pl.BlockSpec((1, D_MODEL), const),
        ],
        out_specs=pl.BlockSpec((tm, D_MODEL), lambda i: (i, 0)),
        out_shape=jax.ShapeDtypeStruct((rows, D_MODEL), F32),
        compiler_params=pltpu.CompilerParams(
            dimension_semantics=("arbitrary",), vmem_limit_bytes=VMEM_LIMIT),
        name="outffn",
    )(m, x, wo, w1, w2, g_post, g_pre2, g_post2)


INV_BASE = 16


def _wide_unit_lower_inverse(mats, size, stack_w, out):
    base = min(size, INV_BASE)
    row = lax.broadcasted_iota(jnp.int32, (size, 2 * size), 0)
    col = lax.broadcasted_iota(jnp.int32, (size, 2 * size), 1) % size
    eye = (row == col).astype(F32)
    ps = [jnp.where((row // base) == (col // base), a, 0.0) if base < size else a for a in mats]
    ts = [eye - p for p in ps]
    steps = base.bit_length() - 2
    if steps:
        ps = [_mm(p, stack_w(p)) for p in ps]
        yield
    for step in range(steps):
        sps = [stack_w(p) for p in ps]
        ts = [t + _mm(t, sp) for t, sp in zip(ts, sps)]
        if step + 1 < steps:
            ps = [_mm(p, sp) for p, sp in zip(ps, sps)]
        yield
    b = base
    while b < size:
        off = ((row // (2 * b)) == (col // (2 * b))) & ((row // b) != (col // b))
        lt = [_mm(jnp.where(off, a, 0.0), stack_w(t)) for a, t in zip(mats, ts)]
        yield
        ts = [t - _mm(t, stack_w(x)) for t, x in zip(ts, lt)]
        yield
        b *= 2
    out.extend(ts)


def _mixer_kernel(chunk, n_sub,
                  main_ref, gab_ref, gates_ref,
                  st_conva_ref, st_shrkv_ref, st_shsm_ref, st_wkv_ref, st_gconv_ref, st_ssm_ref,
                  convw_ref, mu_rkv_ref, mu_sm_ref, w0_ref, w2_ref, a0_ref, a2_ref, g2_ref,
                  kk_ref, ka_ref, rk_ref, lnw_ref, lnb_ref, gconvw_ref, alog_ref, dtb_ref, gnw_ref,
                  m_ref, o_conva_ref, o_shrkv_ref, o_shsm_ref, o_wkv_ref, o_gconv_ref, o_ssm_ref,
                  ext_conv, ext_qkv, ext_rkv, ext_sm, s_wkv, s_ssm):
    C = chunk
    R = n_sub * C
    W2 = 2 * C
    t_idx = pl.program_id(1)
    n_t = pl.num_programs(1)
    conv_ref = main_ref.at[:, COL_CONV:COL_CONV + 3 * D_MODEL]
    rkv_ref = main_ref.at[:, COL_RKV:COL_RKV + 3 * D_MODEL]
    small_ref = main_ref.at[:, COL_SMALL:COL_SMALL + R_SMALL]
    qkv_ref = main_ref.at[:, COL_QKV:COL_QKV + 3 * D_MODEL]
    z_ref = main_ref.at[:, COL_Z:COL_Z + D_MODEL]

    @pl.when(t_idx == 0)
    def _init():
        ext_conv[0:HALO, :] = jnp.zeros((HALO, D_MODEL), F32)
        ext_conv[HALO - 2:HALO, :] = st_conva_ref[0]
        ext_qkv[0:HALO, :] = jnp.zeros((HALO, 3 * D_MODEL), F32)
        ext_qkv[HALO - 3:HALO, :] = st_gconv_ref[0]
        ext_rkv[0:HALO, :] = jnp.zeros((HALO, 3 * D_MODEL), F32)
        ext_rkv[HALO - 1:HALO, :] = st_shrkv_ref[0]
        ext_sm[0:HALO, :] = jnp.zeros((HALO, R_SMALL), F32)
        ext_sm[HALO - 1:HALO, :] = st_shsm_ref[0]
        s_wkv[...] = st_wkv_ref[0]
        s_ssm[...] = st_ssm_ref[0]

    tri_incl = (lax.broadcasted_iota(jnp.int32, (C, C), 0) >= lax.broadcasted_iota(jnp.int32, (C, C), 1)).astype(BF16)

    li = lax.broadcasted_iota(jnp.int32, (LANES, LANES), 0) // RW_HEAD_DIM
    lj = lax.broadcasted_iota(jnp.int32, (LANES, LANES), 1) // RW_HEAD_DIM
    bd128 = (li == lj).astype(F32)
    bd128_bf = bd128.astype(BF16)

    def seg_sum(x):
        xb = x.astype(BF16)
        return jnp.concatenate(
            [lax.dot_general(xb[:, j * LANES:(j + 1) * LANES], bd128_bf, _NN, preferred_element_type=F32)
             for j in range(D_MODEL // LANES)], axis=1)

    t_w = lax.broadcasted_iota(jnp.int32, (C, W2), 0)
    s_w = lax.broadcasted_iota(jnp.int32, (C, W2), 1) % C
    strict_w = t_w > s_w
    incl_w = t_w >= s_w
    lane_h = lax.broadcasted_iota(jnp.int32, (1, LANES), 1)
    mh0 = (lane_h < RW_HEAD_DIM).astype(BF16)
    mh1 = (lane_h >= RW_HEAD_DIM).astype(BF16)
    lane_w = lax.broadcasted_iota(jnp.int32, (1, W2), 1)
    mw0 = (lane_w < C).astype(BF16)
    mw1 = (lane_w >= C).astype(BF16)

    def stack_h(x):
        xb = x.astype(BF16)
        return jnp.concatenate([xb * mh0, xb * mh1], axis=0)

    def stack_w(x):
        xb = x.astype(BF16)
        return jnp.concatenate([xb * mw0, xb * mw1], axis=0)

    def blockdiag2(x0, x1):
        z = jnp.zeros(x0.shape, BF16)
        return jnp.concatenate([jnp.concatenate([x0.astype(BF16), z], axis=1),
                                jnp.concatenate([z, x1.astype(BF16)], axis=1)], axis=0)

    def pair(xs, j):
        return jnp.concatenate([xs[2 * j], xs[2 * j + 1]], axis=1)

    PR = range(RW_PAIRS)
    HD = range(GDN_HEADS)
    GP = range(GDN_HEADS // 2)
    psl = [slice(p * LANES, (p + 1) * LANES) for p in PR]

    ext_conv[HALO:HALO + R, :] = conv_ref[:, D_MODEL:2 * D_MODEL] * conv_ref[:, 2 * D_MODEL:3 * D_MODEL]
    ext_rkv[HALO:HALO + R, :] = rkv_ref[...]
    ext_sm[HALO:HALO + R, :] = small_ref[...]
    ext_qkv[HALO:HALO + R, :] = qkv_ref[...]

    def prep(sub, pre):
        r0 = sub * C
        e0 = HALO + r0
        rows = slice(r0, r0 + C)

        def back(ext, k, c0, width):
            blk = ext[e0 - HALO:e0 + C, c0:c0 + width]
            return pltpu.roll(blk, k, 0)[HALO:HALO + C]

        pre["ya"] = conv_ref[rows, 0:D_MODEL] * (
            convw_ref[0:1, :] * back(ext_conv, 2, 0, D_MODEL)
            + convw_ref[1:2, :] * back(ext_conv, 1, 0, D_MODEL)
            + convw_ref[2:3, :] * ext_conv[e0:e0 + C, :]) * _sigmoid(gates_ref[rows, 0:D_MODEL])
        yield

        p_sm = small_ref[rows, :]
        ps_sm = p_sm + (back(ext_sm, 1, 0, R_SMALL) - p_sm) * mu_sm_ref[...]
        xw = ps_sm[:, 0:R_DECAY]
        xa = ps_sm[:, R_DECAY:R_DECAY + R_AAA]
        xg = ps_sm[:, R_DECAY + R_AAA:R_SMALL]
        w_log = -_softplus(-(w0_ref[...] + _mm(jnp.tanh(xw), w2_ref[...]))) - 0.5
        logd = -jnp.exp(w_log)
        yield
        cum = _mm_exact_lhs(tri_incl, logd)
        a = _sigmoid(a0_ref[...] + _mm(xa, a2_ref[...]))
        pre["g"] = _mm(_sigmoid(xg), g2_ref[...])
        yield

        def shifted(c0):
            p = rkv_ref[rows, c0:c0 + D_MODEL]
            return p + (back(ext_rkv, 1, c0, D_MODEL) - p) * mu_rkv_ref[:, c0:c0 + D_MODEL]

        k = shifted(D_MODEL)
        yield
        kk0 = k * kk_ref[...]
        kk = kk0 * lax.rsqrt(seg_sum(kk0 * kk0) + L2_EPS)
        yield
        k2 = k * (1.0 + (a - 1.0) * ka_ref[...])
        kka = kk * a
        e_inv = jnp.exp(-cum)
        yield
        kk_g = (kk * jnp.exp(cum - logd)).astype(BF16)
        pre["kk_g"] = [kk_g[:, s] for s in psl]
        pre["kk_g_st"] = [stack_h(kk_g[:, s]) for s in psl]
        yield
        k_inv = (k2 * e_inv).astype(BF16)
        pre["k_inv"] = [k_inv[:, s] for s in psl]
        pre["k_inv_st"] = [stack_h(k_inv[:, s]) for s in psl]
        yield
        kka_inv = (kka * e_inv).astype(BF16)
        pre["kka_inv"] = [kka_inv[:, s] for s in psl]
        pre["kka_inv_st"] = [stack_h(kka_inv[:, s]) for s in psl]
        yield
        r = shifted(0)
        r_g = (r * jnp.exp(cum)).astype(BF16)
        pre["r_g"] = [r_g[:, s] for s in psl]
        yield
        v = shifted(2 * D_MODEL)
        pre["v"] = [v[:, s].astype(BF16) for s in psl]
        pre["v_st"] = [stack_h(v[:, s]) for s in psl]
        pre["gam_end"] = jnp.exp(cum[C - 1:C, :])
        yield
        pre["bonus"] = seg_sum(r * k2 * rk_ref[...]) * v
        yield

        gab = gab_ref[rows, :]
        g_log = -jnp.exp(alog_ref[...]) * _softplus(gab + dtb_ref[...])
        beta_all = _sigmoid(gab)
        g_cum = _mm_exact_lhs(tri_incl, g_log)
        g_cum_t = g_cum.T
        yield

        def conv_silu(c0):
            cq = (gconvw_ref[0:1, c0:c0 + LANES] * back(ext_qkv, 3, c0, LANES)
                  + gconvw_ref[1:2, c0:c0 + LANES] * back(ext_qkv, 2, c0, LANES)
                  + gconvw_ref[2:3, c0:c0 + LANES] * back(ext_qkv, 1, c0, LANES)
                  + gconvw_ref[3:4, c0:c0 + LANES] * ext_qkv[e0:e0 + C, c0:c0 + LANES])
            return _silu(cq)

        qn, kn, vb, kb, kbe, qe, k_dec, blk_dec = [], [], [], [], [], [], [], []
        pre.update(qn=qn, kn=kn, vb=vb, kb=kb, kbe=kbe, qe=qe, k_dec=k_dec, blk_dec=blk_dec, decay_w=[])
        for h in HD:
            qh = conv_silu(h * LANES)
            kh = conv_silu(D_MODEL + h * LANES)
            vh = conv_silu(2 * D_MODEL + h * LANES)
            qn_h = qh * lax.rsqrt(jnp.sum(qh * qh, axis=-1, keepdims=True) + L2_EPS) * (GDN_HEAD_DIM ** -0.5)
            kn_h = kh * lax.rsqrt(jnp.sum(kh * kh, axis=-1, keepdims=True) + L2_EPS)
            gc = g_cum[:, h:h + 1]
            beta = beta_all[:, GDN_HEADS + h:GDN_HEADS + h + 1]
            e_g = jnp.exp(gc)
            g_last = g_cum[C - 1:C, h:h + 1]
            kb_h = kn_h * beta
            qn.append(qn_h.astype(BF16))
            kn.append(kn_h.astype(BF16))
            kb.append(kb_h.astype(BF16))
            vb.append((vh * beta).astype(BF16))
            kbe.append((kb_h * e_g).astype(BF16))
            qe.append((qn_h * e_g).astype(BF16))
            k_dec.append((kn_h * jnp.exp(g_last - gc)).astype(BF16))
            blk_dec.append(jnp.exp(g_last))
            if h % 2 == 1:
                gc_w = jnp.where(lane_w < C, g_cum[:, h - 1:h], gc)
                gr_w = jnp.concatenate([g_cum_t[h - 1:h, :], g_cum_t[h:h + 1, :]], axis=1)
                pre["decay_w"].append(jnp.where(incl_w, jnp.exp(jnp.where(incl_w, gc_w - gr_w, 0.0)), 0.0))
            yield
        pre["zg"] = _silu(z_ref[rows, :]) * _sigmoid(gates_ref[rows, 2 * D_MODEL:3 * D_MODEL])
        yield
        pre["gate_b"] = _sigmoid(gates_ref[rows, D_MODEL:2 * D_MODEL])
        yield

    def independent(pre, mid):
        kn_bd = [blockdiag2(pre["kn"][2 * j], pre["kn"][2 * j + 1]) for j in GP]
        a1 = [jnp.where(strict_w, _mm(pre["kk_g"][p], pre["k_inv_st"][p], _NT), 0.0) for p in PR]
        yield
        a2 = [jnp.where(strict_w, _mm(pre["kk_g"][p], pre["kka_inv_st"][p], _NT), 0.0) for p in PR]
        yield
        a_g = [jnp.where(strict_w, _mm(pair(pre["kb"], j), kn_bd[j], _NT) * pre["decay_w"][j], 0.0) for j in GP]
        yield
        mid["b1"] = [jnp.where(incl_w, _mm(pre["r_g"][p], pre["k_inv_st"][p], _NT), 0.0) for p in PR]
        yield
        mid["b2"] = [jnp.where(incl_w, _mm(pre["r_g"][p], pre["kka_inv_st"][p], _NT), 0.0) for p in PR]
        yield
        mid["attn"] = [_mm(pair(pre["qn"], j), kn_bd[j], _NT) * pre["decay_w"][j] for j in GP]
        yield
        t_all = []
        yield from _wide_unit_lower_inverse(a2 + a_g, C, stack_w, t_all)
        t_rw, t_g = t_all[:RW_PAIRS], t_all[RW_PAIRS:]
        mid["w_rw"] = [_mm(t_rw[p], pre["kk_g_st"][p]) for p in PR]
        yield
        av = [_mm(a1[p], pre["v_st"][p]) for p in PR]
        yield
        mid["u_g"] = [_mm(t_g[j], blockdiag2(pre["vb"][2 * j], pre["vb"][2 * j + 1])) for j in GP]
        yield
        mid["w_g"] = [_mm(t_g[j], blockdiag2(pre["kbe"][2 * j], pre["kbe"][2 * j + 1])) for j in GP]
        yield
        mid["u0"] = [_mm(t_rw[p], stack_h(av[p])) for p in PR]
        yield

    def dependent(pre, mid, res):
        st_rw = [s_wkv[p] for p in PR]
        st_g = [s_ssm[h] for h in HD]
        st_gbd = [blockdiag2(st_g[2 * j], st_g[2 * j + 1]) for j in GP]
        u = [_mm(mid["w_rw"][p], st_rw[p], _NT) + mid["u0"][p] for p in PR]
        yield
        rs = [_mm(pre["r_g"][p], st_rw[p], _NT) for p in PR]
        yield
        v_new = [mid["u_g"][j] - _mm(mid["w_g"][j], st_gbd[j]) for j in GP]
        yield
        qs = [_mm(pair(pre["qe"], j), st_gbd[j]) for j in GP]
        yield
        res["y"] = [rs[p] + _mm(mid["b1"][p], pre["v_st"][p]) - _mm(mid["b2"][p], stack_h(u[p])) for p in PR]
        yield
        res["o"] = [qs[j] + _mm(mid["attn"][j], blockdiag2(v_new[j][:, 0:LANES], v_new[j][:, LANES:2 * LANES]))
                    for j in GP]
        yield
        for p in PR:
            upd = _mm(jnp.concatenate([pre["v"][p], (-u[p]).astype(BF16)], axis=0),
                      jnp.concatenate([pre["k_inv"][p], pre["kka_inv"][p]], axis=0), _TN)
            s_wkv[p] = (st_rw[p] + upd * bd128) * pre["gam_end"][:, psl[p]]
        yield
        for h in HD:
            vn = v_new[h // 2][:, (h % 2) * LANES:(h % 2 + 1) * LANES]
            s_ssm[h] = st_g[h] * pre["blk_dec"][h] + _mm(pre["k_dec"][h], vn, _TN)
        yield

    def post(sub, pre, res):
        y_rw = jnp.concatenate(res["y"], axis=1)
        inv_n = 1.0 / RW_HEAD_DIM
        y_c = y_rw - seg_sum(y_rw) * inv_n
        yield
        y_n = y_c * lax.rsqrt(seg_sum(y_c * y_c) * inv_n + RW_GN_EPS)
        yield
        y_n = y_n * lnw_ref[...] + lnb_ref[...]
        yb = (y_n + pre["bonus"]) * pre["g"]
        yield
        o_all = jnp.concatenate(res["o"], axis=1)
        yc = jnp.concatenate(
            [o_all[:, s] * lax.rsqrt(jnp.mean(o_all[:, s] * o_all[:, s], axis=-1, keepdims=True) + NORM_EPS)
             * gnw_ref[...] for s in psl], axis=1)
        yield
        m = pre["ya"] + pre["gate_b"] * yb + yc * pre["zg"]
        m_ref[sub * C:(sub + 1) * C, :] = m.astype(m_ref.dtype)
        yield

    def weave(*gens):
        gens = [g for g in gens if g is not None]
        while gens:
            for g in list(gens):
                try:
                    next(g)
                except StopIteration:
                    gens.remove(g)

    pres = [dict() for _ in range(n_sub)]
    mids = [dict() for _ in range(n_sub)]
    ress = [dict() for _ in range(n_sub)]

    def chain(*gens):
        for g in gens:
            yield from g

    groups = [list(range(i, min(i + 2, n_sub))) for i in range(0, n_sub, 2)]
    for s in groups[0]:
        weave(prep(s, pres[s]))
    for gi, grp in enumerate(groups):
        fill = []
        if gi + 1 < len(groups):
            fill += [prep(s, pres[s]) for s in groups[gi + 1]]
        if gi:
            fill += [post(s, pres[s], ress[s]) for s in groups[gi - 1]]
        weave(*[independent(pres[s], mids[s]) for s in grp], chain(*fill))
        weave(chain(*[dependent(pres[s], mids[s], ress[s]) for s in grp]))
    weave(chain(*[post(s, pres[s], ress[s]) for s in groups[-1]]))

    ext_conv[0:HALO, :] = ext_conv[R:R + HALO, :]
    ext_qkv[0:HALO, :] = ext_qkv[R:R + HALO, :]
    ext_rkv[0:HALO, :] = ext_rkv[R:R + HALO, :]
    ext_sm[0:HALO, :] = ext_sm[R:R + HALO, :]

    @pl.when(t_idx == n_t - 1)
    def _finish():
        o_conva_ref[0] = ext_conv[HALO - 2:HALO, :]
        o_gconv_ref[0] = ext_qkv[HALO - 3:HALO, :]
        o_shrkv_ref[0] = ext_rkv[HALO - 1:HALO, :]
        o_shsm_ref[0] = ext_sm[HALO - 1:HALO, :]
        o_wkv_ref[0] = s_wkv[...]
        o_ssm_ref[0] = s_ssm[...]


def _mixer(proj_main, proj_gates, states, prm, nb, t_len, chunk):
    n_sub = MIXER_SUB if t_len % (MIXER_SUB * chunk) == 0 else 1
    blk_rows = n_sub * chunk
    n_t = t_len // blk_rows
    rows = nb * t_len

    def pspec(width, col):
        blk = col // width
        return pl.BlockSpec((blk_rows, width), lambda b, t: (b * n_t + t, blk))

    def sspec(shape):
        nd = len(shape)
        return pl.BlockSpec((1,) + tuple(shape[1:]), lambda b, t: (b,) + (0,) * (nd - 1))

    def wspec(arr):
        return pl.BlockSpec(arr.shape, lambda b, t: (0,) * arr.ndim)

    st_names = ("conva", "shrkv", "shsm", "wkv", "gconv", "ssm")
    st_list = [states[n] for n in st_names]
    w_names = ("conv_a_w", "mu_rkv", "mu_sm", "w0", "w2", "a0", "a2", "g2", "k_k", "k_a", "r_k",
               "ln_w", "ln_b", "gdn_conv_w", "a_log", "dt_bias", "gdn_norm_w")
    w_list = [prm[n] for n in w_names]
    in_specs = [pspec(COL_GAB, 0), pspec(LANES, COL_GAB), pspec(3 * D_MODEL, 0)] + [
        sspec(s.shape) for s in st_list] + [wspec(w) for w in w_list]
    out_shapes = [jax.ShapeDtypeStruct((rows, D_MODEL), BF16)] + [
        jax.ShapeDtypeStruct(s.shape, F32) for s in st_list]
    out_specs = [pl.BlockSpec((blk_rows, D_MODEL), lambda b, t: (b * n_t + t, 0))] + [
        sspec(s.shape) for s in st_list]
    scratch = [
        pltpu.VMEM((HALO + blk_rows, D_MODEL), F32),
        pltpu.VMEM((HALO + blk_rows, 3 * D_MODEL), F32),
        pltpu.VMEM((HALO + blk_rows, 3 * D_MODEL), F32),
        pltpu.VMEM((HALO + blk_rows, R_SMALL), F32),
        pltpu.VMEM((RW_PAIRS, LANES, LANES), F32),
        pltpu.VMEM((GDN_HEADS, GDN_HEAD_DIM, GDN_HEAD_DIM), F32),
    ]
    outs = pl.pallas_call(
        functools.partial(_mixer_kernel, chunk, n_sub),
        grid=(nb, n_t),
        in_specs=in_specs,
        out_specs=out_specs,
        out_shape=out_shapes,
        scratch_shapes=scratch,
        compiler_params=pltpu.CompilerParams(
            dimension_semantics=("arbitrary", "arbitrary"), vmem_limit_bytes=VMEM_LIMIT),
        name="mixer",
    )(proj_main, proj_main, proj_gates, *st_list, *w_list)
    return outs[0], dict(zip(st_names, outs[1:]))


def _prep_layer(l, norm_mix_pre, norm_mix_post, norm_mlp_pre, norm_mlp_post, w_in, conv_a_w, rwkv_mu, rwkv_w0,
                rwkv_w2, rwkv_a0, rwkv_a2, rwkv_g2, rwkv_k_k, rwkv_k_a, rwkv_r_k, rwkv_ln_w, rwkv_ln_b,
                gdn_conv_w, gdn_a_log, gdn_dt_bias, gdn_norm_w, w_o, w_ff1, w_ff2):
    d = D_MODEL
    row = lambda a: a.reshape(1, -1).astype(F32)
    pad_heads = lambda a: jnp.pad(a.astype(F32), (0, LANES - GDN_HEADS)).reshape(1, LANES)
    mu = rwkv_mu[l]
    return {
        "g_pre": row(norm_mix_pre[l]), "g_post": row(norm_mix_post[l]),
        "g_pre2": row(norm_mlp_pre[l]), "g_post2": row(norm_mlp_post[l]),
        "w_o": w_o[l].astype(BF16), "w_ff1": w_ff1[l].astype(BF16), "w_ff2": w_ff2[l].astype(BF16),
        "conv_a_w": conv_a_w[l], "mu_rkv": row(mu[0:3 * d]), "mu_sm": row(mu[3 * d:]),
        "w0": row(rwkv_w0[l]), "w2": rwkv_w2[l].astype(BF16), "a0": row(rwkv_a0[l]),
        "a2": rwkv_a2[l].astype(BF16), "g2": rwkv_g2[l].astype(BF16),
        "k_k": row(rwkv_k_k[l]), "k_a": row(rwkv_k_a[l]), "r_k": row(rwkv_r_k[l]),
        "ln_w": row(rwkv_ln_w[l]), "ln_b": row(rwkv_ln_b[l]), "gdn_conv_w": gdn_conv_w[l],
        "a_log": pad_heads(gdn_a_log[l]), "dt_bias": pad_heads(gdn_dt_bias[l]), "gdn_norm_w": row(gdn_norm_w[l]),
    }


def _pack_wkv(s):
    b = s.shape[0]
    s = s.reshape(b, RW_PAIRS, 2, RW_HEAD_DIM, RW_HEAD_DIM)
    zero = jnp.zeros_like(s[:, :, 0])
    top = jnp.concatenate([s[:, :, 0], zero], axis=-1)
    bot = jnp.concatenate([zero, s[:, :, 1]], axis=-1)
    return jnp.concatenate([top, bot], axis=-2)


def _unpack_wkv(s):
    b = s.shape[0]
    h = RW_HEAD_DIM
    return jnp.stack([s[:, :, 0:h, 0:h], s[:, :, h:, h:]], axis=2).reshape(b, RW_HEADS, h, h)


def _layer(x, l, nb, t_len, chunk, st, prm, w_main, w_gates, tm_proj, tm_ffn):
    conv_a, rw_shift, rw_wkv, gdn_conv, gdn_ssm = st
    states = {
        "conva": conv_a, "shrkv": rw_shift[:, None, 0:3 * D_MODEL], "shsm": rw_shift[:, None, 3 * D_MODEL:],
        "wkv": _pack_wkv(rw_wkv), "gconv": gdn_conv, "ssm": gdn_ssm,
    }
    proj_main = _proj(x, prm["g_pre"], w_main, l, P_MAIN, tm_proj)
    proj_gates = _proj(x, prm["g_pre"], w_gates, l, 3 * D_MODEL, tm_proj)
    m, new = _mixer(proj_main, proj_gates, states, prm, nb, t_len, chunk)
    x = _outffn(m, x, prm["w_o"], prm["w_ff1"], prm["w_ff2"], prm["g_post"], prm["g_pre2"], prm["g_post2"], tm_ffn)
    new_st = (new["conva"], jnp.concatenate([new["shrkv"][:, 0], new["shsm"][:, 0]], axis=-1),
              _unpack_wkv(new["wkv"]), new["gconv"], new["ssm"])
    return x, new_st


def _row_tile(rows, want):
    t = min(rows, want)
    while rows % t:
        t //= 2
    return t


def _forward(x_prompt, x_sample, sample_states, weights):
    bp, tp, d = x_prompt.shape
    bs, ts, _ = x_sample.shape
    zero_states = (jnp.zeros((bp, 2, d), F32), jnp.zeros((bp, 3 * d + R_SMALL), F32),
                   jnp.zeros((bp, RW_HEADS, RW_HEAD_DIM, RW_HEAD_DIM), F32),
                   jnp.zeros((bp, 3, 3 * d), F32), jnp.zeros((bp, GDN_HEADS, GDN_HEAD_DIM, GDN_HEAD_DIM), F32))
    y_p = x_prompt.reshape(bp * tp, d)
    y_s = x_sample.reshape(bs * ts, d)
    new_p = [[] for _ in range(5)]
    new_s = [[] for _ in range(5)]
    n_layers = weights[0].shape[0]
    w_in = weights[4]
    w_main = w_in.astype(BF16)
    w_gates = w_in[:, :, COL_GATES:P_IN].astype(BF16)
    for l in range(n_layers):
        prm = _prep_layer(l, *weights)
        y_p, st_p = _layer(y_p, l, bp, tp, min(CHUNK, tp), zero_states, prm, w_main, w_gates,
                           _row_tile(bp * tp, 1024), _row_tile(bp * tp, 256))
        y_s, st_s = _layer(y_s, l, bs, ts, min(CHUNK, ts), tuple(s[l] for s in sample_states), prm, w_main, w_gates,
                           _row_tile(bs * ts, 1024), _row_tile(bs * ts, 256))
        for i in range(5):
            new_p[i].append(st_p[i])
            new_s[i].append(st_s[i])
    outs_p = [jnp.stack(t, axis=0) for t in new_p]
    outs_s = [jnp.stack(t, axis=0) for t in new_s]
    return (y_p.reshape(bp, tp, d), y_s.reshape(bs, ts, d), *outs_p, *outs_s)


def kernel(x_prompt, x_sample, state_conv_a, state_rwkv_shift, state_rwkv_wkv, state_gdn_conv, state_gdn_ssm,
           norm_mix_pre, norm_mix_post, norm_mlp_pre, norm_mlp_post, w_in, conv_a_w, rwkv_mu, rwkv_w0, rwkv_w2,
           rwkv_a0, rwkv_a2, rwkv_g2, rwkv_k_k, rwkv_k_a, rwkv_r_k, rwkv_ln_w, rwkv_ln_b, gdn_conv_w, gdn_a_log,
           gdn_dt_bias, gdn_norm_w, w_o, w_ff1, w_ff2):
    sample_states = (state_conv_a, state_rwkv_shift, state_rwkv_wkv, state_gdn_conv, state_gdn_ssm)
    weights = (norm_mix_pre, norm_mix_post, norm_mlp_pre, norm_mlp_post, w_in, conv_a_w, rwkv_mu, rwkv_w0, rwkv_w2,
               rwkv_a0, rwkv_a2, rwkv_g2, rwkv_k_k, rwkv_k_a, rwkv_r_k, rwkv_ln_w, rwkv_ln_b, gdn_conv_w,
               gdn_a_log, gdn_dt_bias, gdn_norm_w, w_o, w_ff1, w_ff2)
    return _forward(x_prompt, x_sample, sample_states, weights)
```

```python
import functools

import jax
import jax.numpy as jnp
from jax import lax
from jax.experimental import pallas as pl
from jax.experimental.pallas import tpu as pltpu

F32 = jnp.float32
BF16 = jnp.bfloat16

D_MODEL = 1024
DEPTH = 4
CHUNK = 64
NORM_EPS = 1e-6
L2_EPS = 1e-12
RW_HEAD_DIM = 64
RW_HEADS = D_MODEL // RW_HEAD_DIM
RW_PAIRS = RW_HEADS // 2
R_DECAY = D_MODEL // 16
R_AAA = D_MODEL // 16
R_GATE = D_MODEL // 8
R_SMALL = R_DECAY + R_AAA + R_GATE
RW_GN_EPS = 64e-5
GDN_HEAD_DIM = 128
GDN_HEADS = D_MODEL // GDN_HEAD_DIM
D_FF = 4 * D_MODEL
LANES = 128
HALO = 8
MIXER_SUB = 4

COL_CONV = 0
COL_RKV = 3 * D_MODEL
COL_SMALL = 6 * D_MODEL
COL_QKV = COL_SMALL + R_SMALL
COL_Z = COL_QKV + 3 * D_MODEL
COL_GAB = COL_Z + D_MODEL
COL_GATES = COL_GAB + 2 * GDN_HEADS
P_IN = COL_GATES + 3 * D_MODEL
MAIN_TN = 14 * LANES
P_MAIN = ((COL_GAB + LANES + MAIN_TN - 1) // MAIN_TN) * MAIN_TN
GATES_TN = 3 * D_MODEL // 2

VMEM_LIMIT = 56 * 1024 * 1024

_NN = (((1,), (0,)), ((), ()))
_NT = (((1,), (1,)), ((), ()))
_TN = (((0,), (0,)), ((), ()))


def _mm(a, b, dims=_NN):
    return lax.dot_general(a.astype(BF16), b.astype(BF16), dims, preferred_element_type=F32)


def _split3(x):
    hi = x.astype(BF16)
    r1 = x - hi.astype(F32)
    mid = r1.astype(BF16)
    lo = (r1 - mid.astype(F32)).astype(BF16)
    return hi, mid, lo


def _mm_exact_lhs(a_bf16, x):
    out = None
    for q in _split3(x):
        t = lax.dot_general(a_bf16, q, _NN, preferred_element_type=F32)
        out = t if out is None else out + t
    return out


def _sigmoid(x):
    return 1.0 / (1.0 + jnp.exp(-x))


def _softplus(x):
    return jnp.maximum(x, 0.0) + jnp.log(1.0 + jnp.exp(-jnp.abs(x)))


def _silu(x):
    return x * _sigmoid(x)


def _rms(x, gain):
    return x * lax.rsqrt(jnp.mean(x * x, axis=-1, keepdims=True) + NORM_EPS) * gain


def _proj_kernel(x_ref, g_ref, w_ref, o_ref, wb_ref):
    @pl.when(pl.program_id(1) == 0)
    def _cast():
        wb_ref[...] = w_ref[...].astype(BF16)

    h = _rms(x_ref[...], g_ref[...])
    o_ref[...] = jnp.dot(h.astype(BF16), wb_ref[...], preferred_element_type=F32)


def _proj(x, gain, w_all, layer, n_cols, tn, tm):
    rows = x.shape[0]
    return pl.pallas_call(
        _proj_kernel,
        grid=(n_cols // tn, rows // tm),
        in_specs=[
            pl.BlockSpec((tm, D_MODEL), lambda j, i: (i, 0)),
            pl.BlockSpec((1, D_MODEL), lambda j, i: (0, 0)),
            pl.BlockSpec((None, D_MODEL, tn), lambda j, i: (layer, 0, j)),
        ],
        out_specs=pl.BlockSpec((tm, tn), lambda j, i: (i, j)),
        out_shape=jax.ShapeDtypeStruct((rows, n_cols), F32),
        scratch_shapes=[pltpu.VMEM((D_MODEL, tn), BF16)],
        compiler_params=pltpu.CompilerParams(
            dimension_semantics=("arbitrary", "arbitrary"), vmem_limit_bytes=VMEM_LIMIT),
        name="proj",
    )(x, gain, w_all)


def _outffn_kernel(m_ref, x_ref, wo_ref, w1_ref, w2_ref, g1_ref, g2_ref, g3_ref, o_ref):
    mo = jnp.dot(m_ref[...], wo_ref[...], preferred_element_type=F32)
    x = x_ref[...] + _rms(mo, g1_ref[...])
    h2 = _rms(x, g2_ref[...])
    f = jnp.dot(h2.astype(BF16), w1_ref[...], preferred_element_type=F32)
    f = jnp.square(jnp.maximum(f, 0.0))
    f = jnp.dot(f.astype(BF16), w2_ref[...], preferred_element_type=F32)
    o_ref[...] = x + _rms(f, g3_ref[...])


def _outffn(m, x, wo, w1, w2, g_post, g_pre2, g_post2, tm):
    rows = x.shape[0]
    const = lambda i: (0, 0)
    return pl.pallas_call(
        _outffn_kernel,
        grid=(rows // tm,),
        in_specs=[
            pl.BlockSpec((tm, D_MODEL), lambda i: (i, 0)),
            pl.BlockSpec((tm, D_MODEL), lambda i: (i, 0)),
            pl.BlockSpec((D_MODEL, D_MODEL), const),
            pl.BlockSpec((D_MODEL, D_FF), const),
            pl.BlockSpec((D_FF, D_MODEL), const),
            pl.BlockSpec((1, D_MODEL), const),
            pl.BlockSpec((1, D_MODEL), const),
            pl.BlockSpec((1, D_MODEL), const),
        ],
        out_specs=pl.BlockSpec((tm, D_MODEL), lambda i: (i, 0)),
        out_shape=jax.ShapeDtypeStruct((rows, D_MODEL), F32),
        compiler_params=pltpu.CompilerParams(
            dimension_semantics=("arbitrary",), vmem_limit_bytes=VMEM_LIMIT),
        name="outffn",
    )(m, x, wo, w1, w2, g_post, g_pre2, g_post2)


INV_BASE = 16


def _wide_unit_lower_inverse(mats, size, stack_w, out):
    base = min(size, INV_BASE)
    row = lax.broadcasted_iota(jnp.int32, (size, 2 * size), 0)
    col = lax.broadcasted_iota(jnp.int32, (size, 2 * size), 1) % size
    eye = (row == col).astype(F32)
    ps = [jnp.where((row // base) == (col // base), a, 0.0) if base < size else a for a in mats]
    ts = [eye - p for p in ps]
    steps = base.bit_length() - 2
    if steps:
        ps = [_mm(p, stack_w(p)) for p in ps]
        yield
    for step in range(steps):
        sps = [stack_w(p) for p in ps]
        ts = [t + _mm(t, sp) for t, sp in zip(ts, sps)]
        if step + 1 < steps:
            ps = [_mm(p, sp) for p, sp in zip(ps, sps)]
        yield
    b = base
    while b < size:
        off = ((row // (2 * b)) == (col // (2 * b))) & ((row // b) != (col // b))
        lt = [_mm(jnp.where(off, a, 0.0), stack_w(t)) for a, t in zip(mats, ts)]
        yield
        ts = [t - _mm(t, stack_w(x)) for t, x in zip(ts, lt)]
        yield
        b *= 2
    out.extend(ts)


def _mixer_kernel(chunk, n_sub,
                  main_ref, gab_ref, gates_ref,
                  st_conva_ref, st_shrkv_ref, st_shsm_ref, st_wkv_ref, st_gconv_ref, st_ssm_ref,
                  convw_ref, mu_rkv_ref, mu_sm_ref, w0_ref, w2_ref, a0_ref, a2_ref, g2_ref,
                  kk_ref, ka_ref, rk_ref, lnw_ref, lnb_ref, gconvw_ref, alog_ref, dtb_ref, gnw_ref,
                  m_ref, o_conva_ref, o_shrkv_ref, o_shsm_ref, o_wkv_ref, o_gconv_ref, o_ssm_ref,
                  ext_conv, ext_qkv, ext_rkv, ext_sm, s_wkv, s_ssm):
    C = chunk
    R = n_sub * C
    W2 = 2 * C
    t_idx = pl.program_id(1)
    n_t = pl.num_programs(1)
    conv_ref = main_ref.at[:, COL_CONV:COL_CONV + 3 * D_MODEL]
    rkv_ref = main_ref.at[:, COL_RKV:COL_RKV + 3 * D_MODEL]
    small_ref = main_ref.at[:, COL_SMALL:COL_SMALL + R_SMALL]
    qkv_ref = main_ref.at[:, COL_QKV:COL_QKV + 3 * D_MODEL]
    z_ref = main_ref.at[:, COL_Z:COL_Z + D_MODEL]

    @pl.when(t_idx == 0)
    def _init():
        ext_conv[0:HALO, :] = jnp.zeros((HALO, D_MODEL), F32)
        ext_conv[HALO - 2:HALO, :] = st_conva_ref[0]
        ext_qkv[0:HALO, :] = jnp.zeros((HALO, 3 * D_MODEL), F32)
        ext_qkv[HALO - 3:HALO, :] = st_gconv_ref[0]
        ext_rkv[0:HALO, :] = jnp.zeros((HALO, 3 * D_MODEL), F32)
        ext_rkv[HALO - 1:HALO, :] = st_shrkv_ref[0]
        ext_sm[0:HALO, :] = jnp.zeros((HALO, R_SMALL), F32)
        ext_sm[HALO - 1:HALO, :] = st_shsm_ref[0]
        s_wkv[...] = st_wkv_ref[0]
        s_ssm[...] = st_ssm_ref[0]

    tri_incl = (lax.broadcasted_iota(jnp.int32, (C, C), 0) >= lax.broadcasted_iota(jnp.int32, (C, C), 1)).astype(BF16)

    li = lax.broadcasted_iota(jnp.int32, (LANES, LANES), 0) // RW_HEAD_DIM
    lj = lax.broadcasted_iota(jnp.int32, (LANES, LANES), 1) // RW_HEAD_DIM
    bd128 = (li == lj).astype(F32)
    bd128_bf = bd128.astype(BF16)

    def seg_sum(x):
        xb = x.astype(BF16)
        return jnp.concatenate(
            [lax.dot_general(xb[:, j * LANES:(j + 1) * LANES], bd128_bf, _NN, preferred_element_type=F32)
             for j in range(D_MODEL // LANES)], axis=1)

    t_w = lax.broadcasted_iota(jnp.int32, (C, W2), 0)
    s_w = lax.broadcasted_iota(jnp.int32, (C, W2), 1) % C
    strict_w = t_w > s_w
    incl_w = t_w >= s_w
    lane_h = lax.broadcasted_iota(jnp.int32, (1, LANES), 1)
    mh0 = (lane_h < RW_HEAD_DIM).astype(BF16)
    mh1 = (lane_h >= RW_HEAD_DIM).astype(BF16)
    lane_w = lax.broadcasted_iota(jnp.int32, (1, W2), 1)
    mw0 = (lane_w < C).astype(BF16)
    mw1 = (lane_w >= C).astype(BF16)

    def stack_h(x):
        xb = x.astype(BF16)
        return jnp.concatenate([xb * mh0, xb * mh1], axis=0)

    def stack_w(x):
        xb = x.astype(BF16)
        return jnp.concatenate([xb * mw0, xb * mw1], axis=0)

    def blockdiag2(x0, x1):
        z = jnp.zeros(x0.shape, BF16)
        return jnp.concatenate([jnp.concatenate([x0.astype(BF16), z], axis=1),
                                jnp.concatenate([z, x1.astype(BF16)], axis=1)], axis=0)

    def pair(xs, j):
        return jnp.concatenate([xs[2 * j], xs[2 * j + 1]], axis=1)

    PR = range(RW_PAIRS)
    HD = range(GDN_HEADS)
    GP = range(GDN_HEADS // 2)
    psl = [slice(p * LANES, (p + 1) * LANES) for p in PR]

    ext_conv[HALO:HALO + R, :] = conv_ref[:, D_MODEL:2 * D_MODEL] * conv_ref[:, 2 * D_MODEL:3 * D_MODEL]
    ext_rkv[HALO:HALO + R, :] = rkv_ref[...]
    ext_sm[HALO:HALO + R, :] = small_ref[...]
    ext_qkv[HALO:HALO + R, :] = qkv_ref[...]

    def prep(sub, pre):
        r0 = sub * C
        e0 = HALO + r0
        rows = slice(r0, r0 + C)

        def back(ext, k, c0, width):
            blk = ext[e0 - HALO:e0 + C, c0:c0 + width]
            return pltpu.roll(blk, k, 0)[HALO:HALO + C]

        pre["ya"] = conv_ref[rows, 0:D_MODEL] * (
            convw_ref[0:1, :] * back(ext_conv, 2, 0, D_MODEL)
            + convw_ref[1:2, :] * back(ext_conv, 1, 0, D_MODEL)
            + convw_ref[2:3, :] * ext_conv[e0:e0 + C, :]) * _sigmoid(gates_ref[rows, 0:D_MODEL])
        yield

        p_sm = small_ref[rows, :]
        ps_sm = p_sm + (back(ext_sm, 1, 0, R_SMALL) - p_sm) * mu_sm_ref[...]
        xw = ps_sm[:, 0:R_DECAY]
        xa = ps_sm[:, R_DECAY:R_DECAY + R_AAA]
        xg = ps_sm[:, R_DECAY + R_AAA:R_SMALL]
        w_log = -_softplus(-(w0_ref[...] + _mm(jnp.tanh(xw), w2_ref[...]))) - 0.5
        logd = -jnp.exp(w_log)
        yield
        cum = _mm_exact_lhs(tri_incl, logd)
        a = _sigmoid(a0_ref[...] + _mm(xa, a2_ref[...]))
        pre["g"] = _mm(_sigmoid(xg), g2_ref[...])
        yield

        def shifted(c0):
            p = rkv_ref[rows, c0:c0 + D_MODEL]
            return p + (back(ext_rkv, 1, c0, D_MODEL) - p) * mu_rkv_ref[:, c0:c0 + D_MODEL]

        k = shifted(D_MODEL)
        yield
        kk0 = k * kk_ref[...]
        kk = kk0 * lax.rsqrt(seg_sum(kk0 * kk0) + L2_EPS)
        yield
        k2 = k * (1.0 + (a - 1.0) * ka_ref[...])
        kka = kk * a
        e_inv = jnp.exp(-cum)
        yield
        kk_g = (kk * jnp.exp(cum - logd)).astype(BF16)
        pre["kk_g"] = [kk_g[:, s] for s in psl]
        pre["kk_g_st"] = [stack_h(kk_g[:, s]) for s in psl]
        yield
        k_inv = (k2 * e_inv).astype(BF16)
        pre["k_inv"] = [k_inv[:, s] for s in psl]
        pre["k_inv_st"] = [stack_h(k_inv[:, s]) for s in psl]
        yield
        kka_inv = (kka * e_inv).astype(BF16)
        pre["kka_inv"] = [kka_inv[:, s] for s in psl]
        pre["kka_inv_st"] = [stack_h(kka_inv[:, s]) for s in psl]
        yield
        r = shifted(0)
        r_g = (r * jnp.exp(cum)).astype(BF16)
        pre["r_g"] = [r_g[:, s] for s in psl]
        yield
        v = shifted(2 * D_MODEL)
        pre["v"] = [v[:, s].astype(BF16) for s in psl]
        pre["v_st"] = [stack_h(v[:, s]) for s in psl]
        pre["gam_end"] = jnp.exp(cum[C - 1:C, :])
        yield
        pre["bonus"] = seg_sum(r * k2 * rk_ref[...]) * v
        yield

        gab = gab_ref[rows, :]
        g_log = -jnp.exp(alog_ref[...]) * _softplus(gab + dtb_ref[...])
        beta_all = _sigmoid(gab)
        g_cum = _mm_exact_lhs(tri_incl, g_log)
        g_cum_t = g_cum.T
        yield

        def conv_silu(c0):
            cq = (gconvw_ref[0:1, c0:c0 + LANES] * back(ext_qkv, 3, c0, LANES)
                  + gconvw_ref[1:2, c0:c0 + LANES] * back(ext_qkv, 2, c0, LANES)
                  + gconvw_ref[2:3, c0:c0 + LANES] * back(ext_qkv, 1, c0, LANES)
                  + gconvw_ref[3:4, c0:c0 + LANES] * ext_qkv[e0:e0 + C, c0:c0 + LANES])
            return _silu(cq)

        qn, kn, vb, kb, kbe, qe, k_dec, blk_dec = [], [], [], [], [], [], [], []
        pre.update(qn=qn, kn=kn, vb=vb, kb=kb, kbe=kbe, qe=qe, k_dec=k_dec, blk_dec=blk_dec, decay_w=[])
        for h in HD:
            qh = conv_silu(h * LANES)
            kh = conv_silu(D_MODEL + h * LANES)
            vh = conv_silu(2 * D_MODEL + h * LANES)
            qn_h = qh * lax.rsqrt(jnp.sum(qh * qh, axis=-1, keepdims=True) + L2_EPS) * (GDN_HEAD_DIM ** -0.5)
            kn_h = kh * lax.rsqrt(jnp.sum(kh * kh, axis=-1, keepdims=True) + L2_EPS)
            gc = g_cum[:, h:h + 1]
            beta = beta_all[:, GDN_HEADS + h:GDN_HEADS + h + 1]
            e_g = jnp.exp(gc)
            g_last = g_cum[C - 1:C, h:h + 1]
            kb_h = kn_h * beta
            qn.append(qn_h.astype(BF16))
            kn.append(kn_h.astype(BF16))
            kb.append(kb_h.astype(BF16))
            vb.append((vh * beta).astype(BF16))
            kbe.append((kb_h * e_g).astype(BF16))
            qe.append((qn_h * e_g).astype(BF16))
            k_dec.append((kn_h * jnp.exp(g_last - gc)).astype(BF16))
            blk_dec.append(jnp.exp(g_last))
            if h % 2 == 1:
                gc_w = jnp.where(lane_w < C, g_cum[:, h - 1:h], gc)
                gr_w = jnp.concatenate([g_cum_t[h - 1:h, :], g_cum_t[h:h + 1, :]], axis=1)
                pre["decay_w"].append(jnp.where(incl_w, jnp.exp(jnp.where(incl_w, gc_w - gr_w, 0.0)), 0.0))
            yield
        pre["zg"] = _silu(z_ref[rows, :]) * _sigmoid(gates_ref[rows, 2 * D_MODEL:3 * D_MODEL])
        yield
        pre["gate_b"] = _sigmoid(gates_ref[rows, D_MODEL:2 * D_MODEL])
        yield

    def independent(pre, mid):
        kn_bd = [blockdiag2(pre["kn"][2 * j], pre["kn"][2 * j + 1]) for j in GP]
        a1 = [jnp.where(strict_w, _mm(pre["kk_g"][p], pre["k_inv_st"][p], _NT), 0.0) for p in PR]
        yield
        a2 = [jnp.where(strict_w, _mm(pre["kk_g"][p], pre["kka_inv_st"][p], _NT), 0.0) for p in PR]
        yield
        a_g = [jnp.where(strict_w, _mm(pair(pre["kb"], j), kn_bd[j], _NT) * pre["decay_w"][j], 0.0) for j in GP]
        yield
        mid["b1"] = [jnp.where(incl_w, _mm(pre["r_g"][p], pre["k_inv_st"][p], _NT), 0.0) for p in PR]
        yield
        mid["b2"] = [jnp.where(incl_w, _mm(pre["r_g"][p], pre["kka_inv_st"][p], _NT), 0.0) for p in PR]
        yield
        mid["attn"] = [_mm(pair(pre["qn"], j), kn_bd[j], _NT) * pre["decay_w"][j] for j in GP]
        yield
        t_all = []
        yield from _wide_unit_lower_inverse(a2 + a_g, C, stack_w, t_all)
        t_rw, t_g = t_all[:RW_PAIRS], t_all[RW_PAIRS:]
        mid["w_rw"] = [_mm(t_rw[p], pre["kk_g_st"][p]) for p in PR]
        yield
        av = [_mm(a1[p], pre["v_st"][p]) for p in PR]
        yield
        mid["u_g"] = [_mm(t_g[j], blockdiag2(pre["vb"][2 * j], pre["vb"][2 * j + 1])) for j in GP]
        yield
        mid["w_g"] = [_mm(t_g[j], blockdiag2(pre["kbe"][2 * j], pre["kbe"][2 * j + 1])) for j in GP]
        yield
        mid["u0"] = [_mm(t_rw[p], stack_h(av[p])) for p in PR]
        yield

    def dependent(pre, mid, res):
        st_rw = [s_wkv[p] for p in PR]
        st_g = [s_ssm[h] for h in HD]
        st_gbd = [blockdiag2(st_g[2 * j], st_g[2 * j + 1]) for j in GP]
        u = [_mm(mid["w_rw"][p], st_rw[p], _NT) + mid["u0"][p] for p in PR]
        yield
        rs = [_mm(pre["r_g"][p], st_rw[p], _NT) for p in PR]
        yield
        v_new = [mid["u_g"][j] - _mm(mid["w_g"][j], st_gbd[j]) for j in GP]
        yield
        qs = [_mm(pair(pre["qe"], j), st_gbd[j]) for j in GP]
        yield
        res["y"] = [rs[p] + _mm(mid["b1"][p], pre["v_st"][p]) - _mm(mid["b2"][p], stack_h(u[p])) for p in PR]
        yield
        res["o"] = [qs[j] + _mm(mid["attn"][j], blockdiag2(v_new[j][:, 0:LANES], v_new[j][:, LANES:2 * LANES]))
                    for j in GP]
        yield
        for p in PR:
            upd = _mm(jnp.concatenate([pre["v"][p], (-u[p]).astype(BF16)], axis=0),
                      jnp.concatenate([pre["k_inv"][p], pre["kka_inv"][p]], axis=0), _TN)
            s_wkv[p] = (st_rw[p] + upd * bd128) * pre["gam_end"][:, psl[p]]
        yield
        for h in HD:
            vn = v_new[h // 2][:, (h % 2) * LANES:(h % 2 + 1) * LANES]
            s_ssm[h] = st_g[h] * pre["blk_dec"][h] + _mm(pre["k_dec"][h], vn, _TN)
        yield

    def post(sub, pre, res):
        y_rw = jnp.concatenate(res["y"], axis=1)
        inv_n = 1.0 / RW_HEAD_DIM
        y_c = y_rw - seg_sum(y_rw) * inv_n
        yield
        y_n = y_c * lax.rsqrt(seg_sum(y_c * y_c) * inv_n + RW_GN_EPS)
        yield
        y_n = y_n * lnw_ref[...] + lnb_ref[...]
        yb = (y_n + pre["bonus"]) * pre["g"]
        yield
        o_all = jnp.concatenate(res["o"], axis=1)
        yc = jnp.concatenate(
            [o_all[:, s] * lax.rsqrt(jnp.mean(o_all[:, s] * o_all[:, s], axis=-1, keepdims=True) + NORM_EPS)
             * gnw_ref[...] for s in psl], axis=1)
        yield
        m = pre["ya"] + pre["gate_b"] * yb + yc * pre["zg"]
        m_ref[sub * C:(sub + 1) * C, :] = m.astype(m_ref.dtype)
        yield

    def weave(*gens):
        gens = [g for g in gens if g is not None]
        while gens:
            for g in list(gens):
                try:
                    next(g)
                except StopIteration:
                    gens.remove(g)

    pres = [dict() for _ in range(n_sub)]
    mids = [dict() for _ in range(n_sub)]
    ress = [dict() for _ in range(n_sub)]

    def chain(*gens):
        for g in gens:
            yield from g

    groups = [list(range(i, min(i + 2, n_sub))) for i in range(0, n_sub, 2)]
    for s in groups[0]:
        weave(prep(s, pres[s]))
    for gi, grp in enumerate(groups):
        fill = []
        if gi + 1 < len(groups):
            fill += [prep(s, pres[s]) for s in groups[gi + 1]]
        if gi:
            fill += [post(s, pres[s], ress[s]) for s in groups[gi - 1]]
        weave(*[independent(pres[s], mids[s]) for s in grp], chain(*fill))
        weave(chain(*[dependent(pres[s], mids[s], ress[s]) for s in grp]))
    weave(chain(*[post(s, pres[s], ress[s]) for s in groups[-1]]))

    ext_conv[0:HALO, :] = ext_conv[R:R + HALO, :]
    ext_qkv[0:HALO, :] = ext_qkv[R:R + HALO, :]
    ext_rkv[0:HALO, :] = ext_rkv[R:R + HALO, :]
    ext_sm[0:HALO, :] = ext_sm[R:R + HALO, :]

    @pl.when(t_idx == n_t - 1)
    def _finish():
        o_conva_ref[0] = ext_conv[HALO - 2:HALO, :]
        o_gconv_ref[0] = ext_qkv[HALO - 3:HALO, :]
        o_shrkv_ref[0] = ext_rkv[HALO - 1:HALO, :]
        o_shsm_ref[0] = ext_sm[HALO - 1:HALO, :]
        o_wkv_ref[0] = s_wkv[...]
        o_ssm_ref[0] = s_ssm[...]


def _mixer(proj_main, proj_gates, states, prm, nb, t_len, chunk):
    n_sub = MIXER_SUB if t_len % (MIXER_SUB * chunk) == 0 else 1
    blk_rows = n_sub * chunk
    n_t = t_len // blk_rows
    rows = nb * t_len

    def pspec(width, col):
        blk = col // width
        return pl.BlockSpec((blk_rows, width), lambda b, t: (b * n_t + t, blk))

    def sspec(shape):
        nd = len(shape)
        return pl.BlockSpec((1,) + tuple(shape[1:]), lambda b, t: (b,) + (0,) * (nd - 1))

    def wspec(arr):
        return pl.BlockSpec(arr.shape, lambda b, t: (0,) * arr.ndim)

    st_names = ("conva", "shrkv", "shsm", "wkv", "gconv", "ssm")
    st_list = [states[n] for n in st_names]
    w_names = ("conv_a_w", "mu_rkv", "mu_sm", "w0", "w2", "a0", "a2", "g2", "k_k", "k_a", "r_k",
               "ln_w", "ln_b", "gdn_conv_w", "a_log", "dt_bias", "gdn_norm_w")
    w_list = [prm[n] for n in w_names]
    in_specs = [pspec(COL_GAB, 0), pspec(LANES, COL_GAB), pspec(3 * D_MODEL, 0)] + [
        sspec(s.shape) for s in st_list] + [wspec(w) for w in w_list]
    out_shapes = [jax.ShapeDtypeStruct((rows, D_MODEL), BF16)] + [
        jax.ShapeDtypeStruct(s.shape, F32) for s in st_list]
    out_specs = [pl.BlockSpec((blk_rows, D_MODEL), lambda b, t: (b * n_t + t, 0))] + [
        sspec(s.shape) for s in st_list]
    scratch = [
        pltpu.VMEM((HALO + blk_rows, D_MODEL), F32),
        pltpu.VMEM((HALO + blk_rows, 3 * D_MODEL), F32),
        pltpu.VMEM((HALO + blk_rows, 3 * D_MODEL), F32),
        pltpu.VMEM((HALO + blk_rows, R_SMALL), F32),
        pltpu.VMEM((RW_PAIRS, LANES, LANES), F32),
        pltpu.VMEM((GDN_HEADS, GDN_HEAD_DIM, GDN_HEAD_DIM), F32),
    ]
    outs = pl.pallas_call(
        functools.partial(_mixer_kernel, chunk, n_sub),
        grid=(nb, n_t),
        in_specs=in_specs,
        out_specs=out_specs,
        out_shape=out_shapes,
        scratch_shapes=scratch,
        compiler_params=pltpu.CompilerParams(
            dimension_semantics=("arbitrary", "arbitrary"), vmem_limit_bytes=VMEM_LIMIT),
        name="mixer",
    )(proj_main, proj_main, proj_gates, *st_list, *w_list)
    return outs[0], dict(zip(st_names, outs[1:]))


def _prep_layer(l, norm_mix_pre, norm_mix_post, norm_mlp_pre, norm_mlp_post, w_in, conv_a_w, rwkv_mu, rwkv_w0,
                rwkv_w2, rwkv_a0, rwkv_a2, rwkv_g2, rwkv_k_k, rwkv_k_a, rwkv_r_k, rwkv_ln_w, rwkv_ln_b,
                gdn_conv_w, gdn_a_log, gdn_dt_bias, gdn_norm_w, w_o, w_ff1, w_ff2):
    d = D_MODEL
    row = lambda a: a.reshape(1, -1).astype(F32)
    pad_heads = lambda a: jnp.pad(a.astype(F32), (0, LANES - GDN_HEADS)).reshape(1, LANES)
    mu = rwkv_mu[l]
    return {
        "g_pre": row(norm_mix_pre[l]), "g_post": row(norm_mix_post[l]),
        "g_pre2": row(norm_mlp_pre[l]), "g_post2": row(norm_mlp_post[l]),
        "w_o": w_o[l].astype(BF16), "w_ff1": w_ff1[l].astype(BF16), "w_ff2": w_ff2[l].astype(BF16),
        "conv_a_w": conv_a_w[l], "mu_rkv": row(mu[0:3 * d]), "mu_sm": row(mu[3 * d:]),
        "w0": row(rwkv_w0[l]), "w2": rwkv_w2[l].astype(BF16), "a0": row(rwkv_a0[l]),
        "a2": rwkv_a2[l].astype(BF16), "g2": rwkv_g2[l].astype(BF16),
        "k_k": row(rwkv_k_k[l]), "k_a": row(rwkv_k_a[l]), "r_k": row(rwkv_r_k[l]),
        "ln_w": row(rwkv_ln_w[l]), "ln_b": row(rwkv_ln_b[l]), "gdn_conv_w": gdn_conv_w[l],
        "a_log": pad_heads(gdn_a_log[l]), "dt_bias": pad_heads(gdn_dt_bias[l]), "gdn_norm_w": row(gdn_norm_w[l]),
    }


def _pack_wkv(s):
    b = s.shape[0]
    s = s.reshape(b, RW_PAIRS, 2, RW_HEAD_DIM, RW_HEAD_DIM)
    zero = jnp.zeros_like(s[:, :, 0])
    top = jnp.concatenate([s[:, :, 0], zero], axis=-1)
    bot = jnp.concatenate([zero, s[:, :, 1]], axis=-1)
    return jnp.concatenate([top, bot], axis=-2)


def _unpack_wkv(s):
    b = s.shape[0]
    h = RW_HEAD_DIM
    return jnp.stack([s[:, :, 0:h, 0:h], s[:, :, h:, h:]], axis=2).reshape(b, RW_HEADS, h, h)


def _layer(x, l, nb, t_len, chunk, st, prm, w_main, w_gates, tm_proj, tm_ffn):
    conv_a, rw_shift, rw_wkv, gdn_conv, gdn_ssm = st
    states = {
        "conva": conv_a, "shrkv": rw_shift[:, None, 0:3 * D_MODEL], "shsm": rw_shift[:, None, 3 * D_MODEL:],
        "wkv": _pack_wkv(rw_wkv), "gconv": gdn_conv, "ssm": gdn_ssm,
    }
    proj_main = _proj(x, prm["g_pre"], w_main, l, P_MAIN, MAIN_TN, tm_proj)
    proj_gates = _proj(x, prm["g_pre"], w_gates, l, 3 * D_MODEL, GATES_TN, tm_proj)
    m, new = _mixer(proj_main, proj_gates, states, prm, nb, t_len, chunk)
    x = _outffn(m, x, prm["w_o"], prm["w_ff1"], prm["w_ff2"], prm["g_post"], prm["g_pre2"], prm["g_post2"], tm_ffn)
    new_st = (new["conva"], jnp.concatenate([new["shrkv"][:, 0], new["shsm"][:, 0]], axis=-1),
              _unpack_wkv(new["wkv"]), new["gconv"], new["ssm"])
    return x, new_st


def _row_tile(rows, want):
    t = min(rows, want)
    while rows % t:
        t //= 2
    return t


def _forward(x_prompt, x_sample, sample_states, weights):
    bp, tp, d = x_prompt.shape
    bs, ts, _ = x_sample.shape
    zero_states = (jnp.zeros((bp, 2, d), F32), jnp.zeros((bp, 3 * d + R_SMALL), F32),
                   jnp.zeros((bp, RW_HEADS, RW_HEAD_DIM, RW_HEAD_DIM), F32),
                   jnp.zeros((bp, 3, 3 * d), F32), jnp.zeros((bp, GDN_HEADS, GDN_HEAD_DIM, GDN_HEAD_DIM), F32))
    y_p = x_prompt.reshape(bp * tp, d)
    y_s = x_sample.reshape(bs * ts, d)
    new_p = [[] for _ in range(5)]
    new_s = [[] for _ in range(5)]
    n_layers = weights[0].shape[0]
    w_in = weights[4]
    w_main = w_in
    w_gates = w_in[:, :, COL_GATES:P_IN]
    for l in range(n_layers):
        prm = _prep_layer(l, *weights)
        y_p, st_p = _layer(y_p, l, bp, tp, min(CHUNK, tp), zero_states, prm, w_main, w_gates,
                           _row_tile(bp * tp, 1024), _row_tile(bp * tp, 256))
        y_s, st_s = _layer(y_s, l, bs, ts, min(CHUNK, ts), tuple(s[l] for s in sample_states), prm, w_main, w_gates,
                           _row_tile(bs * ts, 1024), _row_tile(bs * ts, 256))
        for i in range(5):
            new_p[i].append(st_p[i])
            new_s[i].append(st_s[i])
    outs_p = [jnp.stack(t, axis=0) for t in new_p]
    outs_s = [jnp.stack(t, axis=0) for t in new_s]
    return (y_p.reshape(bp, tp, d), y_s.reshape(bs, ts, d), *outs_p, *outs_s)


def kernel(x_prompt, x_sample, state_conv_a, state_rwkv_shift, state_rwkv_wkv, state_gdn_conv, state_gdn_ssm,
           norm_mix_pre, norm_mix_post, norm_mlp_pre, norm_mlp_post, w_in, conv_a_w, rwkv_mu, rwkv_w0, rwkv_w2,
           rwkv_a0, rwkv_a2, rwkv_g2, rwkv_k_k, rwkv_k_a, rwkv_r_k, rwkv_ln_w, rwkv_ln_b, gdn_conv_w, gdn_a_log,
           gdn_dt_bias, gdn_norm_w, w_o, w_ff1, w_ff2):
    sample_states = (state_conv_a, state_rwkv_shift, state_rwkv_wkv, state_gdn_conv, state_gdn_ssm)
    weights = (norm_mix_pre, norm_mix_post, norm_mlp_pre, norm_mlp_post, w_in, conv_a_w, rwkv_mu, rwkv_w0, rwkv_w2,
               rwkv_a0, rwkv_a2, rwkv_g2, rwkv_k_k, rwkv_k_a, rwkv_r_k, rwkv_ln_w, rwkv_ln_b, gdn_conv_w,
               gdn_a_log, gdn_dt_bias, gdn_norm_w, w_o, w_ff1, w_ff2)
    return _forward(x_prompt, x_sample, sample_states, weights)
```

```python
import functools

import jax
import jax.numpy as jnp
from jax import lax
from jax.experimental import pallas as pl
from jax.experimental.pallas import tpu as pltpu

F32 = jnp.float32
BF16 = jnp.bfloat16

D_MODEL = 1024
DEPTH = 4
CHUNK = 64
NORM_EPS = 1e-6
L2_EPS = 1e-12
RW_HEAD_DIM = 64
RW_HEADS = D_MODEL // RW_HEAD_DIM
RW_PAIRS = RW_HEADS // 2
R_DECAY = D_MODEL // 16
R_AAA = D_MODEL // 16
R_GATE = D_MODEL // 8
R_SMALL = R_DECAY + R_AAA + R_GATE
RW_GN_EPS = 64e-5
GDN_HEAD_DIM = 128
GDN_HEADS = D_MODEL // GDN_HEAD_DIM
D_FF = 4 * D_MODEL
LANES = 128
HALO = 8
MIXER_SUB = 4

COL_CONV = 0
COL_RKV = 3 * D_MODEL
COL_SMALL = 6 * D_MODEL
COL_QKV = COL_SMALL + R_SMALL
COL_Z = COL_QKV + 3 * D_MODEL
COL_GAB = COL_Z + D_MODEL
COL_GATES = COL_GAB + 2 * GDN_HEADS
P_IN = COL_GATES + 3 * D_MODEL
MAIN_TN = 14 * LANES
P_MAIN = ((COL_GAB + LANES + MAIN_TN - 1) // MAIN_TN) * MAIN_TN
GATES_TN = 3 * D_MODEL // 2

VMEM_LIMIT = 56 * 1024 * 1024

_NN = (((1,), (0,)), ((), ()))
_NT = (((1,), (1,)), ((), ()))
_TN = (((0,), (0,)), ((), ()))


def _mm(a, b, dims=_NN):
    return lax.dot_general(a.astype(BF16), b.astype(BF16), dims, preferred_element_type=F32)


def _split3(x):
    hi = x.astype(BF16)
    r1 = x - hi.astype(F32)
    mid = r1.astype(BF16)
    lo = (r1 - mid.astype(F32)).astype(BF16)
    return hi, mid, lo


def _mm_exact_lhs(a_bf16, x):
    out = None
    for q in _split3(x):
        t = lax.dot_general(a_bf16, q, _NN, preferred_element_type=F32)
        out = t if out is None else out + t
    return out


def _sigmoid(x):
    return 1.0 / (1.0 + jnp.exp(-x))


def _softplus(x):
    return jnp.maximum(x, 0.0) + jnp.log(1.0 + jnp.exp(-jnp.abs(x)))


def _silu(x):
    return x * _sigmoid(x)


def _rms(x, gain):
    return x * lax.rsqrt(jnp.mean(x * x, axis=-1, keepdims=True) + NORM_EPS) * gain


def _proj_kernel(x_ref, g_ref, w_ref, o_ref, wb_ref):
    @pl.when(pl.program_id(1) == 0)
    def _cast():
        wb_ref[...] = w_ref[...].astype(BF16)

    h = _rms(x_ref[...], g_ref[...])
    o_ref[...] = jnp.dot(h.astype(BF16), wb_ref[...], preferred_element_type=F32).astype(o_ref.dtype)


def _proj(x, gain, w_all, layer, n_cols, tn, tm):
    rows = x.shape[0]
    return pl.pallas_call(
        _proj_kernel,
        grid=(n_cols // tn, rows // tm),
        in_specs=[
            pl.BlockSpec((tm, D_MODEL), lambda j, i: (i, 0)),
            pl.BlockSpec((1, D_MODEL), lambda j, i: (0, 0)),
            pl.BlockSpec((None, D_MODEL, tn), lambda j, i: (layer, 0, j)),
        ],
        out_specs=pl.BlockSpec((tm, tn), lambda j, i: (i, j)),
        out_shape=jax.ShapeDtypeStruct((rows, n_cols), BF16),
        scratch_shapes=[pltpu.VMEM((D_MODEL, tn), BF16)],
        compiler_params=pltpu.CompilerParams(
            dimension_semantics=("arbitrary", "arbitrary"), vmem_limit_bytes=VMEM_LIMIT),
        name="proj",
    )(x, gain, w_all)


def _outffn_kernel(m_ref, x_ref, wo_ref, w1_ref, w2_ref, g1_ref, g2_ref, g3_ref, o_ref):
    mo = jnp.dot(m_ref[...], wo_ref[...], preferred_element_type=F32)
    x = x_ref[...] + _rms(mo, g1_ref[...])
    h2 = _rms(x, g2_ref[...])
    f = jnp.dot(h2.astype(BF16), w1_ref[...], preferred_element_type=F32)
    f = jnp.square(jnp.maximum(f, 0.0))
    f = jnp.dot(f.astype(BF16), w2_ref[...], preferred_element_type=F32)
    o_ref[...] = x + _rms(f, g3_ref[...])


def _outffn(m, x, wo, w1, w2, g_post, g_pre2, g_post2, tm):
    rows = x.shape[0]
    const = lambda i: (0, 0)
    return pl.pallas_call(
        _outffn_kernel,
        grid=(rows // tm,),
        in_specs=[
            pl.BlockSpec((tm, D_MODEL), lambda i: (i, 0)),
            pl.BlockSpec((tm, D_MODEL), lambda i: (i, 0)),
            pl.BlockSpec((D_MODEL, D_MODEL), const),
            pl.BlockSpec((D_MODEL, D_FF), const),
            pl.BlockSpec((D_FF, D_MODEL), const),
            pl.BlockSpec((1, D_MODEL), const),
            pl.BlockSpec((1, D_MODEL), const),
            pl.BlockSpec((1, D_MODEL), const),
        ],
        out_specs=pl.BlockSpec((tm, D_MODEL), lambda i: (i, 0)),
        out_shape=jax.ShapeDtypeStruct((rows, D_MODEL), F32),
        compiler_params=pltpu.CompilerParams(
            dimension_semantics=("arbitrary",), vmem_limit_bytes=VMEM_LIMIT),
        name="outffn",
    )(m, x, wo, w1, w2, g_post, g_pre2, g_post2)


INV_BASE = 16


def _wide_unit_lower_inverse(mats, size, stack_w, out):
    base = min(size, INV_BASE)
    row = lax.broadcasted_iota(jnp.int32, (size, 2 * size), 0)
    col = lax.broadcasted_iota(jnp.int32, (size, 2 * size), 1) % size
    eye = (row == col).astype(F32)
    ps = [jnp.where((row // base) == (col // base), a, 0.0) if base < size else a for a in mats]
    ts = [eye - p for p in ps]
    steps = base.bit_length() - 2
    if steps:
        ps = [_mm(p, stack_w(p)) for p in ps]
        yield
    for step in range(steps):
        sps = [stack_w(p) for p in ps]
        ts = [t + _mm(t, sp) for t, sp in zip(ts, sps)]
        if step + 1 < steps:
            ps = [_mm(p, sp) for p, sp in zip(ps, sps)]
        yield
    b = base
    while b < size:
        off = ((row // (2 * b)) == (col // (2 * b))) & ((row // b) != (col // b))
        lt = [_mm(jnp.where(off, a, 0.0), stack_w(t)) for a, t in zip(mats, ts)]
        yield
        ts = [t - _mm(t, stack_w(x)) for t, x in zip(ts, lt)]
        yield
        b *= 2
    out.extend(ts)


def _mixer_kernel(chunk, n_sub,
                  main_ref, gab_ref, gates_ref,
                  st_conva_ref, st_shrkv_ref, st_shsm_ref, st_wkv_ref, st_gconv_ref, st_ssm_ref,
                  convw_ref, mu_rkv_ref, mu_sm_ref, w0_ref, w2_ref, a0_ref, a2_ref, g2_ref,
                  kk_ref, ka_ref, rk_ref, lnw_ref, lnb_ref, gconvw_ref, alog_ref, dtb_ref, gnw_ref,
                  m_ref, o_conva_ref, o_shrkv_ref, o_shsm_ref, o_wkv_ref, o_gconv_ref, o_ssm_ref,
                  ext_conv, ext_qkv, ext_rkv, ext_sm, s_wkv, s_ssm):
    C = chunk
    R = n_sub * C
    W2 = 2 * C
    t_idx = pl.program_id(1)
    n_t = pl.num_programs(1)
    conv_ref = main_ref.at[:, COL_CONV:COL_CONV + 3 * D_MODEL]
    rkv_ref = main_ref.at[:, COL_RKV:COL_RKV + 3 * D_MODEL]
    small_ref = main_ref.at[:, COL_SMALL:COL_SMALL + R_SMALL]
    qkv_ref = main_ref.at[:, COL_QKV:COL_QKV + 3 * D_MODEL]
    z_ref = main_ref.at[:, COL_Z:COL_Z + D_MODEL]

    @pl.when(t_idx == 0)
    def _init():
        ext_conv[0:HALO, :] = jnp.zeros((HALO, D_MODEL), F32)
        ext_conv[HALO - 2:HALO, :] = st_conva_ref[0]
        ext_qkv[0:HALO, :] = jnp.zeros((HALO, 3 * D_MODEL), F32)
        ext_qkv[HALO - 3:HALO, :] = st_gconv_ref[0]
        ext_rkv[0:HALO, :] = jnp.zeros((HALO, 3 * D_MODEL), F32)
        ext_rkv[HALO - 1:HALO, :] = st_shrkv_ref[0]
        ext_sm[0:HALO, :] = jnp.zeros((HALO, R_SMALL), F32)
        ext_sm[HALO - 1:HALO, :] = st_shsm_ref[0]
        s_wkv[...] = st_wkv_ref[0]
        s_ssm[...] = st_ssm_ref[0]

    tri_incl = (lax.broadcasted_iota(jnp.int32, (C, C), 0) >= lax.broadcasted_iota(jnp.int32, (C, C), 1)).astype(BF16)

    li = lax.broadcasted_iota(jnp.int32, (LANES, LANES), 0) // RW_HEAD_DIM
    lj = lax.broadcasted_iota(jnp.int32, (LANES, LANES), 1) // RW_HEAD_DIM
    bd128 = (li == lj).astype(F32)
    bd128_bf = bd128.astype(BF16)

    def seg_sum(x):
        xb = x.astype(BF16)
        return jnp.concatenate(
            [lax.dot_general(xb[:, j * LANES:(j + 1) * LANES], bd128_bf, _NN, preferred_element_type=F32)
             for j in range(D_MODEL // LANES)], axis=1)

    t_w = lax.broadcasted_iota(jnp.int32, (C, W2), 0)
    s_w = lax.broadcasted_iota(jnp.int32, (C, W2), 1) % C
    strict_w = t_w > s_w
    incl_w = t_w >= s_w
    lane_h = lax.broadcasted_iota(jnp.int32, (1, LANES), 1)
    mh0 = (lane_h < RW_HEAD_DIM).astype(BF16)
    mh1 = (lane_h >= RW_HEAD_DIM).astype(BF16)
    lane_w = lax.broadcasted_iota(jnp.int32, (1, W2), 1)
    mw0 = (lane_w < C).astype(BF16)
    mw1 = (lane_w >= C).astype(BF16)

    def stack_h(x):
        xb = x.astype(BF16)
        return jnp.concatenate([xb * mh0, xb * mh1], axis=0)

    def stack_w(x):
        xb = x.astype(BF16)
        return jnp.concatenate([xb * mw0, xb * mw1], axis=0)

    def blockdiag2(x0, x1):
        z = jnp.zeros(x0.shape, BF16)
        return jnp.concatenate([jnp.concatenate([x0.astype(BF16), z], axis=1),
                                jnp.concatenate([z, x1.astype(BF16)], axis=1)], axis=0)

    def pair(xs, j):
        return jnp.concatenate([xs[2 * j], xs[2 * j + 1]], axis=1)

    PR = range(RW_PAIRS)
    HD = range(GDN_HEADS)
    GP = range(GDN_HEADS // 2)
    psl = [slice(p * LANES, (p + 1) * LANES) for p in PR]

    ext_conv[HALO:HALO + R, :] = (conv_ref[:, D_MODEL:2 * D_MODEL].astype(F32)
                                  * conv_ref[:, 2 * D_MODEL:3 * D_MODEL].astype(F32))
    ext_rkv[HALO:HALO + R, :] = rkv_ref[...].astype(F32)
    ext_sm[HALO:HALO + R, :] = small_ref[...].astype(F32)
    ext_qkv[HALO:HALO + R, :] = qkv_ref[...].astype(F32)

    def prep(sub, pre):
        r0 = sub * C
        e0 = HALO + r0
        rows = slice(r0, r0 + C)

        def back(ext, k, c0, width):
            blk = ext[e0 - HALO:e0 + C, c0:c0 + width]
            return pltpu.roll(blk, k, 0)[HALO:HALO + C]

        pre["ya"] = conv_ref[rows, 0:D_MODEL].astype(F32) * (
            convw_ref[0:1, :] * back(ext_conv, 2, 0, D_MODEL)
            + convw_ref[1:2, :] * back(ext_conv, 1, 0, D_MODEL)
            + convw_ref[2:3, :] * ext_conv[e0:e0 + C, :]) * _sigmoid(gates_ref[rows, 0:D_MODEL].astype(F32))
        yield

        p_sm = ext_sm[e0:e0 + C, :]
        ps_sm = p_sm + (back(ext_sm, 1, 0, R_SMALL) - p_sm) * mu_sm_ref[...]
        xw = ps_sm[:, 0:R_DECAY]
        xa = ps_sm[:, R_DECAY:R_DECAY + R_AAA]
        xg = ps_sm[:, R_DECAY + R_AAA:R_SMALL]
        w_log = -_softplus(-(w0_ref[...] + _mm(jnp.tanh(xw), w2_ref[...]))) - 0.5
        logd = -jnp.exp(w_log)
        yield
        cum = _mm_exact_lhs(tri_incl, logd)
        a = _sigmoid(a0_ref[...] + _mm(xa, a2_ref[...]))
        pre["g"] = _mm(_sigmoid(xg), g2_ref[...])
        yield

        def shifted(c0):
            p = ext_rkv[e0:e0 + C, c0:c0 + D_MODEL]
            return p + (back(ext_rkv, 1, c0, D_MODEL) - p) * mu_rkv_ref[:, c0:c0 + D_MODEL]

        k = shifted(D_MODEL)
        yield
        kk0 = k * kk_ref[...]
        kk = kk0 * lax.rsqrt(seg_sum(kk0 * kk0) + L2_EPS)
        yield
        k2 = k * (1.0 + (a - 1.0) * ka_ref[...])
        kka = kk * a
        e_inv = jnp.exp(-cum)
        yield
        kk_g = (kk * jnp.exp(cum - logd)).astype(BF16)
        pre["kk_g"] = [kk_g[:, s] for s in psl]
        pre["kk_g_st"] = [stack_h(kk_g[:, s]) for s in psl]
        yield
        k_inv = (k2 * e_inv).astype(BF16)
        pre["k_inv"] = [k_inv[:, s] for s in psl]
        pre["k_inv_st"] = [stack_h(k_inv[:, s]) for s in psl]
        yield
        kka_inv = (kka * e_inv).astype(BF16)
        pre["kka_inv"] = [kka_inv[:, s] for s in psl]
        pre["kka_inv_st"] = [stack_h(kka_inv[:, s]) for s in psl]
        yield
        r = shifted(0)
        r_g = (r * jnp.exp(cum)).astype(BF16)
        pre["r_g"] = [r_g[:, s] for s in psl]
        yield
        v = shifted(2 * D_MODEL)
        pre["v"] = [v[:, s].astype(BF16) for s in psl]
        pre["v_st"] = [stack_h(v[:, s]) for s in psl]
        pre["gam_end"] = jnp.exp(cum[C - 1:C, :])
        yield
        pre["bonus"] = seg_sum(r * k2 * rk_ref[...]) * v
        yield

        gab = gab_ref[rows, :].astype(F32)
        g_log = -jnp.exp(alog_ref[...]) * _softplus(gab + dtb_ref[...])
        beta_all = _sigmoid(gab)
        g_cum = _mm_exact_lhs(tri_incl, g_log)
        g_cum_t = g_cum.T
        yield

        def conv_silu(c0):
            cq = (gconvw_ref[0:1, c0:c0 + LANES] * back(ext_qkv, 3, c0, LANES)
                  + gconvw_ref[1:2, c0:c0 + LANES] * back(ext_qkv, 2, c0, LANES)
                  + gconvw_ref[2:3, c0:c0 + LANES] * back(ext_qkv, 1, c0, LANES)
                  + gconvw_ref[3:4, c0:c0 + LANES] * ext_qkv[e0:e0 + C, c0:c0 + LANES])
            return _silu(cq)

        qn, kn, vb, kb, kbe, qe, k_dec, blk_dec = [], [], [], [], [], [], [], []
        pre.update(qn=qn, kn=kn, vb=vb, kb=kb, kbe=kbe, qe=qe, k_dec=k_dec, blk_dec=blk_dec, decay_w=[])
        for h in HD:
            qh = conv_silu(h * LANES)
            kh = conv_silu(D_MODEL + h * LANES)
            vh = conv_silu(2 * D_MODEL + h * LANES)
            qn_h = qh * lax.rsqrt(jnp.sum(qh * qh, axis=-1, keepdims=True) + L2_EPS) * (GDN_HEAD_DIM ** -0.5)
            kn_h = kh * lax.rsqrt(jnp.sum(kh * kh, axis=-1, keepdims=True) + L2_EPS)
            gc = g_cum[:, h:h + 1]
            beta = beta_all[:, GDN_HEADS + h:GDN_HEADS + h + 1]
            e_g = jnp.exp(gc)
            g_last = g_cum[C - 1:C, h:h + 1]
            kb_h = kn_h * beta
            qn.append(qn_h.astype(BF16))
            kn.append(kn_h.astype(BF16))
            kb.append(kb_h.astype(BF16))
            vb.append((vh * beta).astype(BF16))
            kbe.append((kb_h * e_g).astype(BF16))
            qe.append((qn_h * e_g).astype(BF16))
            k_dec.append((kn_h * jnp.exp(g_last - gc)).astype(BF16))
            blk_dec.append(jnp.exp(g_last))
            if h % 2 == 1:
                gc_w = jnp.where(lane_w < C, g_cum[:, h - 1:h], gc)
                gr_w = jnp.concatenate([g_cum_t[h - 1:h, :], g_cum_t[h:h + 1, :]], axis=1)
                pre["decay_w"].append(jnp.where(incl_w, jnp.exp(jnp.where(incl_w, gc_w - gr_w, 0.0)), 0.0))
            yield
        pre["zg"] = (_silu(z_ref[rows, :].astype(F32))
                     * _sigmoid(gates_ref[rows, 2 * D_MODEL:3 * D_MODEL].astype(F32)))
        yield
        pre["gate_b"] = _sigmoid(gates_ref[rows, D_MODEL:2 * D_MODEL].astype(F32))
        yield

    def independent(pre, mid):
        kn_bd = [blockdiag2(pre["kn"][2 * j], pre["kn"][2 * j + 1]) for j in GP]
        a1 = [jnp.where(strict_w, _mm(pre["kk_g"][p], pre["k_inv_st"][p], _NT), 0.0) for p in PR]
        yield
        a2 = [jnp.where(strict_w, _mm(pre["kk_g"][p], pre["kka_inv_st"][p], _NT), 0.0) for p in PR]
        yield
        a_g = [jnp.where(strict_w, _mm(pair(pre["kb"], j), kn_bd[j], _NT) * pre["decay_w"][j], 0.0) for j in GP]
        yield
        mid["b1"] = [jnp.where(incl_w, _mm(pre["r_g"][p], pre["k_inv_st"][p], _NT), 0.0) for p in PR]
        yield
        mid["b2"] = [jnp.where(incl_w, _mm(pre["r_g"][p], pre["kka_inv_st"][p], _NT), 0.0) for p in PR]
        yield
        mid["attn"] = [_mm(pair(pre["qn"], j), kn_bd[j], _NT) * pre["decay_w"][j] for j in GP]
        yield
        t_all = []
        yield from _wide_unit_lower_inverse(a2 + a_g, C, stack_w, t_all)
        t_rw, t_g = t_all[:RW_PAIRS], t_all[RW_PAIRS:]
        mid["w_rw"] = [_mm(t_rw[p], pre["kk_g_st"][p]) for p in PR]
        yield
        av = [_mm(a1[p], pre["v_st"][p]) for p in PR]
        yield
        mid["u_g"] = [_mm(t_g[j], blockdiag2(pre["vb"][2 * j], pre["vb"][2 * j + 1])) for j in GP]
        yield
        mid["w_g"] = [_mm(t_g[j], blockdiag2(pre["kbe"][2 * j], pre["kbe"][2 * j + 1])) for j in GP]
        yield
        mid["u0"] = [_mm(t_rw[p], stack_h(av[p])) for p in PR]
        yield

    def dependent(pre, mid, res):
        st_rw = [s_wkv[p] for p in PR]
        st_g = [s_ssm[h] for h in HD]
        st_gbd = [blockdiag2(st_g[2 * j], st_g[2 * j + 1]) for j in GP]
        u = [_mm(mid["w_rw"][p], st_rw[p], _NT) + mid["u0"][p] for p in PR]
        yield
        rs = [_mm(pre["r_g"][p], st_rw[p], _NT) for p in PR]
        yield
        v_new = [mid["u_g"][j] - _mm(mid["w_g"][j], st_gbd[j]) for j in GP]
        yield
        qs = [_mm(pair(pre["qe"], j), st_gbd[j]) for j in GP]
        yield
        res["y"] = [rs[p] + _mm(mid["b1"][p], pre["v_st"][p]) - _mm(mid["b2"][p], stack_h(u[p])) for p in PR]
        yield
        res["o"] = [qs[j] + _mm(mid["attn"][j], blockdiag2(v_new[j][:, 0:LANES], v_new[j][:, LANES:2 * LANES]))
                    for j in GP]
        yield
        for p in PR:
            upd = _mm(jnp.concatenate([pre["v"][p], (-u[p]).astype(BF16)], axis=0),
                      jnp.concatenate([pre["k_inv"][p], pre["kka_inv"][p]], axis=0), _TN)
            s_wkv[p] = (st_rw[p] + upd * bd128) * pre["gam_end"][:, psl[p]]
        yield
        for h in HD:
            vn = v_new[h // 2][:, (h % 2) * LANES:(h % 2 + 1) * LANES]
            s_ssm[h] = st_g[h] * pre["blk_dec"][h] + _mm(pre["k_dec"][h], vn, _TN)
        yield

    def post(sub, pre, res):
        y_rw = jnp.concatenate(res["y"], axis=1)
        inv_n = 1.0 / RW_HEAD_DIM
        y_c = y_rw - seg_sum(y_rw) * inv_n
        yield
        y_n = y_c * lax.rsqrt(seg_sum(y_c * y_c) * inv_n + RW_GN_EPS)
        yield
        y_n = y_n * lnw_ref[...] + lnb_ref[...]
        yb = (y_n + pre["bonus"]) * pre["g"]
        yield
        o_all = jnp.concatenate(res["o"], axis=1)
        yc = jnp.concatenate(
            [o_all[:, s] * lax.rsqrt(jnp.mean(o_all[:, s] * o_all[:, s], axis=-1, keepdims=True) + NORM_EPS)
             * gnw_ref[...] for s in psl], axis=1)
        yield
        m = pre["ya"] + pre["gate_b"] * yb + yc * pre["zg"]
        m_ref[sub * C:(sub + 1) * C, :] = m.astype(m_ref.dtype)
        yield

    def weave(*gens):
        gens = [g for g in gens if g is not None]
        while gens:
            for g in list(gens):
                try:
                    next(g)
                except StopIteration:
                    gens.remove(g)

    pres = [dict() for _ in range(n_sub)]
    mids = [dict() for _ in range(n_sub)]
    ress = [dict() for _ in range(n_sub)]

    def chain(*gens):
        for g in gens:
            yield from g

    groups = [list(range(i, min(i + 2, n_sub))) for i in range(0, n_sub, 2)]
    for s in groups[0]:
        weave(prep(s, pres[s]))
    for gi, grp in enumerate(groups):
        fill = []
        if gi + 1 < len(groups):
            fill += [prep(s, pres[s]) for s in groups[gi + 1]]
        if gi:
            fill += [post(s, pres[s], ress[s]) for s in groups[gi - 1]]
        weave(*[independent(pres[s], mids[s]) for s in grp], chain(*fill))
        weave(chain(*[dependent(pres[s], mids[s], ress[s]) for s in grp]))
    weave(chain(*[post(s, pres[s], ress[s]) for s in groups[-1]]))

    ext_conv[0:HALO, :] = ext_conv[R:R + HALO, :]
    ext_qkv[0:HALO, :] = ext_qkv[R:R + HALO, :]
    ext_rkv[0:HALO, :] = ext_rkv[R:R + HALO, :]
    ext_sm[0:HALO, :] = ext_sm[R:R + HALO, :]

    @pl.when(t_idx == n_t - 1)
    def _finish():
        o_conva_ref[0] = ext_conv[HALO - 2:HALO, :]
        o_gconv_ref[0] = ext_qkv[HALO - 3:HALO, :]
        o_shrkv_ref[0] = ext_rkv[HALO - 1:HALO, :]
        o_shsm_ref[0] = ext_sm[HALO - 1:HALO, :]
        o_wkv_ref[0] = s_wkv[...]
        o_ssm_ref[0] = s_ssm[...]


def _mixer(proj_main, proj_gates, states, prm, nb, t_len, chunk):
    n_sub = MIXER_SUB if t_len % (MIXER_SUB * chunk) == 0 else 1
    blk_rows = n_sub * chunk
    n_t = t_len // blk_rows
    rows = nb * t_len

    def pspec(width, col):
        blk = col // width
        return pl.BlockSpec((blk_rows, width), lambda b, t: (b * n_t + t, blk))

    def sspec(shape):
        nd = len(shape)
        return pl.BlockSpec((1,) + tuple(shape[1:]), lambda b, t: (b,) + (0,) * (nd - 1))

    def wspec(arr):
        return pl.BlockSpec(arr.shape, lambda b, t: (0,) * arr.ndim)

    st_names = ("conva", "shrkv", "shsm", "wkv", "gconv", "ssm")
    st_list = [states[n] for n in st_names]
    w_names = ("conv_a_w", "mu_rkv", "mu_sm", "w0", "w2", "a0", "a2", "g2", "k_k", "k_a", "r_k",
               "ln_w", "ln_b", "gdn_conv_w", "a_log", "dt_bias", "gdn_norm_w")
    w_list = [prm[n] for n in w_names]
    in_specs = [pspec(COL_GAB, 0), pspec(LANES, COL_GAB), pspec(3 * D_MODEL, 0)] + [
        sspec(s.shape) for s in st_list] + [wspec(w) for w in w_list]
    out_shapes = [jax.ShapeDtypeStruct((rows, D_MODEL), BF16)] + [
        jax.ShapeDtypeStruct(s.shape, F32) for s in st_list]
    out_specs = [pl.BlockSpec((blk_rows, D_MODEL), lambda b, t: (b * n_t + t, 0))] + [
        sspec(s.shape) for s in st_list]
    scratch = [
        pltpu.VMEM((HALO + blk_rows, D_MODEL), F32),
        pltpu.VMEM((HALO + blk_rows, 3 * D_MODEL), F32),
        pltpu.VMEM((HALO + blk_rows, 3 * D_MODEL), F32),
        pltpu.VMEM((HALO + blk_rows, R_SMALL), F32),
        pltpu.VMEM((RW_PAIRS, LANES, LANES), F32),
        pltpu.VMEM((GDN_HEADS, GDN_HEAD_DIM, GDN_HEAD_DIM), F32),
    ]
    outs = pl.pallas_call(
        functools.partial(_mixer_kernel, chunk, n_sub),
        grid=(nb, n_t),
        in_specs=in_specs,
        out_specs=out_specs,
        out_shape=out_shapes,
        scratch_shapes=scratch,
        compiler_params=pltpu.CompilerParams(
            dimension_semantics=("arbitrary", "arbitrary"), vmem_limit_bytes=VMEM_LIMIT),
        name="mixer",
    )(proj_main, proj_main, proj_gates, *st_list, *w_list)
    return outs[0], dict(zip(st_names, outs[1:]))


def _prep_layer(l, norm_mix_pre, norm_mix_post, norm_mlp_pre, norm_mlp_post, w_in, conv_a_w, rwkv_mu, rwkv_w0,
                rwkv_w2, rwkv_a0, rwkv_a2, rwkv_g2, rwkv_k_k, rwkv_k_a, rwkv_r_k, rwkv_ln_w, rwkv_ln_b,
                gdn_conv_w, gdn_a_log, gdn_dt_bias, gdn_norm_w, w_o, w_ff1, w_ff2):
    d = D_MODEL
    row = lambda a: a.reshape(1, -1).astype(F32)
    pad_heads = lambda a: jnp.pad(a.astype(F32), (0, LANES - GDN_HEADS)).reshape(1, LANES)
    mu = rwkv_mu[l]
    return {
        "g_pre": row(norm_mix_pre[l]), "g_post": row(norm_mix_post[l]),
        "g_pre2": row(norm_mlp_pre[l]), "g_post2": row(norm_mlp_post[l]),
        "w_o": w_o[l].astype(BF16), "w_ff1": w_ff1[l].astype(BF16), "w_ff2": w_ff2[l].astype(BF16),
        "conv_a_w": conv_a_w[l], "mu_rkv": row(mu[0:3 * d]), "mu_sm": row(mu[3 * d:]),
        "w0": row(rwkv_w0[l]), "w2": rwkv_w2[l].astype(BF16), "a0": row(rwkv_a0[l]),
        "a2": rwkv_a2[l].astype(BF16), "g2": rwkv_g2[l].astype(BF16),
        "k_k": row(rwkv_k_k[l]), "k_a": row(rwkv_k_a[l]), "r_k": row(rwkv_r_k[l]),
        "ln_w": row(rwkv_ln_w[l]), "ln_b": row(rwkv_ln_b[l]), "gdn_conv_w": gdn_conv_w[l],
        "a_log": pad_heads(gdn_a_log[l]), "dt_bias": pad_heads(gdn_dt_bias[l]), "gdn_norm_w": row(gdn_norm_w[l]),
    }


def _pack_wkv(s):
    b = s.shape[0]
    s = s.reshape(b, RW_PAIRS, 2, RW_HEAD_DIM, RW_HEAD_DIM)
    zero = jnp.zeros_like(s[:, :, 0])
    top = jnp.concatenate([s[:, :, 0], zero], axis=-1)
    bot = jnp.concatenate([zero, s[:, :, 1]], axis=-1)
    return jnp.concatenate([top, bot], axis=-2)


def _unpack_wkv(s):
    b = s.shape[0]
    h = RW_HEAD_DIM
    return jnp.stack([s[:, :, 0:h, 0:h], s[:, :, h:, h:]], axis=2).reshape(b, RW_HEADS, h, h)


def _layer(x, l, nb, t_len, chunk, st, prm, w_main, w_gates, tm_proj, tm_ffn):
    conv_a, rw_shift, rw_wkv, gdn_conv, gdn_ssm = st
    states = {
        "conva": conv_a, "shrkv": rw_shift[:, None, 0:3 * D_MODEL], "shsm": rw_shift[:, None, 3 * D_MODEL:],
        "wkv": _pack_wkv(rw_wkv), "gconv": gdn_conv, "ssm": gdn_ssm,
    }
    proj_main = _proj(x, prm["g_pre"], w_main, l, P_MAIN, MAIN_TN, tm_proj)
    proj_gates = _proj(x, prm["g_pre"], w_gates, l, 3 * D_MODEL, GATES_TN, tm_proj)
    m, new = _mixer(proj_main, proj_gates, states, prm, nb, t_len, chunk)
    x = _outffn(m, x, prm["w_o"], prm["w_ff1"], prm["w_ff2"], prm["g_post"], prm["g_pre2"], prm["g_post2"], tm_ffn)
    new_st = (new["conva"], jnp.concatenate([new["shrkv"][:, 0], new["shsm"][:, 0]], axis=-1),
              _unpack_wkv(new["wkv"]), new["gconv"], new["ssm"])
    return x, new_st


def _row_tile(rows, want):
    t = min(rows, want)
    while rows % t:
        t //= 2
    return t


def _forward(x_prompt, x_sample, sample_states, weights):
    bp, tp, d = x_prompt.shape
    bs, ts, _ = x_sample.shape
    zero_states = (jnp.zeros((bp, 2, d), F32), jnp.zeros((bp, 3 * d + R_SMALL), F32),
                   jnp.zeros((bp, RW_HEADS, RW_HEAD_DIM, RW_HEAD_DIM), F32),
                   jnp.zeros((bp, 3, 3 * d), F32), jnp.zeros((bp, GDN_HEADS, GDN_HEAD_DIM, GDN_HEAD_DIM), F32))
    y_p = x_prompt.reshape(bp * tp, d)
    y_s = x_sample.reshape(bs * ts, d)
    new_p = [[] for _ in range(5)]
    new_s = [[] for _ in range(5)]
    n_layers = weights[0].shape[0]
    w_in = weights[4]
    w_main = w_in
    w_gates = w_in[:, :, COL_GATES:P_IN]
    for l in range(n_layers):
        prm = _prep_layer(l, *weights)
        y_p, st_p = _layer(y_p, l, bp, tp, min(CHUNK, tp), zero_states, prm, w_main, w_gates,
                           _row_tile(bp * tp, 1024), _row_tile(bp * tp, 256))
        y_s, st_s = _layer(y_s, l, bs, ts, min(CHUNK, ts), tuple(s[l] for s in sample_states), prm, w_main, w_gates,
                           _row_tile(bs * ts, 1024), _row_tile(bs * ts, 256))
        for i in range(5):
            new_p[i].append(st_p[i])
            new_s[i].append(st_s[i])
    outs_p = [jnp.stack(t, axis=0) for t in new_p]
    outs_s = [jnp.stack(t, axis=0) for t in new_s]
    return (y_p.reshape(bp, tp, d), y_s.reshape(bs, ts, d), *outs_p, *outs_s)


def kernel(x_prompt, x_sample, state_conv_a, state_rwkv_shift, state_rwkv_wkv, state_gdn_conv, state_gdn_ssm,
           norm_mix_pre, norm_mix_post, norm_mlp_pre, norm_mlp_post, w_in, conv_a_w, rwkv_mu, rwkv_w0, rwkv_w2,
           rwkv_a0, rwkv_a2, rwkv_g2, rwkv_k_k, rwkv_k_a, rwkv_r_k, rwkv_ln_w, rwkv_ln_b, gdn_conv_w, gdn_a_log,
           gdn_dt_bias, gdn_norm_w, w_o, w_ff1, w_ff2):
    sample_states = (state_conv_a, state_rwkv_shift, state_rwkv_wkv, state_gdn_conv, state_gdn_ssm)
    weights = (norm_mix_pre, norm_mix_post, norm_mlp_pre, norm_mlp_post, w_in, conv_a_w, rwkv_mu, rwkv_w0, rwkv_w2,
               rwkv_a0, rwkv_a2, rwkv_g2, rwkv_k_k, rwkv_k_a, rwkv_r_k, rwkv_ln_w, rwkv_ln_b, gdn_conv_w,
               gdn_a_log, gdn_dt_bias, gdn_norm_w, w_o, w_ff1, w_ff2)
    return _forward(x_prompt, x_sample, sample_states, weights)
```

```python
import functools

import jax
import jax.numpy as jnp
from jax import lax
from jax.experimental import pallas as pl
from jax.experimental.pallas import tpu as pltpu

F32 = jnp.float32
BF16 = jnp.bfloat16

D_MODEL = 1024
DEPTH = 4
CHUNK = 64
NORM_EPS = 1e-6
L2_EPS = 1e-12
RW_HEAD_DIM = 64
RW_HEADS = D_MODEL // RW_HEAD_DIM
RW_PAIRS = RW_HEADS // 2
R_DECAY = D_MODEL // 16
R_AAA = D_MODEL // 16
R_GATE = D_MODEL // 8
R_SMALL = R_DECAY + R_AAA + R_GATE
RW_GN_EPS = 64e-5
GDN_HEAD_DIM = 128
GDN_HEADS = D_MODEL // GDN_HEAD_DIM
D_FF = 4 * D_MODEL
LANES = 128
HALO = 8
MIXER_SUB = 4

COL_CONV = 0
COL_RKV = 3 * D_MODEL
COL_SMALL = 6 * D_MODEL
COL_QKV = COL_SMALL + R_SMALL
COL_Z = COL_QKV + 3 * D_MODEL
COL_GAB = COL_Z + D_MODEL
COL_GATES = COL_GAB + 2 * GDN_HEADS
P_IN = COL_GATES + 3 * D_MODEL
MAIN_TN = 14 * LANES
P_MAIN = ((COL_GAB + LANES + MAIN_TN - 1) // MAIN_TN) * MAIN_TN
GATES_TN = 3 * D_MODEL // 2

VMEM_LIMIT = 56 * 1024 * 1024

_NN = (((1,), (0,)), ((), ()))
_NT = (((1,), (1,)), ((), ()))
_TN = (((0,), (0,)), ((), ()))


def _mm(a, b, dims=_NN):
    return lax.dot_general(a.astype(BF16), b.astype(BF16), dims, preferred_element_type=F32)


def _split3(x):
    hi = x.astype(BF16)
    r1 = x - hi.astype(F32)
    mid = r1.astype(BF16)
    lo = (r1 - mid.astype(F32)).astype(BF16)
    return hi, mid, lo


def _mm_exact_lhs(a_bf16, x):
    out = None
    for q in _split3(x):
        t = lax.dot_general(a_bf16, q, _NN, preferred_element_type=F32)
        out = t if out is None else out + t
    return out


def _sigmoid(x):
    return 1.0 / (1.0 + jnp.exp(-x))


def _softplus(x):
    return jnp.maximum(x, 0.0) + jnp.log(1.0 + jnp.exp(-jnp.abs(x)))


def _silu(x):
    return x * _sigmoid(x)


def _rms(x, gain):
    return x * lax.rsqrt(jnp.mean(x * x, axis=-1, keepdims=True) + NORM_EPS) * gain


def _proj_kernel(x_ref, g_ref, w_ref, o_ref, wb_ref):
    @pl.when(pl.program_id(1) == 0)
    def _cast():
        wb_ref[...] = w_ref[...].astype(BF16)

    h = _rms(x_ref[...], g_ref[...])
    o_ref[...] = lax.dot_general(h.astype(BF16), wb_ref[...], _NT, preferred_element_type=F32).astype(o_ref.dtype)


def _proj(x, gain, wt_all, layer, n_cols, tn, tm):
    rows = x.shape[0]
    return pl.pallas_call(
        _proj_kernel,
        grid=(n_cols // tn, rows // tm),
        in_specs=[
            pl.BlockSpec((tm, D_MODEL), lambda j, i: (i, 0)),
            pl.BlockSpec((1, D_MODEL), lambda j, i: (0, 0)),
            pl.BlockSpec((None, tn, D_MODEL), lambda j, i: (layer, j, 0)),
        ],
        out_specs=pl.BlockSpec((tm, tn), lambda j, i: (i, j)),
        out_shape=jax.ShapeDtypeStruct((rows, n_cols), BF16),
        scratch_shapes=[pltpu.VMEM((tn, D_MODEL), BF16)],
        compiler_params=pltpu.CompilerParams(
            dimension_semantics=("arbitrary", "arbitrary"), vmem_limit_bytes=VMEM_LIMIT),
        name="proj",
    )(x, gain, wt_all)


def _outffn_kernel(m_ref, x_ref, wo_ref, w1_ref, w2_ref, g1_ref, g2_ref, g3_ref, o_ref):
    mo = jnp.dot(m_ref[...], wo_ref[...], preferred_element_type=F32)
    x = x_ref[...] + _rms(mo, g1_ref[...])
    h2 = _rms(x, g2_ref[...])
    f = jnp.dot(h2.astype(BF16), w1_ref[...], preferred_element_type=F32)
    f = jnp.square(jnp.maximum(f, 0.0))
    f = jnp.dot(f.astype(BF16), w2_ref[...], preferred_element_type=F32)
    o_ref[...] = x + _rms(f, g3_ref[...])


def _outffn(m, x, wo, w1, w2, g_post, g_pre2, g_post2, tm):
    rows = x.shape[0]
    const = lambda i: (0, 0)
    return pl.pallas_call(
        _outffn_kernel,
        grid=(rows // tm,),
        in_specs=[
            pl.BlockSpec((tm, D_MODEL), lambda i: (i, 0)),
            pl.BlockSpec((tm, D_MODEL), lambda i: (i, 0)),
            pl.BlockSpec((D_MODEL, D_MODEL), const),
            pl.BlockSpec((D_MODEL, D_FF), const),
            pl.BlockSpec((D_FF, D_MODEL), const),
            pl.BlockSpec((1, D_MODEL), const),
            pl.BlockSpec((1, D_MODEL), const),
            pl.BlockSpec((1, D_MODEL), const),
        ],
        out_specs=pl.BlockSpec((tm, D_MODEL), lambda i: (i, 0)),
        out_shape=jax.ShapeDtypeStruct((rows, D_MODEL), F32),
        compiler_params=pltpu.CompilerParams(
            dimension_semantics=("arbitrary",), vmem_limit_bytes=VMEM_LIMIT),
        name="outffn",
    )(m, x, wo, w1, w2, g_post, g_pre2, g_post2)


INV_BASE = 16


def _wide_unit_lower_inverse(mats, size, stack_w, out):
    base = min(size, INV_BASE)
    row = lax.broadcasted_iota(jnp.int32, (size, 2 * size), 0)
    col = lax.broadcasted_iota(jnp.int32, (size, 2 * size), 1) % size
    eye = (row == col).astype(F32)
    ps = [jnp.where((row // base) == (col // base), a, 0.0) if base < size else a for a in mats]
    ts = [eye - p for p in ps]
    steps = base.bit_length() - 2
    if steps:
        ps = [_mm(p, stack_w(p)) for p in ps]
        yield
    for step in range(steps):
        sps = [stack_w(p) for p in ps]
        ts = [t + _mm(t, sp) for t, sp in zip(ts, sps)]
        if step + 1 < steps:
            ps = [_mm(p, sp) for p, sp in zip(ps, sps)]
        yield
    b = base
    while b < size:
        off = ((row // (2 * b)) == (col // (2 * b))) & ((row // b) != (col // b))
        lt = [_mm(jnp.where(off, a, 0.0), stack_w(t)) for a, t in zip(mats, ts)]
        yield
        ts = [t - _mm(t, stack_w(x)) for t, x in zip(ts, lt)]
        yield
        b *= 2
    out.extend(ts)


def _mixer_kernel(chunk, n_sub,
                  main_ref, gab_ref, gates_ref,
                  st_conva_ref, st_shrkv_ref, st_shsm_ref, st_wkv_ref, st_gconv_ref, st_ssm_ref,
                  convw_ref, mu_rkv_ref, mu_sm_ref, w0_ref, w2_ref, a0_ref, a2_ref, g2_ref,
                  kk_ref, ka_ref, rk_ref, lnw_ref, lnb_ref, gconvw_ref, alog_ref, dtb_ref, gnw_ref,
                  m_ref, o_conva_ref, o_shrkv_ref, o_shsm_ref, o_wkv_ref, o_gconv_ref, o_ssm_ref,
                  ext_conv, ext_qkv, ext_rkv, ext_sm, s_wkv, s_ssm):
    C = chunk
    R = n_sub * C
    W2 = 2 * C
    t_idx = pl.program_id(1)
    n_t = pl.num_programs(1)
    conv_ref = main_ref.at[:, COL_CONV:COL_CONV + 3 * D_MODEL]
    rkv_ref = main_ref.at[:, COL_RKV:COL_RKV + 3 * D_MODEL]
    small_ref = main_ref.at[:, COL_SMALL:COL_SMALL + R_SMALL]
    qkv_ref = main_ref.at[:, COL_QKV:COL_QKV + 3 * D_MODEL]
    z_ref = main_ref.at[:, COL_Z:COL_Z + D_MODEL]

    @pl.when(t_idx == 0)
    def _init():
        ext_conv[0:HALO, :] = jnp.zeros((HALO, D_MODEL), F32)
        ext_conv[HALO - 2:HALO, :] = st_conva_ref[0]
        ext_qkv[0:HALO, :] = jnp.zeros((HALO, 3 * D_MODEL), F32)
        ext_qkv[HALO - 3:HALO, :] = st_gconv_ref[0]
        ext_rkv[0:HALO, :] = jnp.zeros((HALO, 3 * D_MODEL), F32)
        ext_rkv[HALO - 1:HALO, :] = st_shrkv_ref[0]
        ext_sm[0:HALO, :] = jnp.zeros((HALO, R_SMALL), F32)
        ext_sm[HALO - 1:HALO, :] = st_shsm_ref[0]
        s_wkv[...] = st_wkv_ref[0]
        s_ssm[...] = st_ssm_ref[0]

    tri_incl = (lax.broadcasted_iota(jnp.int32, (C, C), 0) >= lax.broadcasted_iota(jnp.int32, (C, C), 1)).astype(BF16)

    li = lax.broadcasted_iota(jnp.int32, (LANES, LANES), 0) // RW_HEAD_DIM
    lj = lax.broadcasted_iota(jnp.int32, (LANES, LANES), 1) // RW_HEAD_DIM
    bd128 = (li == lj).astype(F32)
    bd128_bf = bd128.astype(BF16)

    def seg_sum(x):
        xb = x.astype(BF16)
        return jnp.concatenate(
            [lax.dot_general(xb[:, j * LANES:(j + 1) * LANES], bd128_bf, _NN, preferred_element_type=F32)
             for j in range(D_MODEL // LANES)], axis=1)

    t_w = lax.broadcasted_iota(jnp.int32, (C, W2), 0)
    s_w = lax.broadcasted_iota(jnp.int32, (C, W2), 1) % C
    strict_w = t_w > s_w
    incl_w = t_w >= s_w
    lane_h = lax.broadcasted_iota(jnp.int32, (1, LANES), 1)
    mh0 = (lane_h < RW_HEAD_DIM).astype(BF16)
    mh1 = (lane_h >= RW_HEAD_DIM).astype(BF16)
    lane_w = lax.broadcasted_iota(jnp.int32, (1, W2), 1)
    mw0 = (lane_w < C).astype(BF16)
    mw1 = (lane_w >= C).astype(BF16)

    def stack_h(x):
        xb = x.astype(BF16)
        return jnp.concatenate([xb * mh0, xb * mh1], axis=0)

    def stack_w(x):
        xb = x.astype(BF16)
        return jnp.concatenate([xb * mw0, xb * mw1], axis=0)

    def blockdiag2(x0, x1):
        z = jnp.zeros(x0.shape, BF16)
        return jnp.concatenate([jnp.concatenate([x0.astype(BF16), z], axis=1),
                                jnp.concatenate([z, x1.astype(BF16)], axis=1)], axis=0)

    def pair(xs, j):
        return jnp.concatenate([xs[2 * j], xs[2 * j + 1]], axis=1)

    PR = range(RW_PAIRS)
    HD = range(GDN_HEADS)
    GP = range(GDN_HEADS // 2)
    psl = [slice(p * LANES, (p + 1) * LANES) for p in PR]

    ext_conv[HALO:HALO + R, :] = (conv_ref[:, D_MODEL:2 * D_MODEL].astype(F32)
                                  * conv_ref[:, 2 * D_MODEL:3 * D_MODEL].astype(F32))
    ext_rkv[HALO:HALO + R, :] = rkv_ref[...].astype(F32)
    ext_sm[HALO:HALO + R, :] = small_ref[...].astype(F32)
    ext_qkv[HALO:HALO + R, :] = qkv_ref[...].astype(F32)

    def prep(sub, pre):
        r0 = sub * C
        e0 = HALO + r0
        rows = slice(r0, r0 + C)

        def back(ext, k, c0, width):
            blk = ext[e0 - HALO:e0 + C, c0:c0 + width]
            return pltpu.roll(blk, k, 0)[HALO:HALO + C]

        pre["ya"] = conv_ref[rows, 0:D_MODEL].astype(F32) * (
            convw_ref[0:1, :] * back(ext_conv, 2, 0, D_MODEL)
            + convw_ref[1:2, :] * back(ext_conv, 1, 0, D_MODEL)
            + convw_ref[2:3, :] * ext_conv[e0:e0 + C, :]) * _sigmoid(gates_ref[rows, 0:D_MODEL].astype(F32))
        yield

        p_sm = ext_sm[e0:e0 + C, :]
        ps_sm = p_sm + (back(ext_sm, 1, 0, R_SMALL) - p_sm) * mu_sm_ref[...]
        xw = ps_sm[:, 0:R_DECAY]
        xa = ps_sm[:, R_DECAY:R_DECAY + R_AAA]
        xg = ps_sm[:, R_DECAY + R_AAA:R_SMALL]
        w_log = -_softplus(-(w0_ref[...] + _mm(jnp.tanh(xw), w2_ref[...]))) - 0.5
        logd = -jnp.exp(w_log)
        yield
        cum = _mm_exact_lhs(tri_incl, logd)
        a = _sigmoid(a0_ref[...] + _mm(xa, a2_ref[...]))
        pre["g"] = _mm(_sigmoid(xg), g2_ref[...])
        yield

        def shifted(c0):
            p = ext_rkv[e0:e0 + C, c0:c0 + D_MODEL]
            return p + (back(ext_rkv, 1, c0, D_MODEL) - p) * mu_rkv_ref[:, c0:c0 + D_MODEL]

        k = shifted(D_MODEL)
        yield
        kk0 = k * kk_ref[...]
        kk = kk0 * lax.rsqrt(seg_sum(kk0 * kk0) + L2_EPS)
        yield
        k2 = k * (1.0 + (a - 1.0) * ka_ref[...])
        kka = kk * a
        e_inv = jnp.exp(-cum)
        yield
        kk_g = (kk * jnp.exp(cum - logd)).astype(BF16)
        pre["kk_g"] = [kk_g[:, s] for s in psl]
        pre["kk_g_st"] = [stack_h(kk_g[:, s]) for s in psl]
        yield
        k_inv = (k2 * e_inv).astype(BF16)
        pre["k_inv"] = [k_inv[:, s] for s in psl]
        pre["k_inv_st"] = [stack_h(k_inv[:, s]) for s in psl]
        yield
        kka_inv = (kka * e_inv).astype(BF16)
        pre["kka_inv"] = [kka_inv[:, s] for s in psl]
        pre["kka_inv_st"] = [stack_h(kka_inv[:, s]) for s in psl]
        yield
        r = shifted(0)
        r_g = (r * jnp.exp(cum)).astype(BF16)
        pre["r_g"] = [r_g[:, s] for s in psl]
        yield
        v = shifted(2 * D_MODEL)
        pre["v"] = [v[:, s].astype(BF16) for s in psl]
        pre["v_st"] = [stack_h(v[:, s]) for s in psl]
        pre["gam_end"] = jnp.exp(cum[C - 1:C, :])
        yield
        pre["bonus"] = seg_sum(r * k2 * rk_ref[...]) * v
        yield

        gab = gab_ref[rows, :].astype(F32)
        g_log = -jnp.exp(alog_ref[...]) * _softplus(gab + dtb_ref[...])
        beta_all = _sigmoid(gab)
        g_cum = _mm_exact_lhs(tri_incl, g_log)
        g_cum_t = g_cum.T
        yield

        def conv_silu(c0):
            cq = (gconvw_ref[0:1, c0:c0 + LANES] * back(ext_qkv, 3, c0, LANES)
                  + gconvw_ref[1:2, c0:c0 + LANES] * back(ext_qkv, 2, c0, LANES)
                  + gconvw_ref[2:3, c0:c0 + LANES] * back(ext_qkv, 1, c0, LANES)
                  + gconvw_ref[3:4, c0:c0 + LANES] * ext_qkv[e0:e0 + C, c0:c0 + LANES])
            return _silu(cq)

        qn, kn, vb, kb, kbe, qe, k_dec, blk_dec = [], [], [], [], [], [], [], []
        pre.update(qn=qn, kn=kn, vb=vb, kb=kb, kbe=kbe, qe=qe, k_dec=k_dec, blk_dec=blk_dec, decay_w=[])
        for h in HD:
            qh = conv_silu(h * LANES)
            kh = conv_silu(D_MODEL + h * LANES)
            vh = conv_silu(2 * D_MODEL + h * LANES)
            qn_h = qh * lax.rsqrt(jnp.sum(qh * qh, axis=-1, keepdims=True) + L2_EPS) * (GDN_HEAD_DIM ** -0.5)
            kn_h = kh * lax.rsqrt(jnp.sum(kh * kh, axis=-1, keepdims=True) + L2_EPS)
            gc = g_cum[:, h:h + 1]
            beta = beta_all[:, GDN_HEADS + h:GDN_HEADS + h + 1]
            e_g = jnp.exp(gc)
            g_last = g_cum[C - 1:C, h:h + 1]
            kb_h = kn_h * beta
            qn.append(qn_h.astype(BF16))
            kn.append(kn_h.astype(BF16))
            kb.append(kb_h.astype(BF16))
            vb.append((vh * beta).astype(BF16))
            kbe.append((kb_h * e_g).astype(BF16))
            qe.append((qn_h * e_g).astype(BF16))
            k_dec.append((kn_h * jnp.exp(g_last - gc)).astype(BF16))
            blk_dec.append(jnp.exp(g_last))
            if h % 2 == 1:
                gc_w = jnp.where(lane_w < C, g_cum[:, h - 1:h], gc)
                gr_w = jnp.concatenate([g_cum_t[h - 1:h, :], g_cum_t[h:h + 1, :]], axis=1)
                pre["decay_w"].append(jnp.where(incl_w, jnp.exp(jnp.where(incl_w, gc_w - gr_w, 0.0)), 0.0))
            yield
        pre["zg"] = (_silu(z_ref[rows, :].astype(F32))
                     * _sigmoid(gates_ref[rows, 2 * D_MODEL:3 * D_MODEL].astype(F32)))
        yield
        pre["gate_b"] = _sigmoid(gates_ref[rows, D_MODEL:2 * D_MODEL].astype(F32))
        yield

    def independent(pre, mid):
        kn_bd = [blockdiag2(pre["kn"][2 * j], pre["kn"][2 * j + 1]) for j in GP]
        a1 = [jnp.where(strict_w, _mm(pre["kk_g"][p], pre["k_inv_st"][p], _NT), 0.0) for p in PR]
        yield
        a2 = [jnp.where(strict_w, _mm(pre["kk_g"][p], pre["kka_inv_st"][p], _NT), 0.0) for p in PR]
        yield
        a_g = [jnp.where(strict_w, _mm(pair(pre["kb"], j), kn_bd[j], _NT) * pre["decay_w"][j], 0.0) for j in GP]
        yield
        mid["b1"] = [jnp.where(incl_w, _mm(pre["r_g"][p], pre["k_inv_st"][p], _NT), 0.0) for p in PR]
        yield
        mid["b2"] = [jnp.where(incl_w, _mm(pre["r_g"][p], pre["kka_inv_st"][p], _NT), 0.0) for p in PR]
        yield
        mid["attn"] = [_mm(pair(pre["qn"], j), kn_bd[j], _NT) * pre["decay_w"][j] for j in GP]
        yield
        t_all = []
        yield from _wide_unit_lower_inverse(a2 + a_g, C, stack_w, t_all)
        t_rw, t_g = t_all[:RW_PAIRS], t_all[RW_PAIRS:]
        mid["w_rw"] = [_mm(t_rw[p], pre["kk_g_st"][p]) for p in PR]
        yield
        av = [_mm(a1[p], pre["v_st"][p]) for p in PR]
        yield
        mid["u_g"] = [_mm(t_g[j], blockdiag2(pre["vb"][2 * j], pre["vb"][2 * j + 1])) for j in GP]
        yield
        mid["w_g"] = [_mm(t_g[j], blockdiag2(pre["kbe"][2 * j], pre["kbe"][2 * j + 1])) for j in GP]
        yield
        mid["u0"] = [_mm(t_rw[p], stack_h(av[p])) for p in PR]
        yield

    def dependent(pre, mid, res):
        st_rw = [s_wkv[p] for p in PR]
        st_g = [s_ssm[h] for h in HD]
        st_gbd = [blockdiag2(st_g[2 * j], st_g[2 * j + 1]) for j in GP]
        u = [_mm(mid["w_rw"][p], st_rw[p], _NT) + mid["u0"][p] for p in PR]
        yield
        rs = [_mm(pre["r_g"][p], st_rw[p], _NT) for p in PR]
        yield
        v_new = [mid["u_g"][j] - _mm(mid["w_g"][j], st_gbd[j]) for j in GP]
        yield
        qs = [_mm(pair(pre["qe"], j), st_gbd[j]) for j in GP]
        yield
        res["y"] = [rs[p] + _mm(mid["b1"][p], pre["v_st"][p]) - _mm(mid["b2"][p], stack_h(u[p])) for p in PR]
        yield
        res["o"] = [qs[j] + _mm(mid["attn"][j], blockdiag2(v_new[j][:, 0:LANES], v_new[j][:, LANES:2 * LANES]))
                    for j in GP]
        yield
        for p in PR:
            upd = _mm(jnp.concatenate([pre["v"][p], (-u[p]).astype(BF16)], axis=0),
                      jnp.concatenate([pre["k_inv"][p], pre["kka_inv"][p]], axis=0), _TN)
            s_wkv[p] = (st_rw[p] + upd * bd128) * pre["gam_end"][:, psl[p]]
        yield
        for h in HD:
            vn = v_new[h // 2][:, (h % 2) * LANES:(h % 2 + 1) * LANES]
            s_ssm[h] = st_g[h] * pre["blk_dec"][h] + _mm(pre["k_dec"][h], vn, _TN)
        yield

    def post(sub, pre, res):
        y_rw = jnp.concatenate(res["y"], axis=1)
        inv_n = 1.0 / RW_HEAD_DIM
        y_c = y_rw - seg_sum(y_rw) * inv_n
        yield
        y_n = y_c * lax.rsqrt(seg_sum(y_c * y_c) * inv_n + RW_GN_EPS)
        yield
        y_n = y_n * lnw_ref[...] + lnb_ref[...]
        yb = (y_n + pre["bonus"]) * pre["g"]
        yield
        o_all = jnp.concatenate(res["o"], axis=1)
        yc = jnp.concatenate(
            [o_all[:, s] * lax.rsqrt(jnp.mean(o_all[:, s] * o_all[:, s], axis=-1, keepdims=True) + NORM_EPS)
             * gnw_ref[...] for s in psl], axis=1)
        yield
        m = pre["ya"] + pre["gate_b"] * yb + yc * pre["zg"]
        m_ref[sub * C:(sub + 1) * C, :] = m.astype(m_ref.dtype)
        yield

    def weave(*gens):
        gens = [g for g in gens if g is not None]
        while gens:
            for g in list(gens):
                try:
                    next(g)
                except StopIteration:
                    gens.remove(g)

    pres = [dict() for _ in range(n_sub)]
    mids = [dict() for _ in range(n_sub)]
    ress = [dict() for _ in range(n_sub)]

    def chain(*gens):
        for g in gens:
            yield from g

    groups = [list(range(i, min(i + 2, n_sub))) for i in range(0, n_sub, 2)]
    for s in groups[0]:
        weave(prep(s, pres[s]))
    for gi, grp in enumerate(groups):
        fill = []
        if gi + 1 < len(groups):
            fill += [prep(s, pres[s]) for s in groups[gi + 1]]
        if gi:
            fill += [post(s, pres[s], ress[s]) for s in groups[gi - 1]]
        weave(*[independent(pres[s], mids[s]) for s in grp], chain(*fill))
        weave(chain(*[dependent(pres[s], mids[s], ress[s]) for s in grp]))
    weave(chain(*[post(s, pres[s], ress[s]) for s in groups[-1]]))

    ext_conv[0:HALO, :] = ext_conv[R:R + HALO, :]
    ext_qkv[0:HALO, :] = ext_qkv[R:R + HALO, :]
    ext_rkv[0:HALO, :] = ext_rkv[R:R + HALO, :]
    ext_sm[0:HALO, :] = ext_sm[R:R + HALO, :]

    @pl.when(t_idx == n_t - 1)
    def _finish():
        o_conva_ref[0] = ext_conv[HALO - 2:HALO, :]
        o_gconv_ref[0] = ext_qkv[HALO - 3:HALO, :]
        o_shrkv_ref[0] = ext_rkv[HALO - 1:HALO, :]
        o_shsm_ref[0] = ext_sm[HALO - 1:HALO, :]
        o_wkv_ref[0] = s_wkv[...]
        o_ssm_ref[0] = s_ssm[...]


def _mixer(proj_main, proj_gates, states, prm, nb, t_len, chunk):
    n_sub = MIXER_SUB if t_len % (MIXER_SUB * chunk) == 0 else 1
    blk_rows = n_sub * chunk
    n_t = t_len // blk_rows
    rows = nb * t_len

    def pspec(width, col):
        blk = col // width
        return pl.BlockSpec((blk_rows, width), lambda b, t: (b * n_t + t, blk))

    def sspec(shape):
        nd = len(shape)
        return pl.BlockSpec((1,) + tuple(shape[1:]), lambda b, t: (b,) + (0,) * (nd - 1))

    def wspec(arr):
        return pl.BlockSpec(arr.shape, lambda b, t: (0,) * arr.ndim)

    st_names = ("conva", "shrkv", "shsm", "wkv", "gconv", "ssm")
    st_list = [states[n] for n in st_names]
    w_names = ("conv_a_w", "mu_rkv", "mu_sm", "w0", "w2", "a0", "a2", "g2", "k_k", "k_a", "r_k",
               "ln_w", "ln_b", "gdn_conv_w", "a_log", "dt_bias", "gdn_norm_w")
    w_list = [prm[n] for n in w_names]
    in_specs = [pspec(COL_GAB, 0), pspec(LANES, COL_GAB), pspec(3 * D_MODEL, 0)] + [
        sspec(s.shape) for s in st_list] + [wspec(w) for w in w_list]
    out_shapes = [jax.ShapeDtypeStruct((rows, D_MODEL), BF16)] + [
        jax.ShapeDtypeStruct(s.shape, F32) for s in st_list]
    out_specs = [pl.BlockSpec((blk_rows, D_MODEL), lambda b, t: (b * n_t + t, 0))] + [
        sspec(s.shape) for s in st_list]
    scratch = [
        pltpu.VMEM((HALO + blk_rows, D_MODEL), F32),
        pltpu.VMEM((HALO + blk_rows, 3 * D_MODEL), F32),
        pltpu.VMEM((HALO + blk_rows, 3 * D_MODEL), F32),
        pltpu.VMEM((HALO + blk_rows, R_SMALL), F32),
        pltpu.VMEM((RW_PAIRS, LANES, LANES), F32),
        pltpu.VMEM((GDN_HEADS, GDN_HEAD_DIM, GDN_HEAD_DIM), F32),
    ]
    outs = pl.pallas_call(
        functools.partial(_mixer_kernel, chunk, n_sub),
        grid=(nb, n_t),
        in_specs=in_specs,
        out_specs=out_specs,
        out_shape=out_shapes,
        scratch_shapes=scratch,
        compiler_params=pltpu.CompilerParams(
            dimension_semantics=("arbitrary", "arbitrary"), vmem_limit_bytes=VMEM_LIMIT),
        name="mixer",
    )(proj_main, proj_main, proj_gates, *st_list, *w_list)
    return outs[0], dict(zip(st_names, outs[1:]))


def _prep_layer(l, norm_mix_pre, norm_mix_post, norm_mlp_pre, norm_mlp_post, w_in, conv_a_w, rwkv_mu, rwkv_w0,
                rwkv_w2, rwkv_a0, rwkv_a2, rwkv_g2, rwkv_k_k, rwkv_k_a, rwkv_r_k, rwkv_ln_w, rwkv_ln_b,
                gdn_conv_w, gdn_a_log, gdn_dt_bias, gdn_norm_w, w_o, w_ff1, w_ff2):
    d = D_MODEL
    row = lambda a: a.reshape(1, -1).astype(F32)
    pad_heads = lambda a: jnp.pad(a.astype(F32), (0, LANES - GDN_HEADS)).reshape(1, LANES)
    mu = rwkv_mu[l]
    return {
        "g_pre": row(norm_mix_pre[l]), "g_post": row(norm_mix_post[l]),
        "g_pre2": row(norm_mlp_pre[l]), "g_post2": row(norm_mlp_post[l]),
        "w_o": w_o[l].astype(BF16), "w_ff1": w_ff1[l].astype(BF16), "w_ff2": w_ff2[l].astype(BF16),
        "conv_a_w": conv_a_w[l], "mu_rkv": row(mu[0:3 * d]), "mu_sm": row(mu[3 * d:]),
        "w0": row(rwkv_w0[l]), "w2": rwkv_w2[l].astype(BF16), "a0": row(rwkv_a0[l]),
        "a2": rwkv_a2[l].astype(BF16), "g2": rwkv_g2[l].astype(BF16),
        "k_k": row(rwkv_k_k[l]), "k_a": row(rwkv_k_a[l]), "r_k": row(rwkv_r_k[l]),
        "ln_w": row(rwkv_ln_w[l]), "ln_b": row(rwkv_ln_b[l]), "gdn_conv_w": gdn_conv_w[l],
        "a_log": pad_heads(gdn_a_log[l]), "dt_bias": pad_heads(gdn_dt_bias[l]), "gdn_norm_w": row(gdn_norm_w[l]),
    }


def _pack_wkv(s):
    b = s.shape[0]
    s = s.reshape(b, RW_PAIRS, 2, RW_HEAD_DIM, RW_HEAD_DIM)
    zero = jnp.zeros_like(s[:, :, 0])
    top = jnp.concatenate([s[:, :, 0], zero], axis=-1)
    bot = jnp.concatenate([zero, s[:, :, 1]], axis=-1)
    return jnp.concatenate([top, bot], axis=-2)


def _unpack_wkv(s):
    b = s.shape[0]
    h = RW_HEAD_DIM
    return jnp.stack([s[:, :, 0:h, 0:h], s[:, :, h:, h:]], axis=2).reshape(b, RW_HEADS, h, h)


def _layer(x, l, nb, t_len, chunk, st, prm, w_main, w_gates, tm_proj, tm_ffn):
    conv_a, rw_shift, rw_wkv, gdn_conv, gdn_ssm = st
    states = {
        "conva": conv_a, "shrkv": rw_shift[:, None, 0:3 * D_MODEL], "shsm": rw_shift[:, None, 3 * D_MODEL:],
        "wkv": _pack_wkv(rw_wkv), "gconv": gdn_conv, "ssm": gdn_ssm,
    }
    proj_main = _proj(x, prm["g_pre"], w_main, l, P_MAIN, MAIN_TN, tm_proj)
    proj_gates = _proj(x, prm["g_pre"], w_gates, l, 3 * D_MODEL, GATES_TN, tm_proj)
    m, new = _mixer(proj_main, proj_gates, states, prm, nb, t_len, chunk)
    x = _outffn(m, x, prm["w_o"], prm["w_ff1"], prm["w_ff2"], prm["g_post"], prm["g_pre2"], prm["g_post2"], tm_ffn)
    new_st = (new["conva"], jnp.concatenate([new["shrkv"][:, 0], new["shsm"][:, 0]], axis=-1),
              _unpack_wkv(new["wkv"]), new["gconv"], new["ssm"])
    return x, new_st


def _row_tile(rows, want):
    t = min(rows, want)
    while rows % t:
        t //= 2
    return t


def _forward(x_prompt, x_sample, sample_states, weights):
    bp, tp, d = x_prompt.shape
    bs, ts, _ = x_sample.shape
    zero_states = (jnp.zeros((bp, 2, d), F32), jnp.zeros((bp, 3 * d + R_SMALL), F32),
                   jnp.zeros((bp, RW_HEADS, RW_HEAD_DIM, RW_HEAD_DIM), F32),
                   jnp.zeros((bp, 3, 3 * d), F32), jnp.zeros((bp, GDN_HEADS, GDN_HEAD_DIM, GDN_HEAD_DIM), F32))
    y_p = x_prompt.reshape(bp * tp, d)
    y_s = x_sample.reshape(bs * ts, d)
    new_p = [[] for _ in range(5)]
    new_s = [[] for _ in range(5)]
    n_layers = weights[0].shape[0]
    w_in = weights[4]
    w_main = jnp.swapaxes(w_in, 1, 2)
    w_gates = w_main[:, COL_GATES:P_IN, :]
    for l in range(n_layers):
        prm = _prep_layer(l, *weights)
        y_p, st_p = _layer(y_p, l, bp, tp, min(CHUNK, tp), zero_states, prm, w_main, w_gates,
                           _row_tile(bp * tp, 1024), _row_tile(bp * tp, 256))
        y_s, st_s = _layer(y_s, l, bs, ts, min(CHUNK, ts), tuple(s[l] for s in sample_states), prm, w_main, w_gates,
                           _row_tile(bs * ts, 1024), _row_tile(bs * ts, 256))
        for i in range(5):
            new_p[i].append(st_p[i])
            new_s[i].append(st_s[i])
    outs_p = [jnp.stack(t, axis=0) for t in new_p]
    outs_s = [jnp.stack(t, axis=0) for t in new_s]
    return (y_p.reshape(bp, tp, d), y_s.reshape(bs, ts, d), *outs_p, *outs_s)


def kernel(x_prompt, x_sample, state_conv_a, state_rwkv_shift, state_rwkv_wkv, state_gdn_conv, state_gdn_ssm,
           norm_mix_pre, norm_mix_post, norm_mlp_pre, norm_mlp_post, w_in, conv_a_w, rwkv_mu, rwkv_w0, rwkv_w2,
           rwkv_a0, rwkv_a2, rwkv_g2, rwkv_k_k, rwkv_k_a, rwkv_r_k, rwkv_ln_w, rwkv_ln_b, gdn_conv_w, gdn_a_log,
           gdn_dt_bias, gdn_norm_w, w_o, w_ff1, w_ff2):
    sample_states = (state_conv_a, state_rwkv_shift, state_rwkv_wkv, state_gdn_conv, state_gdn_ssm)
    weights = (norm_mix_pre, norm_mix_post, norm_mlp_pre, norm_mlp_post, w_in, conv_a_w, rwkv_mu, rwkv_w0, rwkv_w2,
               rwkv_a0, rwkv_a2, rwkv_g2, rwkv_k_k, rwkv_k_a, rwkv_r_k, rwkv_ln_w, rwkv_ln_b, gdn_conv_w,
               gdn_a_log, gdn_dt_bias, gdn_norm_w, w_o, w_ff1, w_ff2)
    return _forward(x_prompt, x_sample, sample_states, weights)
```

```python
import functools

import jax
import jax.numpy as jnp
from jax import lax
from jax.experimental import pallas as pl
from jax.experimental.pallas import tpu as pltpu

F32 = jnp.float32
BF16 = jnp.bfloat16

D_MODEL = 1024
DEPTH = 4
CHUNK = 64
NORM_EPS = 1e-6
L2_EPS = 1e-12
RW_HEAD_DIM = 64
RW_HEADS = D_MODEL // RW_HEAD_DIM
RW_PAIRS = RW_HEADS // 2
R_DECAY = D_MODEL // 16
R_AAA = D_MODEL // 16
R_GATE = D_MODEL // 8
R_SMALL = R_DECAY + R_AAA + R_GATE
RW_GN_EPS = 64e-5
GDN_HEAD_DIM = 128
GDN_HEADS = D_MODEL // GDN_HEAD_DIM
D_FF = 4 * D_MODEL
LANES = 128
HALO = 8
MIXER_SUB = 4

COL_CONV = 0
COL_RKV = 3 * D_MODEL
COL_SMALL = 6 * D_MODEL
COL_QKV = COL_SMALL + R_SMALL
COL_Z = COL_QKV + 3 * D_MODEL
COL_GAB = COL_Z + D_MODEL
COL_GATES = COL_GAB + 2 * GDN_HEADS
P_IN = COL_GATES + 3 * D_MODEL
MAIN_TN = 14 * LANES
P_MAIN = ((COL_GAB + LANES + MAIN_TN - 1) // MAIN_TN) * MAIN_TN
GATES_TN = 3 * D_MODEL // 2

VMEM_LIMIT = 56 * 1024 * 1024

_NN = (((1,), (0,)), ((), ()))
_NT = (((1,), (1,)), ((), ()))
_TN = (((0,), (0,)), ((), ()))


def _mm(a, b, dims=_NN):
    return lax.dot_general(a.astype(BF16), b.astype(BF16), dims, preferred_element_type=F32)


def _split3(x):
    hi = x.astype(BF16)
    r1 = x - hi.astype(F32)
    mid = r1.astype(BF16)
    lo = (r1 - mid.astype(F32)).astype(BF16)
    return hi, mid, lo


def _mm_exact_lhs(a_bf16, x):
    out = None
    for q in _split3(x):
        t = lax.dot_general(a_bf16, q, _NN, preferred_element_type=F32)
        out = t if out is None else out + t
    return out


def _sigmoid(x):
    return 1.0 / (1.0 + jnp.exp(-x))


def _softplus(x):
    return jnp.maximum(x, 0.0) + jnp.log(1.0 + jnp.exp(-jnp.abs(x)))


def _silu(x):
    return x * _sigmoid(x)


def _rms(x, gain):
    return x * lax.rsqrt(jnp.mean(x * x, axis=-1, keepdims=True) + NORM_EPS) * gain


def _proj_kernel(x_ref, g_ref, w_ref, o_ref, wb_ref):
    @pl.when(pl.program_id(1) == 0)
    def _cast():
        wb_ref[...] = w_ref[...].astype(BF16)

    h = _rms(x_ref[...], g_ref[...])
    o_ref[...] = lax.dot_general(h.astype(BF16), wb_ref[...], _NT, preferred_element_type=F32).astype(o_ref.dtype)


def _proj(x, gain, wt_all, layer, n_cols, tn, tm):
    rows = x.shape[0]
    return pl.pallas_call(
        _proj_kernel,
        grid=(n_cols // tn, rows // tm),
        in_specs=[
            pl.BlockSpec((tm, D_MODEL), lambda j, i: (i, 0)),
            pl.BlockSpec((1, D_MODEL), lambda j, i: (0, 0)),
            pl.BlockSpec((None, tn, D_MODEL), lambda j, i: (layer, j, 0)),
        ],
        out_specs=pl.BlockSpec((tm, tn), lambda j, i: (i, j)),
        out_shape=jax.ShapeDtypeStruct((rows, n_cols), BF16),
        scratch_shapes=[pltpu.VMEM((tn, D_MODEL), BF16)],
        compiler_params=pltpu.CompilerParams(
            dimension_semantics=("arbitrary", "arbitrary"), vmem_limit_bytes=VMEM_LIMIT),
        name="proj",
    )(x, gain, wt_all)


def _outffn_kernel(m_ref, x_ref, wo_ref, w1_ref, w2_ref, g1_ref, g2_ref, g3_ref, o_ref):
    mo = jnp.dot(m_ref[...], wo_ref[...], preferred_element_type=F32)
    x = x_ref[...] + _rms(mo, g1_ref[...])
    h2 = _rms(x, g2_ref[...])
    f = jnp.dot(h2.astype(BF16), w1_ref[...], preferred_element_type=F32)
    f = jnp.square(jnp.maximum(f, 0.0))
    f = jnp.dot(f.astype(BF16), w2_ref[...], preferred_element_type=F32)
    o_ref[...] = x + _rms(f, g3_ref[...])


def _outffn(m, x, wo, w1, w2, g_post, g_pre2, g_post2, tm):
    rows = x.shape[0]
    const = lambda i: (0, 0)
    return pl.pallas_call(
        _outffn_kernel,
        grid=(rows // tm,),
        in_specs=[
            pl.BlockSpec((tm, D_MODEL), lambda i: (i, 0)),
            pl.BlockSpec((tm, D_MODEL), lambda i: (i, 0)),
            pl.BlockSpec((D_MODEL, D_MODEL), const, pipeline_mode=pl.Buffered(1)),
            pl.BlockSpec((D_MODEL, D_FF), const, pipeline_mode=pl.Buffered(1)),
            pl.BlockSpec((D_FF, D_MODEL), const, pipeline_mode=pl.Buffered(1)),
            pl.BlockSpec((1, D_MODEL), const),
            pl.BlockSpec((1, D_MODEL), const),
            pl.BlockSpec((1, D_MODEL), const),
        ],
        out_specs=pl.BlockSpec((tm, D_MODEL), lambda i: (i, 0)),
        out_shape=jax.ShapeDtypeStruct((rows, D_MODEL), F32),
        compiler_params=pltpu.CompilerParams(
            dimension_semantics=("arbitrary",), vmem_limit_bytes=VMEM_LIMIT),
        name="outffn",
    )(m, x, wo, w1, w2, g_post, g_pre2, g_post2)


INV_BASE = 16


def _wide_unit_lower_inverse(mats, size, stack_w, out):
    base = min(size, INV_BASE)
    row = lax.broadcasted_iota(jnp.int32, (size, 2 * size), 0)
    col = lax.broadcasted_iota(jnp.int32, (size, 2 * size), 1) % size
    eye = (row == col).astype(F32)
    ps = [jnp.where((row // base) == (col // base), a, 0.0) if base < size else a for a in mats]
    ts = [eye - p for p in ps]
    steps = base.bit_length() - 2
    if steps:
        ps = [_mm(p, stack_w(p)) for p in ps]
        yield
    for step in range(steps):
        sps = [stack_w(p) for p in ps]
        ts = [t + _mm(t, sp) for t, sp in zip(ts, sps)]
        if step + 1 < steps:
            ps = [_mm(p, sp) for p, sp in zip(ps, sps)]
        yield
    b = base
    while b < size:
        off = ((row // (2 * b)) == (col // (2 * b))) & ((row // b) != (col // b))
        lt = [_mm(jnp.where(off, a, 0.0), stack_w(t)) for a, t in zip(mats, ts)]
        yield
        ts = [t - _mm(t, stack_w(x)) for t, x in zip(ts, lt)]
        yield
        b *= 2
    out.extend(ts)


def _mixer_kernel(chunk, n_sub,
                  main_ref, gab_ref, gates_ref,
                  st_conva_ref, st_shrkv_ref, st_shsm_ref, st_wkv_ref, st_gconv_ref, st_ssm_ref,
                  convw_ref, mu_rkv_ref, mu_sm_ref, w0_ref, w2_ref, a0_ref, a2_ref, g2_ref,
                  kk_ref, ka_ref, rk_ref, lnw_ref, lnb_ref, gconvw_ref, alog_ref, dtb_ref, gnw_ref,
                  m_ref, o_conva_ref, o_shrkv_ref, o_shsm_ref, o_wkv_ref, o_gconv_ref, o_ssm_ref,
                  ext_conv, ext_qkv, ext_rkv, ext_sm, s_wkv, s_ssm):
    C = chunk
    R = n_sub * C
    W2 = 2 * C
    t_idx = pl.program_id(1)
    n_t = pl.num_programs(1)
    conv_ref = main_ref.at[:, COL_CONV:COL_CONV + 3 * D_MODEL]
    rkv_ref = main_ref.at[:, COL_RKV:COL_RKV + 3 * D_MODEL]
    small_ref = main_ref.at[:, COL_SMALL:COL_SMALL + R_SMALL]
    qkv_ref = main_ref.at[:, COL_QKV:COL_QKV + 3 * D_MODEL]
    z_ref = main_ref.at[:, COL_Z:COL_Z + D_MODEL]

    @pl.when(t_idx == 0)
    def _init():
        ext_conv[0:HALO, :] = jnp.zeros((HALO, D_MODEL), F32)
        ext_conv[HALO - 2:HALO, :] = st_conva_ref[0]
        ext_qkv[0:HALO, :] = jnp.zeros((HALO, 3 * D_MODEL), F32)
        ext_qkv[HALO - 3:HALO, :] = st_gconv_ref[0]
        ext_rkv[0:HALO, :] = jnp.zeros((HALO, 3 * D_MODEL), F32)
        ext_rkv[HALO - 1:HALO, :] = st_shrkv_ref[0]
        ext_sm[0:HALO, :] = jnp.zeros((HALO, R_SMALL), F32)
        ext_sm[HALO - 1:HALO, :] = st_shsm_ref[0]
        s_wkv[...] = st_wkv_ref[0]
        s_ssm[...] = st_ssm_ref[0]

    tri_incl = (lax.broadcasted_iota(jnp.int32, (C, C), 0) >= lax.broadcasted_iota(jnp.int32, (C, C), 1)).astype(BF16)

    li = lax.broadcasted_iota(jnp.int32, (LANES, LANES), 0) // RW_HEAD_DIM
    lj = lax.broadcasted_iota(jnp.int32, (LANES, LANES), 1) // RW_HEAD_DIM
    bd128 = (li == lj).astype(F32)
    bd128_bf = bd128.astype(BF16)

    def seg_sum(x):
        xb = x.astype(BF16)
        return jnp.concatenate(
            [lax.dot_general(xb[:, j * LANES:(j + 1) * LANES], bd128_bf, _NN, preferred_element_type=F32)
             for j in range(D_MODEL // LANES)], axis=1)

    t_w = lax.broadcasted_iota(jnp.int32, (C, W2), 0)
    s_w = lax.broadcasted_iota(jnp.int32, (C, W2), 1) % C
    strict_w = t_w > s_w
    incl_w = t_w >= s_w
    lane_h = lax.broadcasted_iota(jnp.int32, (1, LANES), 1)
    mh0 = (lane_h < RW_HEAD_DIM).astype(BF16)
    mh1 = (lane_h >= RW_HEAD_DIM).astype(BF16)
    lane_w = lax.broadcasted_iota(jnp.int32, (1, W2), 1)
    mw0 = (lane_w < C).astype(BF16)
    mw1 = (lane_w >= C).astype(BF16)

    def stack_h(x):
        xb = x.astype(BF16)
        return jnp.concatenate([xb * mh0, xb * mh1], axis=0)

    def stack_w(x):
        xb = x.astype(BF16)
        return jnp.concatenate([xb * mw0, xb * mw1], axis=0)

    def blockdiag2(x0, x1):
        z = jnp.zeros(x0.shape, BF16)
        return jnp.concatenate([jnp.concatenate([x0.astype(BF16), z], axis=1),
                                jnp.concatenate([z, x1.astype(BF16)], axis=1)], axis=0)

    def pair(xs, j):
        return jnp.concatenate([xs[2 * j], xs[2 * j + 1]], axis=1)

    PR = range(RW_PAIRS)
    HD = range(GDN_HEADS)
    GP = range(GDN_HEADS // 2)
    psl = [slice(p * LANES, (p + 1) * LANES) for p in PR]

    ext_conv[HALO:HALO + R, :] = (conv_ref[:, D_MODEL:2 * D_MODEL].astype(F32)
                                  * conv_ref[:, 2 * D_MODEL:3 * D_MODEL].astype(F32))
    ext_rkv[HALO:HALO + R, :] = rkv_ref[...].astype(F32)
    ext_sm[HALO:HALO + R, :] = small_ref[...].astype(F32)
    ext_qkv[HALO:HALO + R, :] = qkv_ref[...].astype(F32)

    def prep(sub, pre):
        r0 = sub * C
        e0 = HALO + r0
        rows = slice(r0, r0 + C)

        def back(ext, k, c0, width):
            blk = ext[e0 - HALO:e0 + C, c0:c0 + width]
            return pltpu.roll(blk, k, 0)[HALO:HALO + C]

        pre["ya"] = conv_ref[rows, 0:D_MODEL].astype(F32) * (
            convw_ref[0:1, :] * back(ext_conv, 2, 0, D_MODEL)
            + convw_ref[1:2, :] * back(ext_conv, 1, 0, D_MODEL)
            + convw_ref[2:3, :] * ext_conv[e0:e0 + C, :]) * _sigmoid(gates_ref[rows, 0:D_MODEL].astype(F32))
        yield

        p_sm = ext_sm[e0:e0 + C, :]
        ps_sm = p_sm + (back(ext_sm, 1, 0, R_SMALL) - p_sm) * mu_sm_ref[...]
        xw = ps_sm[:, 0:R_DECAY]
        xa = ps_sm[:, R_DECAY:R_DECAY + R_AAA]
        xg = ps_sm[:, R_DECAY + R_AAA:R_SMALL]
        w_log = -_softplus(-(w0_ref[...] + _mm(jnp.tanh(xw), w2_ref[...]))) - 0.5
        logd = -jnp.exp(w_log)
        yield
        cum = _mm_exact_lhs(tri_incl, logd)
        a = _sigmoid(a0_ref[...] + _mm(xa, a2_ref[...]))
        pre["g"] = _mm(_sigmoid(xg), g2_ref[...])
        yield

        def shifted(c0):
            p = ext_rkv[e0:e0 + C, c0:c0 + D_MODEL]
            return p + (back(ext_rkv, 1, c0, D_MODEL) - p) * mu_rkv_ref[:, c0:c0 + D_MODEL]

        k = shifted(D_MODEL)
        yield
        kk0 = k * kk_ref[...]
        kk = kk0 * lax.rsqrt(seg_sum(kk0 * kk0) + L2_EPS)
        yield
        k2 = k * (1.0 + (a - 1.0) * ka_ref[...])
        kka = kk * a
        e_inv = jnp.exp(-cum)
        yield
        kk_g = (kk * jnp.exp(cum - logd)).astype(BF16)
        pre["kk_g"] = [kk_g[:, s] for s in psl]
        pre["kk_g_st"] = [stack_h(kk_g[:, s]) for s in psl]
        yield
        k_inv = (k2 * e_inv).astype(BF16)
        pre["k_inv"] = [k_inv[:, s] for s in psl]
        pre["k_inv_st"] = [stack_h(k_inv[:, s]) for s in psl]
        yield
        kka_inv = (kka * e_inv).astype(BF16)
        pre["kka_inv"] = [kka_inv[:, s] for s in psl]
        pre["kka_inv_st"] = [stack_h(kka_inv[:, s]) for s in psl]
        yield
        r = shifted(0)
        r_g = (r * jnp.exp(cum)).astype(BF16)
        pre["r_g"] = [r_g[:, s] for s in psl]
        yield
        v = shifted(2 * D_MODEL)
        pre["v"] = [v[:, s].astype(BF16) for s in psl]
        pre["v_st"] = [stack_h(v[:, s]) for s in psl]
        pre["gam_end"] = jnp.exp(cum[C - 1:C, :])
        yield
        pre["bonus"] = seg_sum(r * k2 * rk_ref[...]) * v
        yield

        gab = gab_ref[rows, :].astype(F32)
        g_log = -jnp.exp(alog_ref[...]) * _softplus(gab + dtb_ref[...])
        beta_all = _sigmoid(gab)
        g_cum = _mm_exact_lhs(tri_incl, g_log)
        g_cum_t = g_cum.T
        yield

        def conv_silu(c0):
            cq = (gconvw_ref[0:1, c0:c0 + LANES] * back(ext_qkv, 3, c0, LANES)
                  + gconvw_ref[1:2, c0:c0 + LANES] * back(ext_qkv, 2, c0, LANES)
                  + gconvw_ref[2:3, c0:c0 + LANES] * back(ext_qkv, 1, c0, LANES)
                  + gconvw_ref[3:4, c0:c0 + LANES] * ext_qkv[e0:e0 + C, c0:c0 + LANES])
            return _silu(cq)

        qn, kn, vb, kb, kbe, qe, k_dec, blk_dec = [], [], [], [], [], [], [], []
        pre.update(qn=qn, kn=kn, vb=vb, kb=kb, kbe=kbe, qe=qe, k_dec=k_dec, blk_dec=blk_dec, decay_w=[])
        for h in HD:
            qh = conv_silu(h * LANES)
            kh = conv_silu(D_MODEL + h * LANES)
            vh = conv_silu(2 * D_MODEL + h * LANES)
            qn_h = qh * lax.rsqrt(jnp.sum(qh * qh, axis=-1, keepdims=True) + L2_EPS) * (GDN_HEAD_DIM ** -0.5)
            kn_h = kh * lax.rsqrt(jnp.sum(kh * kh, axis=-1, keepdims=True) + L2_EPS)
            gc = g_cum[:, h:h + 1]
            beta = beta_all[:, GDN_HEADS + h:GDN_HEADS + h + 1]
            e_g = jnp.exp(gc)
            g_last = g_cum[C - 1:C, h:h + 1]
            kb_h = kn_h * beta
            qn.append(qn_h.astype(BF16))
            kn.append(kn_h.astype(BF16))
            kb.append(kb_h.astype(BF16))
            vb.append((vh * beta).astype(BF16))
            kbe.append((kb_h * e_g).astype(BF16))
            qe.append((qn_h * e_g).astype(BF16))
            k_dec.append((kn_h * jnp.exp(g_last - gc)).astype(BF16))
            blk_dec.append(jnp.exp(g_last))
            if h % 2 == 1:
                gc_w = jnp.where(lane_w < C, g_cum[:, h - 1:h], gc)
                gr_w = jnp.concatenate([g_cum_t[h - 1:h, :], g_cum_t[h:h + 1, :]], axis=1)
                pre["decay_w"].append(jnp.where(incl_w, jnp.exp(jnp.where(incl_w, gc_w - gr_w, 0.0)), 0.0))
            yield
        pre["zg"] = (_silu(z_ref[rows, :].astype(F32))
                     * _sigmoid(gates_ref[rows, 2 * D_MODEL:3 * D_MODEL].astype(F32)))
        yield
        pre["gate_b"] = _sigmoid(gates_ref[rows, D_MODEL:2 * D_MODEL].astype(F32))
        yield

    def independent(pre, mid):
        kn_bd = [blockdiag2(pre["kn"][2 * j], pre["kn"][2 * j + 1]) for j in GP]
        a1 = [jnp.where(strict_w, _mm(pre["kk_g"][p], pre["k_inv_st"][p], _NT), 0.0) for p in PR]
        yield
        a2 = [jnp.where(strict_w, _mm(pre["kk_g"][p], pre["kka_inv_st"][p], _NT), 0.0) for p in PR]
        yield
        a_g = [jnp.where(strict_w, _mm(pair(pre["kb"], j), kn_bd[j], _NT) * pre["decay_w"][j], 0.0) for j in GP]
        yield
        mid["b1"] = [jnp.where(incl_w, _mm(pre["r_g"][p], pre["k_inv_st"][p], _NT), 0.0) for p in PR]
        yield
        mid["b2"] = [jnp.where(incl_w, _mm(pre["r_g"][p], pre["kka_inv_st"][p], _NT), 0.0) for p in PR]
        yield
        mid["attn"] = [_mm(pair(pre["qn"], j), kn_bd[j], _NT) * pre["decay_w"][j] for j in GP]
        yield
        t_all = []
        yield from _wide_unit_lower_inverse(a2 + a_g, C, stack_w, t_all)
        t_rw, t_g = t_all[:RW_PAIRS], t_all[RW_PAIRS:]
        mid["w_rw"] = [_mm(t_rw[p], pre["kk_g_st"][p]) for p in PR]
        yield
        av = [_mm(a1[p], pre["v_st"][p]) for p in PR]
        yield
        mid["u_g"] = [_mm(t_g[j], blockdiag2(pre["vb"][2 * j], pre["vb"][2 * j + 1])) for j in GP]
        yield
        mid["w_g"] = [_mm(t_g[j], blockdiag2(pre["kbe"][2 * j], pre["kbe"][2 * j + 1])) for j in GP]
        yield
        mid["u0"] = [_mm(t_rw[p], stack_h(av[p])) for p in PR]
        yield

    def dependent(pre, mid, res):
        st_rw = [s_wkv[p] for p in PR]
        st_g = [s_ssm[h] for h in HD]
        st_gbd = [blockdiag2(st_g[2 * j], st_g[2 * j + 1]) for j in GP]
        u = [_mm(mid["w_rw"][p], st_rw[p], _NT) + mid["u0"][p] for p in PR]
        yield
        rs = [_mm(pre["r_g"][p], st_rw[p], _NT) for p in PR]
        yield
        v_new = [mid["u_g"][j] - _mm(mid["w_g"][j], st_gbd[j]) for j in GP]
        yield
        qs = [_mm(pair(pre["qe"], j), st_gbd[j]) for j in GP]
        yield
        res["y"] = [rs[p] + _mm(mid["b1"][p], pre["v_st"][p]) - _mm(mid["b2"][p], stack_h(u[p])) for p in PR]
        yield
        res["o"] = [qs[j] + _mm(mid["attn"][j], blockdiag2(v_new[j][:, 0:LANES], v_new[j][:, LANES:2 * LANES]))
                    for j in GP]
        yield
        for p in PR:
            upd = _mm(jnp.concatenate([pre["v"][p], (-u[p]).astype(BF16)], axis=0),
                      jnp.concatenate([pre["k_inv"][p], pre["kka_inv"][p]], axis=0), _TN)
            s_wkv[p] = (st_rw[p] + upd * bd128) * pre["gam_end"][:, psl[p]]
        yield
        for h in HD:
            vn = v_new[h // 2][:, (h % 2) * LANES:(h % 2 + 1) * LANES]
            s_ssm[h] = st_g[h] * pre["blk_dec"][h] + _mm(pre["k_dec"][h], vn, _TN)
        yield

    def post(sub, pre, res):
        y_rw = jnp.concatenate(res["y"], axis=1)
        inv_n = 1.0 / RW_HEAD_DIM
        y_c = y_rw - seg_sum(y_rw) * inv_n
        yield
        y_n = y_c * lax.rsqrt(seg_sum(y_c * y_c) * inv_n + RW_GN_EPS)
        yield
        y_n = y_n * lnw_ref[...] + lnb_ref[...]
        yb = (y_n + pre["bonus"]) * pre["g"]
        yield
        o_all = jnp.concatenate(res["o"], axis=1)
        yc = jnp.concatenate(
            [o_all[:, s] * lax.rsqrt(jnp.mean(o_all[:, s] * o_all[:, s], axis=-1, keepdims=True) + NORM_EPS)
             * gnw_ref[...] for s in psl], axis=1)
        yield
        m = pre["ya"] + pre["gate_b"] * yb + yc * pre["zg"]
        m_ref[sub * C:(sub + 1) * C, :] = m.astype(m_ref.dtype)
        yield

    def weave(*gens):
        gens = [g for g in gens if g is not None]
        while gens:
            for g in list(gens):
                try:
                    next(g)
                except StopIteration:
                    gens.remove(g)

    pres = [dict() for _ in range(n_sub)]
    mids = [dict() for _ in range(n_sub)]
    ress = [dict() for _ in range(n_sub)]

    def chain(*gens):
        for g in gens:
            yield from g

    groups = [list(range(i, min(i + 2, n_sub))) for i in range(0, n_sub, 2)]
    for s in groups[0]:
        weave(prep(s, pres[s]))
    for gi, grp in enumerate(groups):
        fill = []
        if gi + 1 < len(groups):
            fill += [prep(s, pres[s]) for s in groups[gi + 1]]
        if gi:
            fill += [post(s, pres[s], ress[s]) for s in groups[gi - 1]]
        weave(*[independent(pres[s], mids[s]) for s in grp], chain(*fill))
        weave(chain(*[dependent(pres[s], mids[s], ress[s]) for s in grp]))
    weave(chain(*[post(s, pres[s], ress[s]) for s in groups[-1]]))

    ext_conv[0:HALO, :] = ext_conv[R:R + HALO, :]
    ext_qkv[0:HALO, :] = ext_qkv[R:R + HALO, :]
    ext_rkv[0:HALO, :] = ext_rkv[R:R + HALO, :]
    ext_sm[0:HALO, :] = ext_sm[R:R + HALO, :]

    @pl.when(t_idx == n_t - 1)
    def _finish():
        o_conva_ref[0] = ext_conv[HALO - 2:HALO, :]
        o_gconv_ref[0] = ext_qkv[HALO - 3:HALO, :]
        o_shrkv_ref[0] = ext_rkv[HALO - 1:HALO, :]
        o_shsm_ref[0] = ext_sm[HALO - 1:HALO, :]
        o_wkv_ref[0] = s_wkv[...]
        o_ssm_ref[0] = s_ssm[...]


def _mixer(proj_main, proj_gates, states, prm, nb, t_len, chunk):
    n_sub = MIXER_SUB if t_len % (MIXER_SUB * chunk) == 0 else 1
    blk_rows = n_sub * chunk
    n_t = t_len // blk_rows
    rows = nb * t_len

    def pspec(width, col):
        blk = col // width
        return pl.BlockSpec((blk_rows, width), lambda b, t: (b * n_t + t, blk))

    def sspec(shape):
        nd = len(shape)
        return pl.BlockSpec((1,) + tuple(shape[1:]), lambda b, t: (b,) + (0,) * (nd - 1))

    def wspec(arr):
        return pl.BlockSpec(arr.shape, lambda b, t: (0,) * arr.ndim)

    st_names = ("conva", "shrkv", "shsm", "wkv", "gconv", "ssm")
    st_list = [states[n] for n in st_names]
    w_names = ("conv_a_w", "mu_rkv", "mu_sm", "w0", "w2", "a0", "a2", "g2", "k_k", "k_a", "r_k",
               "ln_w", "ln_b", "gdn_conv_w", "a_log", "dt_bias", "gdn_norm_w")
    w_list = [prm[n] for n in w_names]
    in_specs = [pspec(COL_GAB, 0), pspec(LANES, COL_GAB), pspec(3 * D_MODEL, 0)] + [
        sspec(s.shape) for s in st_list] + [wspec(w) for w in w_list]
    out_shapes = [jax.ShapeDtypeStruct((rows, D_MODEL), BF16)] + [
        jax.ShapeDtypeStruct(s.shape, F32) for s in st_list]
    out_specs = [pl.BlockSpec((blk_rows, D_MODEL), lambda b, t: (b * n_t + t, 0))] + [
        sspec(s.shape) for s in st_list]
    scratch = [
        pltpu.VMEM((HALO + blk_rows, D_MODEL), F32),
        pltpu.VMEM((HALO + blk_rows, 3 * D_MODEL), F32),
        pltpu.VMEM((HALO + blk_rows, 3 * D_MODEL), F32),
        pltpu.VMEM((HALO + blk_rows, R_SMALL), F32),
        pltpu.VMEM((RW_PAIRS, LANES, LANES), F32),
        pltpu.VMEM((GDN_HEADS, GDN_HEAD_DIM, GDN_HEAD_DIM), F32),
    ]
    outs = pl.pallas_call(
        functools.partial(_mixer_kernel, chunk, n_sub),
        grid=(nb, n_t),
        in_specs=in_specs,
        out_specs=out_specs,
        out_shape=out_shapes,
        scratch_shapes=scratch,
        compiler_params=pltpu.CompilerParams(
            dimension_semantics=("arbitrary", "arbitrary"), vmem_limit_bytes=VMEM_LIMIT),
        name="mixer",
    )(proj_main, proj_main, proj_gates, *st_list, *w_list)
    return outs[0], dict(zip(st_names, outs[1:]))


def _prep_layer(l, norm_mix_pre, norm_mix_post, norm_mlp_pre, norm_mlp_post, w_in, conv_a_w, rwkv_mu, rwkv_w0,
                rwkv_w2, rwkv_a0, rwkv_a2, rwkv_g2, rwkv_k_k, rwkv_k_a, rwkv_r_k, rwkv_ln_w, rwkv_ln_b,
                gdn_conv_w, gdn_a_log, gdn_dt_bias, gdn_norm_w, w_o, w_ff1, w_ff2):
    d = D_MODEL
    row = lambda a: a.reshape(1, -1).astype(F32)
    pad_heads = lambda a: jnp.pad(a.astype(F32), (0, LANES - GDN_HEADS)).reshape(1, LANES)
    mu = rwkv_mu[l]
    return {
        "g_pre": row(norm_mix_pre[l]), "g_post": row(norm_mix_post[l]),
        "g_pre2": row(norm_mlp_pre[l]), "g_post2": row(norm_mlp_post[l]),
        "w_o": w_o[l].astype(BF16), "w_ff1": w_ff1[l].astype(BF16), "w_ff2": w_ff2[l].astype(BF16),
        "conv_a_w": conv_a_w[l], "mu_rkv": row(mu[0:3 * d]), "mu_sm": row(mu[3 * d:]),
        "w0": row(rwkv_w0[l]), "w2": rwkv_w2[l].astype(BF16), "a0": row(rwkv_a0[l]),
        "a2": rwkv_a2[l].astype(BF16), "g2": rwkv_g2[l].astype(BF16),
        "k_k": row(rwkv_k_k[l]), "k_a": row(rwkv_k_a[l]), "r_k": row(rwkv_r_k[l]),
        "ln_w": row(rwkv_ln_w[l]), "ln_b": row(rwkv_ln_b[l]), "gdn_conv_w": gdn_conv_w[l],
        "a_log": pad_heads(gdn_a_log[l]), "dt_bias": pad_heads(gdn_dt_bias[l]), "gdn_norm_w": row(gdn_norm_w[l]),
    }


def _pack_wkv(s):
    b = s.shape[0]
    s = s.reshape(b, RW_PAIRS, 2, RW_HEAD_DIM, RW_HEAD_DIM)
    zero = jnp.zeros_like(s[:, :, 0])
    top = jnp.concatenate([s[:, :, 0], zero], axis=-1)
    bot = jnp.concatenate([zero, s[:, :, 1]], axis=-1)
    return jnp.concatenate([top, bot], axis=-2)


def _unpack_wkv(s):
    b = s.shape[0]
    h = RW_HEAD_DIM
    return jnp.stack([s[:, :, 0:h, 0:h], s[:, :, h:, h:]], axis=2).reshape(b, RW_HEADS, h, h)


def _layer(x, l, nb, t_len, chunk, st, prm, w_main, w_gates, tm_proj, tm_ffn):
    conv_a, rw_shift, rw_wkv, gdn_conv, gdn_ssm = st
    states = {
        "conva": conv_a, "shrkv": rw_shift[:, None, 0:3 * D_MODEL], "shsm": rw_shift[:, None, 3 * D_MODEL:],
        "wkv": _pack_wkv(rw_wkv), "gconv": gdn_conv, "ssm": gdn_ssm,
    }
    proj_main = _proj(x, prm["g_pre"], w_main, l, P_MAIN, MAIN_TN, tm_proj)
    proj_gates = _proj(x, prm["g_pre"], w_gates, l, 3 * D_MODEL, GATES_TN, tm_proj)
    m, new = _mixer(proj_main, proj_gates, states, prm, nb, t_len, chunk)
    x = _outffn(m, x, prm["w_o"], prm["w_ff1"], prm["w_ff2"], prm["g_post"], prm["g_pre2"], prm["g_post2"], tm_ffn)
    new_st = (new["conva"], jnp.concatenate([new["shrkv"][:, 0], new["shsm"][:, 0]], axis=-1),
              _unpack_wkv(new["wkv"]), new["gconv"], new["ssm"])
    return x, new_st


def _row_tile(rows, want):
    t = min(rows, want)
    while rows % t:
        t //= 2
    return t


def _forward(x_prompt, x_sample, sample_states, weights):
    bp, tp, d = x_prompt.shape
    bs, ts, _ = x_sample.shape
    zero_states = (jnp.zeros((bp, 2, d), F32), jnp.zeros((bp, 3 * d + R_SMALL), F32),
                   jnp.zeros((bp, RW_HEADS, RW_HEAD_DIM, RW_HEAD_DIM), F32),
                   jnp.zeros((bp, 3, 3 * d), F32), jnp.zeros((bp, GDN_HEADS, GDN_HEAD_DIM, GDN_HEAD_DIM), F32))
    y_p = x_prompt.reshape(bp * tp, d)
    y_s = x_sample.reshape(bs * ts, d)
    new_p = [[] for _ in range(5)]
    new_s = [[] for _ in range(5)]
    n_layers = weights[0].shape[0]
    w_in = weights[4]
    w_main = jnp.swapaxes(w_in, 1, 2)
    w_gates = w_main[:, COL_GATES:P_IN, :]
    for l in range(n_layers):
        prm = _prep_layer(l, *weights)
        y_p, st_p = _layer(y_p, l, bp, tp, min(CHUNK, tp), zero_states, prm, w_main, w_gates,
                           _row_tile(bp * tp, 1024), _row_tile(bp * tp, 512))
        y_s, st_s = _layer(y_s, l, bs, ts, min(CHUNK, ts), tuple(s[l] for s in sample_states), prm, w_main, w_gates,
                           _row_tile(bs * ts, 1024), _row_tile(bs * ts, 256))
        for i in range(5):
            new_p[i].append(st_p[i])
            new_s[i].append(st_s[i])
    outs_p = [jnp.stack(t, axis=0) for t in new_p]
    outs_s = [jnp.stack(t, axis=0) for t in new_s]
    return (y_p.reshape(bp, tp, d), y_s.reshape(bs, ts, d), *outs_p, *outs_s)


def kernel(x_prompt, x_sample, state_conv_a, state_rwkv_shift, state_rwkv_wkv, state_gdn_conv, state_gdn_ssm,
           norm_mix_pre, norm_mix_post, norm_mlp_pre, norm_mlp_post, w_in, conv_a_w, rwkv_mu, rwkv_w0, rwkv_w2,
           rwkv_a0, rwkv_a2, rwkv_g2, rwkv_k_k, rwkv_k_a, rwkv_r_k, rwkv_ln_w, rwkv_ln_b, gdn_conv_w, gdn_a_log,
           gdn_dt_bias, gdn_norm_w, w_o, w_ff1, w_ff2):
    sample_states = (state_conv_a, state_rwkv_shift, state_rwkv_wkv, state_gdn_conv, state_gdn_ssm)
    weights = (norm_mix_pre, norm_mix_post, norm_mlp_pre, norm_mlp_post, w_in, conv_a_w, rwkv_mu, rwkv_w0, rwkv_w2,
               rwkv_a0, rwkv_a2, rwkv_g2, rwkv_k_k, rwkv_k_a, rwkv_r_k, rwkv_ln_w, rwkv_ln_b, gdn_conv_w,
               gdn_a_log, gdn_dt_bias, gdn_norm_w, w_o, w_ff1, w_ff2)
    return _forward(x_prompt, x_sample, sample_states, weights)
```

```python
import functools

import jax
import jax.numpy as jnp
from jax import lax
from jax.experimental import pallas as pl
from jax.experimental.pallas import tpu as pltpu

F32 = jnp.float32
BF16 = jnp.bfloat16

D_MODEL = 1024
DEPTH = 4
CHUNK = 64
NORM_EPS = 1e-6
L2_EPS = 1e-12
RW_HEAD_DIM = 64
RW_HEADS = D_MODEL // RW_HEAD_DIM
RW_PAIRS = RW_HEADS // 2
R_DECAY = D_MODEL // 16
R_AAA = D_MODEL // 16
R_GATE = D_MODEL // 8
R_SMALL = R_DECAY + R_AAA + R_GATE
RW_GN_EPS = 64e-5
GDN_HEAD_DIM = 128
GDN_HEADS = D_MODEL // GDN_HEAD_DIM
D_FF = 4 * D_MODEL
LANES = 128
HALO = 8
BF16_ROWS = 16
MIXER_SUB = 8

COL_CONV = 0
COL_RKV = 3 * D_MODEL
COL_SMALL = 6 * D_MODEL
COL_QKV = COL_SMALL + R_SMALL
COL_Z = COL_QKV + 3 * D_MODEL
COL_GAB = COL_Z + D_MODEL
COL_GATES = COL_GAB + 2 * GDN_HEADS
P_IN = COL_GATES + 3 * D_MODEL
MAIN_TN = 14 * LANES
P_MAIN = ((COL_GAB + LANES + MAIN_TN - 1) // MAIN_TN) * MAIN_TN
GATES_TN = 3 * D_MODEL // 2

VMEM_LIMIT = 56 * 1024 * 1024

_NN = (((1,), (0,)), ((), ()))
_NT = (((1,), (1,)), ((), ()))
_TN = (((0,), (0,)), ((), ()))


def _mm(a, b, dims=_NN):
    return lax.dot_general(a.astype(BF16), b.astype(BF16), dims, preferred_element_type=F32)


def _split3(x):
    hi = x.astype(BF16)
    r1 = x - hi.astype(F32)
    mid = r1.astype(BF16)
    lo = (r1 - mid.astype(F32)).astype(BF16)
    return hi, mid, lo


def _mm_exact_lhs(a_bf16, x):
    out = None
    for q in _split3(x):
        t = lax.dot_general(a_bf16, q, _NN, preferred_element_type=F32)
        out = t if out is None else out + t
    return out


def _sigmoid(x):
    return 1.0 / (1.0 + jnp.exp(-x))


def _softplus(x):
    return jnp.maximum(x, 0.0) + jnp.log(1.0 + jnp.exp(-jnp.abs(x)))


def _silu(x):
    return x * _sigmoid(x)


def _rms(x, gain):
    return x * lax.rsqrt(jnp.mean(x * x, axis=-1, keepdims=True) + NORM_EPS) * gain


def _proj_kernel(x_ref, g_ref, w_ref, o_ref, wb_ref):
    @pl.when(pl.program_id(1) == 0)
    def _cast():
        wb_ref[...] = w_ref[...].astype(BF16)

    h = _rms(x_ref[...], g_ref[...])
    o_ref[...] = lax.dot_general(h.astype(BF16), wb_ref[...], _NT, preferred_element_type=F32).astype(o_ref.dtype)


def _proj(x, gain, wt_all, layer, n_cols, tn, tm):
    rows = x.shape[0]
    return pl.pallas_call(
        _proj_kernel,
        grid=(n_cols // tn, rows // tm),
        in_specs=[
            pl.BlockSpec((tm, D_MODEL), lambda j, i: (i, 0)),
            pl.BlockSpec((1, D_MODEL), lambda j, i: (0, 0)),
            pl.BlockSpec((None, tn, D_MODEL), lambda j, i: (layer, j, 0)),
        ],
        out_specs=pl.BlockSpec((tm, tn), lambda j, i: (i, j)),
        out_shape=jax.ShapeDtypeStruct((rows, n_cols), BF16),
        scratch_shapes=[pltpu.VMEM((tn, D_MODEL), BF16)],
        compiler_params=pltpu.CompilerParams(
            dimension_semantics=("arbitrary", "arbitrary"), vmem_limit_bytes=VMEM_LIMIT),
        name="proj",
    )(x, gain, wt_all)


def _outffn_kernel(m_ref, x_ref, wo_ref, w1_ref, w2_ref, g1_ref, g2_ref, g3_ref, o_ref):
    mo = jnp.dot(m_ref[...], wo_ref[...], preferred_element_type=F32)
    x = x_ref[...] + _rms(mo, g1_ref[...])
    h2 = _rms(x, g2_ref[...])
    f = jnp.dot(h2.astype(BF16), w1_ref[...], preferred_element_type=F32)
    f = jnp.square(jnp.maximum(f, 0.0))
    f = jnp.dot(f.astype(BF16), w2_ref[...], preferred_element_type=F32)
    o_ref[...] = x + _rms(f, g3_ref[...])


def _outffn(m, x, wo, w1, w2, g_post, g_pre2, g_post2, tm):
    rows = x.shape[0]
    const = lambda i: (0, 0)
    return pl.pallas_call(
        _outffn_kernel,
        grid=(rows // tm,),
        in_specs=[
            pl.BlockSpec((tm, D_MODEL), lambda i: (i, 0)),
            pl.BlockSpec((tm, D_MODEL), lambda i: (i, 0)),
            pl.BlockSpec((D_MODEL, D_MODEL), const, pipeline_mode=pl.Buffered(1)),
            pl.BlockSpec((D_MODEL, D_FF), const, pipeline_mode=pl.Buffered(1)),
            pl.BlockSpec((D_FF, D_MODEL), const, pipeline_mode=pl.Buffered(1)),
            pl.BlockSpec((1, D_MODEL), const),
            pl.BlockSpec((1, D_MODEL), const),
            pl.BlockSpec((1, D_MODEL), const),
        ],
        out_specs=pl.BlockSpec((tm, D_MODEL), lambda i: (i, 0)),
        out_shape=jax.ShapeDtypeStruct((rows, D_MODEL), F32),
        compiler_params=pltpu.CompilerParams(
            dimension_semantics=("arbitrary",), vmem_limit_bytes=VMEM_LIMIT),
        name="outffn",
    )(m, x, wo, w1, w2, g_post, g_pre2, g_post2)


INV_BASE = 16


def _wide_unit_lower_inverse(mats, size, stack_w, out):
    base = min(size, INV_BASE)
    row = lax.broadcasted_iota(jnp.int32, (size, 2 * size), 0)
    col = lax.broadcasted_iota(jnp.int32, (size, 2 * size), 1) % size
    eye = (row == col).astype(F32)
    ps = [jnp.where((row // base) == (col // base), a, 0.0) if base < size else a for a in mats]
    ts = [eye - p for p in ps]
    steps = base.bit_length() - 2
    if steps:
        ps = [_mm(p, stack_w(p)) for p in ps]
        yield
    for step in range(steps):
        sps = [stack_w(p) for p in ps]
        ts = [t + _mm(t, sp) for t, sp in zip(ts, sps)]
        if step + 1 < steps:
            ps = [_mm(p, sp) for p, sp in zip(ps, sps)]
        yield
    b = base
    while b < size:
        off = ((row // (2 * b)) == (col // (2 * b))) & ((row // b) != (col // b))
        lt = [_mm(jnp.where(off, a, 0.0), stack_w(t)) for a, t in zip(mats, ts)]
        yield
        ts = [t - _mm(t, stack_w(x)) for t, x in zip(ts, lt)]
        yield
        b *= 2
    out.extend(ts)


def _mixer_kernel(chunk, n_sub,
                  main_ref, gab_ref, gates_ref,
                  st_conva_ref, st_shrkv_ref, st_shsm_ref, st_wkv_ref, st_gconv_ref, st_ssm_ref,
                  convw_ref, mu_rkv_ref, mu_sm_ref, w0_ref, w2_ref, a0_ref, a2_ref, g2_ref,
                  kk_ref, ka_ref, rk_ref, lnw_ref, lnb_ref, gconvw_ref, alog_ref, dtb_ref, gnw_ref,
                  m_ref, o_conva_ref, o_shrkv_ref, o_shsm_ref, o_wkv_ref, o_gconv_ref, o_ssm_ref,
                  ext_conv, ext_qkv, ext_rkv, ext_sm, s_wkv, s_ssm):
    C = chunk
    R = n_sub * C
    W2 = 2 * C
    t_idx = pl.program_id(1)
    n_t = pl.num_programs(1)
    conv_ref = main_ref.at[:, COL_CONV:COL_CONV + 3 * D_MODEL]
    rkv_ref = main_ref.at[:, COL_RKV:COL_RKV + 3 * D_MODEL]
    small_ref = main_ref.at[:, COL_SMALL:COL_SMALL + R_SMALL]
    qkv_ref = main_ref.at[:, COL_QKV:COL_QKV + 3 * D_MODEL]
    z_ref = main_ref.at[:, COL_Z:COL_Z + D_MODEL]

    @pl.when(t_idx == 0)
    def _init():
        ext_conv[0:HALO, :] = jnp.zeros((HALO, D_MODEL), F32)
        ext_conv[HALO - 2:HALO, :] = st_conva_ref[0]
        ext_qkv[0:HALO, :] = jnp.zeros((HALO, 3 * D_MODEL), F32)
        ext_qkv[HALO - 3:HALO, :] = st_gconv_ref[0]
        ext_rkv[0:HALO, :] = jnp.zeros((HALO, 3 * D_MODEL), F32)
        ext_rkv[HALO - 1:HALO, :] = st_shrkv_ref[0]
        ext_sm[0:HALO, :] = jnp.zeros((HALO, R_SMALL), F32)
        ext_sm[HALO - 1:HALO, :] = st_shsm_ref[0]
        s_wkv[...] = st_wkv_ref[0]
        s_ssm[...] = st_ssm_ref[0]

    tri_incl = (lax.broadcasted_iota(jnp.int32, (C, C), 0) >= lax.broadcasted_iota(jnp.int32, (C, C), 1)).astype(BF16)

    li = lax.broadcasted_iota(jnp.int32, (LANES, LANES), 0) // RW_HEAD_DIM
    lj = lax.broadcasted_iota(jnp.int32, (LANES, LANES), 1) // RW_HEAD_DIM
    bd128 = (li == lj).astype(F32)
    bd128_bf = bd128.astype(BF16)

    def seg_sum(x):
        xb = x.astype(BF16)
        return jnp.concatenate(
            [lax.dot_general(xb[:, j * LANES:(j + 1) * LANES], bd128_bf, _NN, preferred_element_type=F32)
             for j in range(D_MODEL // LANES)], axis=1)

    t_w = lax.broadcasted_iota(jnp.int32, (C, W2), 0)
    s_w = lax.broadcasted_iota(jnp.int32, (C, W2), 1) % C
    strict_w = t_w > s_w
    incl_w = t_w >= s_w
    lane_h = lax.broadcasted_iota(jnp.int32, (1, LANES), 1)
    mh0 = (lane_h < RW_HEAD_DIM).astype(BF16)
    mh1 = (lane_h >= RW_HEAD_DIM).astype(BF16)
    lane_w = lax.broadcasted_iota(jnp.int32, (1, W2), 1)
    mw0 = (lane_w < C).astype(BF16)
    mw1 = (lane_w >= C).astype(BF16)

    def stack_h(x):
        xb = x.astype(BF16)
        return jnp.concatenate([xb * mh0, xb * mh1], axis=0)

    def stack_w(x):
        xb = x.astype(BF16)
        return jnp.concatenate([xb * mw0, xb * mw1], axis=0)

    def blockdiag2(x0, x1):
        z = jnp.zeros(x0.shape, BF16)
        return jnp.concatenate([jnp.concatenate([x0.astype(BF16), z], axis=1),
                                jnp.concatenate([z, x1.astype(BF16)], axis=1)], axis=0)

    def pair(xs, j):
        return jnp.concatenate([xs[2 * j], xs[2 * j + 1]], axis=1)

    PR = range(RW_PAIRS)
    HD = range(GDN_HEADS)
    GP = range(GDN_HEADS // 2)
    psl = [slice(p * LANES, (p + 1) * LANES) for p in PR]

    def load_conv(lo, hi, c0, width):
        return (conv_ref[lo:hi, D_MODEL + c0:D_MODEL + c0 + width].astype(F32)
                * conv_ref[lo:hi, 2 * D_MODEL + c0:2 * D_MODEL + c0 + width].astype(F32))

    loaders = {
        "conv": load_conv,
        "rkv": lambda lo, hi, c0, width: rkv_ref[lo:hi, c0:c0 + width].astype(F32),
        "sm": lambda lo, hi, c0, width: small_ref[lo:hi, c0:c0 + width].astype(F32),
        "qkv": lambda lo, hi, c0, width: qkv_ref[lo:hi, c0:c0 + width].astype(F32),
    }
    exts = {"conv": ext_conv, "rkv": ext_rkv, "sm": ext_sm, "qkv": ext_qkv}
    for name, ext in exts.items():
        ext[HALO:HALO + C, :] = loaders[name](0, C, 0, ext.shape[1])

    def prep(sub, pre):
        r0 = sub * C
        rows = slice(r0, r0 + C)

        def back(name, k, c0, width):
            if sub == 0:
                return pltpu.roll(exts[name][0:HALO + C, c0:c0 + width], k, 0)[HALO:HALO + C]
            blk = loaders[name](r0 - BF16_ROWS, r0 + C, c0, width)
            return pltpu.roll(blk, k, 0)[BF16_ROWS:BF16_ROWS + C]

        def cur(name, c0, width):
            if sub == 0:
                return exts[name][HALO:HALO + C, c0:c0 + width]
            return loaders[name](r0, r0 + C, c0, width)

        pre["ya"] = conv_ref[rows, 0:D_MODEL].astype(F32) * (
            convw_ref[0:1, :] * back("conv", 2, 0, D_MODEL)
            + convw_ref[1:2, :] * back("conv", 1, 0, D_MODEL)
            + convw_ref[2:3, :] * cur("conv", 0, D_MODEL)) * _sigmoid(gates_ref[rows, 0:D_MODEL].astype(F32))
        yield

        p_sm = cur("sm", 0, R_SMALL)
        ps_sm = p_sm + (back("sm", 1, 0, R_SMALL) - p_sm) * mu_sm_ref[...]
        xw = ps_sm[:, 0:R_DECAY]
        xa = ps_sm[:, R_DECAY:R_DECAY + R_AAA]
        xg = ps_sm[:, R_DECAY + R_AAA:R_SMALL]
        w_log = -_softplus(-(w0_ref[...] + _mm(jnp.tanh(xw), w2_ref[...]))) - 0.5
        logd = -jnp.exp(w_log)
        yield
        cum = _mm_exact_lhs(tri_incl, logd)
        a = _sigmoid(a0_ref[...] + _mm(xa, a2_ref[...]))
        pre["g"] = _mm(_sigmoid(xg), g2_ref[...])
        yield

        def shifted(c0):
            p = cur("rkv", c0, D_MODEL)
            return p + (back("rkv", 1, c0, D_MODEL) - p) * mu_rkv_ref[:, c0:c0 + D_MODEL]

        k = shifted(D_MODEL)
        yield
        kk0 = k * kk_ref[...]
        kk = kk0 * lax.rsqrt(seg_sum(kk0 * kk0) + L2_EPS)
        yield
        k2 = k * (1.0 + (a - 1.0) * ka_ref[...])
        kka = kk * a
        e_inv = jnp.exp(-cum)
        yield
        kk_g = (kk * jnp.exp(cum - logd)).astype(BF16)
        pre["kk_g"] = [kk_g[:, s] for s in psl]
        pre["kk_g_st"] = [stack_h(kk_g[:, s]) for s in psl]
        yield
        k_inv = (k2 * e_inv).astype(BF16)
        pre["k_inv"] = [k_inv[:, s] for s in psl]
        pre["k_inv_st"] = [stack_h(k_inv[:, s]) for s in psl]
        yield
        kka_inv = (kka * e_inv).astype(BF16)
        pre["kka_inv"] = [kka_inv[:, s] for s in psl]
        pre["kka_inv_st"] = [stack_h(kka_inv[:, s]) for s in psl]
        yield
        r = shifted(0)
        r_g = (r * jnp.exp(cum)).astype(BF16)
        pre["r_g"] = [r_g[:, s] for s in psl]
        yield
        v = shifted(2 * D_MODEL)
        pre["v"] = [v[:, s].astype(BF16) for s in psl]
        pre["v_st"] = [stack_h(v[:, s]) for s in psl]
        pre["gam_end"] = jnp.exp(cum[C - 1:C, :])
        yield
        pre["bonus"] = seg_sum(r * k2 * rk_ref[...]) * v
        yield

        gab = gab_ref[rows, :].astype(F32)
        g_log = -jnp.exp(alog_ref[...]) * _softplus(gab + dtb_ref[...])
        beta_all = _sigmoid(gab)
        g_cum = _mm_exact_lhs(tri_incl, g_log)
        g_cum_t = g_cum.T
        yield

        def conv_silu(c0):
            cq = (gconvw_ref[0:1, c0:c0 + LANES] * back("qkv", 3, c0, LANES)
                  + gconvw_ref[1:2, c0:c0 + LANES] * back("qkv", 2, c0, LANES)
                  + gconvw_ref[2:3, c0:c0 + LANES] * back("qkv", 1, c0, LANES)
                  + gconvw_ref[3:4, c0:c0 + LANES] * cur("qkv", c0, LANES))
            return _silu(cq)

        qn, kn, vb, kb, kbe, qe, k_dec, blk_dec = [], [], [], [], [], [], [], []
        pre.update(qn=qn, kn=kn, vb=vb, kb=kb, kbe=kbe, qe=qe, k_dec=k_dec, blk_dec=blk_dec, decay_w=[])
        for h in HD:
            qh = conv_silu(h * LANES)
            kh = conv_silu(D_MODEL + h * LANES)
            vh = conv_silu(2 * D_MODEL + h * LANES)
            qn_h = qh * lax.rsqrt(jnp.sum(qh * qh, axis=-1, keepdims=True) + L2_EPS) * (GDN_HEAD_DIM ** -0.5)
            kn_h = kh * lax.rsqrt(jnp.sum(kh * kh, axis=-1, keepdims=True) + L2_EPS)
            gc = g_cum[:, h:h + 1]
            beta = beta_all[:, GDN_HEADS + h:GDN_HEADS + h + 1]
            e_g = jnp.exp(gc)
            g_last = g_cum[C - 1:C, h:h + 1]
            kb_h = kn_h * beta
            qn.append(qn_h.astype(BF16))
            kn.append(kn_h.astype(BF16))
            kb.append(kb_h.astype(BF16))
            vb.append((vh * beta).astype(BF16))
            kbe.append((kb_h * e_g).astype(BF16))
            qe.append((qn_h * e_g).astype(BF16))
            k_dec.append((kn_h * jnp.exp(g_last - gc)).astype(BF16))
            blk_dec.append(jnp.exp(g_last))
            if h % 2 == 1:
                gc_w = jnp.where(lane_w < C, g_cum[:, h - 1:h], gc)
                gr_w = jnp.concatenate([g_cum_t[h - 1:h, :], g_cum_t[h:h + 1, :]], axis=1)
                pre["decay_w"].append(jnp.where(incl_w, jnp.exp(jnp.where(incl_w, gc_w - gr_w, 0.0)), 0.0))
            yield
        pre["zg"] = (_silu(z_ref[rows, :].astype(F32))
                     * _sigmoid(gates_ref[rows, 2 * D_MODEL:3 * D_MODEL].astype(F32)))
        yield
        pre["gate_b"] = _sigmoid(gates_ref[rows, D_MODEL:2 * D_MODEL].astype(F32))
        yield

    def independent(pre, mid):
        kn_bd = [blockdiag2(pre["kn"][2 * j], pre["kn"][2 * j + 1]) for j in GP]
        a1 = [jnp.where(strict_w, _mm(pre["kk_g"][p], pre["k_inv_st"][p], _NT), 0.0) for p in PR]
        yield
        a2 = [jnp.where(strict_w, _mm(pre["kk_g"][p], pre["kka_inv_st"][p], _NT), 0.0) for p in PR]
        yield
        a_g = [jnp.where(strict_w, _mm(pair(pre["kb"], j), kn_bd[j], _NT) * pre["decay_w"][j], 0.0) for j in GP]
        yield
        mid["b1"] = [jnp.where(incl_w, _mm(pre["r_g"][p], pre["k_inv_st"][p], _NT), 0.0) for p in PR]
        yield
        mid["b2"] = [jnp.where(incl_w, _mm(pre["r_g"][p], pre["kka_inv_st"][p], _NT), 0.0) for p in PR]
        yield
        mid["attn"] = [_mm(pair(pre["qn"], j), kn_bd[j], _NT) * pre["decay_w"][j] for j in GP]
        yield
        t_all = []
        yield from _wide_unit_lower_inverse(a2 + a_g, C, stack_w, t_all)
        t_rw, t_g = t_all[:RW_PAIRS], t_all[RW_PAIRS:]
        mid["w_rw"] = [_mm(t_rw[p], pre["kk_g_st"][p]) for p in PR]
        yield
        av = [_mm(a1[p], pre["v_st"][p]) for p in PR]
        yield
        mid["u_g"] = [_mm(t_g[j], blockdiag2(pre["vb"][2 * j], pre["vb"][2 * j + 1])) for j in GP]
        yield
        mid["w_g"] = [_mm(t_g[j], blockdiag2(pre["kbe"][2 * j], pre["kbe"][2 * j + 1])) for j in GP]
        yield
        mid["u0"] = [_mm(t_rw[p], stack_h(av[p])) for p in PR]
        yield

    def dependent(pre, mid, res):
        st_rw = [s_wkv[p] for p in PR]
        st_g = [s_ssm[h] for h in HD]
        st_gbd = [blockdiag2(st_g[2 * j], st_g[2 * j + 1]) for j in GP]
        u = [_mm(mid["w_rw"][p], st_rw[p], _NT) + mid["u0"][p] for p in PR]
        yield
        rs = [_mm(pre["r_g"][p], st_rw[p], _NT) for p in PR]
        yield
        v_new = [mid["u_g"][j] - _mm(mid["w_g"][j], st_gbd[j]) for j in GP]
        yield
        qs = [_mm(pair(pre["qe"], j), st_gbd[j]) for j in GP]
        yield
        res["y"] = [rs[p] + _mm(mid["b1"][p], pre["v_st"][p]) - _mm(mid["b2"][p], stack_h(u[p])) for p in PR]
        yield
        res["o"] = [qs[j] + _mm(mid["attn"][j], blockdiag2(v_new[j][:, 0:LANES], v_new[j][:, LANES:2 * LANES]))
                    for j in GP]
        yield
        for p in PR:
            upd = _mm(jnp.concatenate([pre["v"][p], (-u[p]).astype(BF16)], axis=0),
                      jnp.concatenate([pre["k_inv"][p], pre["kka_inv"][p]], axis=0), _TN)
            s_wkv[p] = (st_rw[p] + upd * bd128) * pre["gam_end"][:, psl[p]]
        yield
        for h in HD:
            vn = v_new[h // 2][:, (h % 2) * LANES:(h % 2 + 1) * LANES]
            s_ssm[h] = st_g[h] * pre["blk_dec"][h] + _mm(pre["k_dec"][h], vn, _TN)
        yield

    def post(sub, pre, res):
        y_rw = jnp.concatenate(res["y"], axis=1)
        inv_n = 1.0 / RW_HEAD_DIM
        y_c = y_rw - seg_sum(y_rw) * inv_n
        yield
        y_n = y_c * lax.rsqrt(seg_sum(y_c * y_c) * inv_n + RW_GN_EPS)
        yield
        y_n = y_n * lnw_ref[...] + lnb_ref[...]
        yb = (y_n + pre["bonus"]) * pre["g"]
        yield
        o_all = jnp.concatenate(res["o"], axis=1)
        yc = jnp.concatenate(
            [o_all[:, s] * lax.rsqrt(jnp.mean(o_all[:, s] * o_all[:, s], axis=-1, keepdims=True) + NORM_EPS)
             * gnw_ref[...] for s in psl], axis=1)
        yield
        m = pre["ya"] + pre["gate_b"] * yb + yc * pre["zg"]
        m_ref[sub * C:(sub + 1) * C, :] = m.astype(m_ref.dtype)
        yield

    def weave(*gens):
        gens = [g for g in gens if g is not None]
        while gens:
            for g in list(gens):
                try:
                    next(g)
                except StopIteration:
                    gens.remove(g)

    pres = [dict() for _ in range(n_sub)]
    mids = [dict() for _ in range(n_sub)]
    ress = [dict() for _ in range(n_sub)]

    def chain(*gens):
        for g in gens:
            yield from g

    groups = [list(range(i, min(i + 2, n_sub))) for i in range(0, n_sub, 2)]
    for s in groups[0]:
        weave(prep(s, pres[s]))
    for gi, grp in enumerate(groups):
        fill = []
        if gi + 1 < len(groups):
            fill += [prep(s, pres[s]) for s in groups[gi + 1]]
        if gi:
            fill += [post(s, pres[s], ress[s]) for s in groups[gi - 1]]
        weave(*[independent(pres[s], mids[s]) for s in grp], chain(*fill))
        weave(chain(*[dependent(pres[s], mids[s], ress[s]) for s in grp]))
    weave(chain(*[post(s, pres[s], ress[s]) for s in groups[-1]]))

    for name, ext in exts.items():
        ext[0:HALO, :] = loaders[name](R - BF16_ROWS, R, 0, ext.shape[1])[BF16_ROWS - HALO:BF16_ROWS]

    @pl.when(t_idx == n_t - 1)
    def _finish():
        o_conva_ref[0] = ext_conv[HALO - 2:HALO, :]
        o_gconv_ref[0] = ext_qkv[HALO - 3:HALO, :]
        o_shrkv_ref[0] = ext_rkv[HALO - 1:HALO, :]
        o_shsm_ref[0] = ext_sm[HALO - 1:HALO, :]
        o_wkv_ref[0] = s_wkv[...]
        o_ssm_ref[0] = s_ssm[...]


def _mixer(proj_main, proj_gates, states, prm, nb, t_len, chunk):
    n_sub = MIXER_SUB if t_len % (MIXER_SUB * chunk) == 0 else 1
    blk_rows = n_sub * chunk
    n_t = t_len // blk_rows
    rows = nb * t_len

    def pspec(width, col):
        blk = col // width
        return pl.BlockSpec((blk_rows, width), lambda b, t: (b * n_t + t, blk))

    def sspec(shape):
        nd = len(shape)
        return pl.BlockSpec((1,) + tuple(shape[1:]), lambda b, t: (b,) + (0,) * (nd - 1))

    def wspec(arr):
        return pl.BlockSpec(arr.shape, lambda b, t: (0,) * arr.ndim)

    st_names = ("conva", "shrkv", "shsm", "wkv", "gconv", "ssm")
    st_list = [states[n] for n in st_names]
    w_names = ("conv_a_w", "mu_rkv", "mu_sm", "w0", "w2", "a0", "a2", "g2", "k_k", "k_a", "r_k",
               "ln_w", "ln_b", "gdn_conv_w", "a_log", "dt_bias", "gdn_norm_w")
    w_list = [prm[n] for n in w_names]
    in_specs = [pspec(COL_GAB, 0), pspec(LANES, COL_GAB), pspec(3 * D_MODEL, 0)] + [
        sspec(s.shape) for s in st_list] + [wspec(w) for w in w_list]
    out_shapes = [jax.ShapeDtypeStruct((rows, D_MODEL), BF16)] + [
        jax.ShapeDtypeStruct(s.shape, F32) for s in st_list]
    out_specs = [pl.BlockSpec((blk_rows, D_MODEL), lambda b, t: (b * n_t + t, 0))] + [
        sspec(s.shape) for s in st_list]
    scratch = [
        pltpu.VMEM((HALO + chunk, D_MODEL), F32),
        pltpu.VMEM((HALO + chunk, 3 * D_MODEL), F32),
        pltpu.VMEM((HALO + chunk, 3 * D_MODEL), F32),
        pltpu.VMEM((HALO + chunk, R_SMALL), F32),
        pltpu.VMEM((RW_PAIRS, LANES, LANES), F32),
        pltpu.VMEM((GDN_HEADS, GDN_HEAD_DIM, GDN_HEAD_DIM), F32),
    ]
    outs = pl.pallas_call(
        functools.partial(_mixer_kernel, chunk, n_sub),
        grid=(nb, n_t),
        in_specs=in_specs,
        out_specs=out_specs,
        out_shape=out_shapes,
        scratch_shapes=scratch,
        compiler_params=pltpu.CompilerParams(
            dimension_semantics=("arbitrary", "arbitrary"), vmem_limit_bytes=VMEM_LIMIT),
        name="mixer",
    )(proj_main, proj_main, proj_gates, *st_list, *w_list)
    return outs[0], dict(zip(st_names, outs[1:]))


def _prep_layer(l, norm_mix_pre, norm_mix_post, norm_mlp_pre, norm_mlp_post, w_in, conv_a_w, rwkv_mu, rwkv_w0,
                rwkv_w2, rwkv_a0, rwkv_a2, rwkv_g2, rwkv_k_k, rwkv_k_a, rwkv_r_k, rwkv_ln_w, rwkv_ln_b,
                gdn_conv_w, gdn_a_log, gdn_dt_bias, gdn_norm_w, w_o, w_ff1, w_ff2):
    d = D_MODEL
    row = lambda a: a.reshape(1, -1).astype(F32)
    pad_heads = lambda a: jnp.pad(a.astype(F32), (0, LANES - GDN_HEADS)).reshape(1, LANES)
    mu = rwkv_mu[l]
    return {
        "g_pre": row(norm_mix_pre[l]), "g_post": row(norm_mix_post[l]),
        "g_pre2": row(norm_mlp_pre[l]), "g_post2": row(norm_mlp_post[l]),
        "w_o": w_o[l].astype(BF16), "w_ff1": w_ff1[l].astype(BF16), "w_ff2": w_ff2[l].astype(BF16),
        "conv_a_w": conv_a_w[l], "mu_rkv": row(mu[0:3 * d]), "mu_sm": row(mu[3 * d:]),
        "w0": row(rwkv_w0[l]), "w2": rwkv_w2[l].astype(BF16), "a0": row(rwkv_a0[l]),
        "a2": rwkv_a2[l].astype(BF16), "g2": rwkv_g2[l].astype(BF16),
        "k_k": row(rwkv_k_k[l]), "k_a": row(rwkv_k_a[l]), "r_k": row(rwkv_r_k[l]),
        "ln_w": row(rwkv_ln_w[l]), "ln_b": row(rwkv_ln_b[l]), "gdn_conv_w": gdn_conv_w[l],
        "a_log": pad_heads(gdn_a_log[l]), "dt_bias": pad_heads(gdn_dt_bias[l]), "gdn_norm_w": row(gdn_norm_w[l]),
    }


def _pack_wkv(s):
    b = s.shape[0]
    s = s.reshape(b, RW_PAIRS, 2, RW_HEAD_DIM, RW_HEAD_DIM)
    zero = jnp.zeros_like(s[:, :, 0])
    top = jnp.concatenate([s[:, :, 0], zero], axis=-1)
    bot = jnp.concatenate([zero, s[:, :, 1]], axis=-1)
    return jnp.concatenate([top, bot], axis=-2)


def _unpack_wkv(s):
    b = s.shape[0]
    h = RW_HEAD_DIM
    return jnp.stack([s[:, :, 0:h, 0:h], s[:, :, h:, h:]], axis=2).reshape(b, RW_HEADS, h, h)


def _layer(x, l, nb, t_len, chunk, st, prm, w_main, w_gates, tm_proj, tm_ffn):
    conv_a, rw_shift, rw_wkv, gdn_conv, gdn_ssm = st
    states = {
        "conva": conv_a, "shrkv": rw_shift[:, None, 0:3 * D_MODEL], "shsm": rw_shift[:, None, 3 * D_MODEL:],
        "wkv": _pack_wkv(rw_wkv), "gconv": gdn_conv, "ssm": gdn_ssm,
    }
    proj_main = _proj(x, prm["g_pre"], w_main, l, P_MAIN, MAIN_TN, tm_proj)
    proj_gates = _proj(x, prm["g_pre"], w_gates, l, 3 * D_MODEL, GATES_TN, tm_proj)
    m, new = _mixer(proj_main, proj_gates, states, prm, nb, t_len, chunk)
    x = _outffn(m, x, prm["w_o"], prm["w_ff1"], prm["w_ff2"], prm["g_post"], prm["g_pre2"], prm["g_post2"], tm_ffn)
    new_st = (new["conva"], jnp.concatenate([new["shrkv"][:, 0], new["shsm"][:, 0]], axis=-1),
              _unpack_wkv(new["wkv"]), new["gconv"], new["ssm"])
    return x, new_st


def _row_tile(rows, want):
    t = min(rows, want)
    while rows % t:
        t //= 2
    return t


def _forward(x_prompt, x_sample, sample_states, weights):
    bp, tp, d = x_prompt.shape
    bs, ts, _ = x_sample.shape
    zero_states = (jnp.zeros((bp, 2, d), F32), jnp.zeros((bp, 3 * d + R_SMALL), F32),
                   jnp.zeros((bp, RW_HEADS, RW_HEAD_DIM, RW_HEAD_DIM), F32),
                   jnp.zeros((bp, 3, 3 * d), F32), jnp.zeros((bp, GDN_HEADS, GDN_HEAD_DIM, GDN_HEAD_DIM), F32))
    y_p = x_prompt.reshape(bp * tp, d)
    y_s = x_sample.reshape(bs * ts, d)
    new_p = [[] for _ in range(5)]
    new_s = [[] for _ in range(5)]
    n_layers = weights[0].shape[0]
    w_in = weights[4]
    w_main = jnp.swapaxes(w_in, 1, 2)
    w_gates = w_main[:, COL_GATES:P_IN, :]
    for l in range(n_layers):
        prm = _prep_layer(l, *weights)
        y_p, st_p = _layer(y_p, l, bp, tp, min(CHUNK, tp), zero_states, prm, w_main, w_gates,
                           _row_tile(bp * tp, 1024), _row_tile(bp * tp, 512))
        y_s, st_s = _layer(y_s, l, bs, ts, min(CHUNK, ts), tuple(s[l] for s in sample_states), prm, w_main, w_gates,
                           _row_tile(bs * ts, 1024), _row_tile(bs * ts, 256))
        for i in range(5):
            new_p[i].append(st_p[i])
            new_s[i].append(st_s[i])
    outs_p = [jnp.stack(t, axis=0) for t in new_p]
    outs_s = [jnp.stack(t, axis=0) for t in new_s]
    return (y_p.reshape(bp, tp, d), y_s.reshape(bs, ts, d), *outs_p, *outs_s)


def kernel(x_prompt, x_sample, state_conv_a, state_rwkv_shift, state_rwkv_wkv, state_gdn_conv, state_gdn_ssm,
           norm_mix_pre, norm_mix_post, norm_mlp_pre, norm_mlp_post, w_in, conv_a_w, rwkv_mu, rwkv_w0, rwkv_w2,
           rwkv_a0, rwkv_a2, rwkv_g2, rwkv_k_k, rwkv_k_a, rwkv_r_k, rwkv_ln_w, rwkv_ln_b, gdn_conv_w, gdn_a_log,
           gdn_dt_bias, gdn_norm_w, w_o, w_ff1, w_ff2):
    sample_states = (state_conv_a, state_rwkv_shift, state_rwkv_wkv, state_gdn_conv, state_gdn_ssm)
    weights = (norm_mix_pre, norm_mix_post, norm_mlp_pre, norm_mlp_post, w_in, conv_a_w, rwkv_mu, rwkv_w0, rwkv_w2,
               rwkv_a0, rwkv_a2, rwkv_g2, rwkv_k_k, rwkv_k_a, rwkv_r_k, rwkv_ln_w, rwkv_ln_b, gdn_conv_w,
               gdn_a_log, gdn_dt_bias, gdn_norm_w, w_o, w_ff1, w_ff2)
    return _forward(x_prompt, x_sample, sample_states, weights)
```

```python
import functools

import jax
import jax.numpy as jnp
from jax import lax
from jax.experimental import pallas as pl
from jax.experimental.pallas import tpu as pltpu

F32 = jnp.float32
BF16 = jnp.bfloat16

D_MODEL = 1024
DEPTH = 4
CHUNK = 64
NORM_EPS = 1e-6
L2_EPS = 1e-12
RW_HEAD_DIM = 64
RW_HEADS = D_MODEL // RW_HEAD_DIM
RW_PAIRS = RW_HEADS // 2
R_DECAY = D_MODEL // 16
R_AAA = D_MODEL // 16
R_GATE = D_MODEL // 8
R_SMALL = R_DECAY + R_AAA + R_GATE
RW_GN_EPS = 64e-5
GDN_HEAD_DIM = 128
GDN_HEADS = D_MODEL // GDN_HEAD_DIM
D_FF = 4 * D_MODEL
LANES = 128
HALO = 8
BF16_ROWS = 16
MIXER_SUB = 4

COL_CONV = 0
COL_RKV = 3 * D_MODEL
COL_SMALL = 6 * D_MODEL
COL_QKV = COL_SMALL + R_SMALL
COL_Z = COL_QKV + 3 * D_MODEL
COL_GAB = COL_Z + D_MODEL
COL_GATES = COL_GAB + 2 * GDN_HEADS
P_IN = COL_GATES + 3 * D_MODEL
MAIN_TN = 14 * LANES
P_MAIN = ((COL_GAB + LANES + MAIN_TN - 1) // MAIN_TN) * MAIN_TN
GATES_TN = 3 * D_MODEL // 2

VMEM_LIMIT = 56 * 1024 * 1024

_NN = (((1,), (0,)), ((), ()))
_NT = (((1,), (1,)), ((), ()))
_TN = (((0,), (0,)), ((), ()))


def _mm(a, b, dims=_NN):
    return lax.dot_general(a.astype(BF16), b.astype(BF16), dims, preferred_element_type=F32)


def _split3(x):
    hi = x.astype(BF16)
    r1 = x - hi.astype(F32)
    mid = r1.astype(BF16)
    lo = (r1 - mid.astype(F32)).astype(BF16)
    return hi, mid, lo


def _mm_exact_lhs(a_bf16, x):
    out = None
    for q in _split3(x):
        t = lax.dot_general(a_bf16, q, _NN, preferred_element_type=F32)
        out = t if out is None else out + t
    return out


def _sigmoid(x):
    return 1.0 / (1.0 + jnp.exp(-x))


def _softplus(x):
    return jnp.maximum(x, 0.0) + jnp.log(1.0 + jnp.exp(-jnp.abs(x)))


def _silu(x):
    return x * _sigmoid(x)


def _rms(x, gain):
    return x * lax.rsqrt(jnp.mean(x * x, axis=-1, keepdims=True) + NORM_EPS) * gain


def _proj_kernel(x_ref, g_ref, w_ref, o_ref, wb_ref):
    @pl.when(pl.program_id(1) == 0)
    def _cast():
        wb_ref[...] = w_ref[...].astype(BF16)

    h = _rms(x_ref[...], g_ref[...])
    o_ref[...] = lax.dot_general(h.astype(BF16), wb_ref[...], _NT, preferred_element_type=F32).astype(o_ref.dtype)


def _proj(x, gain, wt_all, layer, n_cols, tn, tm):
    rows = x.shape[0]
    return pl.pallas_call(
        _proj_kernel,
        grid=(n_cols // tn, rows // tm),
        in_specs=[
            pl.BlockSpec((tm, D_MODEL), lambda j, i: (i, 0)),
            pl.BlockSpec((1, D_MODEL), lambda j, i: (0, 0)),
            pl.BlockSpec((None, tn, D_MODEL), lambda j, i: (layer, j, 0)),
        ],
        out_specs=pl.BlockSpec((tm, tn), lambda j, i: (i, j)),
        out_shape=jax.ShapeDtypeStruct((rows, n_cols), BF16),
        scratch_shapes=[pltpu.VMEM((tn, D_MODEL), BF16)],
        compiler_params=pltpu.CompilerParams(
            dimension_semantics=("arbitrary", "arbitrary"), vmem_limit_bytes=VMEM_LIMIT),
        name="proj",
    )(x, gain, wt_all)


def _outffn_kernel(m_ref, x_ref, wo_ref, w1_ref, w2_ref, g1_ref, g2_ref, g3_ref, o_ref):
    mo = jnp.dot(m_ref[...], wo_ref[...], preferred_element_type=F32)
    x = x_ref[...] + _rms(mo, g1_ref[...])
    h2 = _rms(x, g2_ref[...])
    f = jnp.dot(h2.astype(BF16), w1_ref[...], preferred_element_type=F32)
    f = jnp.square(jnp.maximum(f, 0.0))
    f = jnp.dot(f.astype(BF16), w2_ref[...], preferred_element_type=F32)
    o_ref[...] = x + _rms(f, g3_ref[...])


def _outffn(m, x, layer, wo, w1, w2, g_post, g_pre2, g_post2, tm):
    rows = x.shape[0]
    const = lambda i: (0, 0)
    wconst = lambda i: (layer, 0, 0)
    return pl.pallas_call(
        _outffn_kernel,
        grid=(rows // tm,),
        in_specs=[
            pl.BlockSpec((tm, D_MODEL), lambda i: (i, 0)),
            pl.BlockSpec((tm, D_MODEL), lambda i: (i, 0)),
            pl.BlockSpec((None, D_MODEL, D_MODEL), wconst, pipeline_mode=pl.Buffered(1)),
            pl.BlockSpec((None, D_MODEL, D_FF), wconst, pipeline_mode=pl.Buffered(1)),
            pl.BlockSpec((None, D_FF, D_MODEL), wconst, pipeline_mode=pl.Buffered(1)),
            pl.BlockSpec((1, D_MODEL), const),
            pl.BlockSpec((1, D_MODEL), const),
            pl.BlockSpec((1, D_MODEL), const),
        ],
        out_specs=pl.BlockSpec((tm, D_MODEL), lambda i: (i, 0)),
        out_shape=jax.ShapeDtypeStruct((rows, D_MODEL), F32),
        compiler_params=pltpu.CompilerParams(
            dimension_semantics=("arbitrary",), vmem_limit_bytes=VMEM_LIMIT),
        name="outffn",
    )(m, x, wo, w1, w2, g_post, g_pre2, g_post2)


INV_BASE = 16


def _wide_unit_lower_inverse(mats, size, stack_w, out):
    base = min(size, INV_BASE)
    row = lax.broadcasted_iota(jnp.int32, (size, 2 * size), 0)
    col = lax.broadcasted_iota(jnp.int32, (size, 2 * size), 1) % size
    eye = (row == col).astype(F32)
    ps = [jnp.where((row // base) == (col // base), a, 0.0) if base < size else a for a in mats]
    ts = [eye - p for p in ps]
    steps = base.bit_length() - 2
    if steps:
        ps = [_mm(p, stack_w(p)) for p in ps]
        yield
    for step in range(steps):
        sps = [stack_w(p) for p in ps]
        ts = [t + _mm(t, sp) for t, sp in zip(ts, sps)]
        if step + 1 < steps:
            ps = [_mm(p, sp) for p, sp in zip(ps, sps)]
        yield
    b = base
    while b < size:
        off = ((row // (2 * b)) == (col // (2 * b))) & ((row // b) != (col // b))
        lt = [_mm(jnp.where(off, a, 0.0), stack_w(t)) for a, t in zip(mats, ts)]
        yield
        ts = [t - _mm(t, stack_w(x)) for t, x in zip(ts, lt)]
        yield
        b *= 2
    out.extend(ts)


def _mixer_kernel(chunk, n_sub,
                  main_ref, gab_ref, gates_ref,
                  st_conva_ref, st_shrkv_ref, st_shsm_ref, st_wkv_ref, st_gconv_ref, st_ssm_ref,
                  convw_ref, mu_rkv_ref, mu_sm_ref, w0_ref, w2_ref, a0_ref, a2_ref, g2_ref,
                  kk_ref, ka_ref, rk_ref, lnw_ref, lnb_ref, gconvw_ref, alog_ref, dtb_ref, gnw_ref,
                  m_ref, o_conva_ref, o_shrkv_ref, o_shsm_ref, o_wkv_ref, o_gconv_ref, o_ssm_ref,
                  ext_conv, ext_qkv, ext_rkv, ext_sm, s_wkv, s_ssm):
    C = chunk
    R = n_sub * C
    W2 = 2 * C
    t_idx = pl.program_id(1)
    n_t = pl.num_programs(1)
    conv_ref = main_ref.at[:, COL_CONV:COL_CONV + 3 * D_MODEL]
    rkv_ref = main_ref.at[:, COL_RKV:COL_RKV + 3 * D_MODEL]
    small_ref = main_ref.at[:, COL_SMALL:COL_SMALL + R_SMALL]
    qkv_ref = main_ref.at[:, COL_QKV:COL_QKV + 3 * D_MODEL]
    z_ref = main_ref.at[:, COL_Z:COL_Z + D_MODEL]

    @pl.when(t_idx == 0)
    def _init():
        ext_conv[0:HALO, :] = jnp.zeros((HALO, D_MODEL), F32)
        ext_conv[HALO - 2:HALO, :] = st_conva_ref[0]
        ext_qkv[0:HALO, :] = jnp.zeros((HALO, 3 * D_MODEL), F32)
        ext_qkv[HALO - 3:HALO, :] = st_gconv_ref[0]
        ext_rkv[0:HALO, :] = jnp.zeros((HALO, 3 * D_MODEL), F32)
        ext_rkv[HALO - 1:HALO, :] = st_shrkv_ref[0]
        ext_sm[0:HALO, :] = jnp.zeros((HALO, R_SMALL), F32)
        ext_sm[HALO - 1:HALO, :] = st_shsm_ref[0]
        s_wkv[...] = st_wkv_ref[0]
        s_ssm[...] = st_ssm_ref[0]

    tri_incl = (lax.broadcasted_iota(jnp.int32, (C, C), 0) >= lax.broadcasted_iota(jnp.int32, (C, C), 1)).astype(BF16)

    li = lax.broadcasted_iota(jnp.int32, (LANES, LANES), 0) // RW_HEAD_DIM
    lj = lax.broadcasted_iota(jnp.int32, (LANES, LANES), 1) // RW_HEAD_DIM
    bd128 = (li == lj).astype(F32)
    bd128_bf = bd128.astype(BF16)

    def seg_sum(x):
        xb = x.astype(BF16)
        return jnp.concatenate(
            [lax.dot_general(xb[:, j * LANES:(j + 1) * LANES], bd128_bf, _NN, preferred_element_type=F32)
             for j in range(D_MODEL // LANES)], axis=1)

    t_w = lax.broadcasted_iota(jnp.int32, (C, W2), 0)
    s_w = lax.broadcasted_iota(jnp.int32, (C, W2), 1) % C
    strict_w = t_w > s_w
    incl_w = t_w >= s_w
    lane_h = lax.broadcasted_iota(jnp.int32, (1, LANES), 1)
    mh0 = (lane_h < RW_HEAD_DIM).astype(BF16)
    mh1 = (lane_h >= RW_HEAD_DIM).astype(BF16)
    lane_w = lax.broadcasted_iota(jnp.int32, (1, W2), 1)
    mw0 = (lane_w < C).astype(BF16)
    mw1 = (lane_w >= C).astype(BF16)

    def stack_h(x):
        xb = x.astype(BF16)
        return jnp.concatenate([xb * mh0, xb * mh1], axis=0)

    def stack_w(x):
        xb = x.astype(BF16)
        return jnp.concatenate([xb * mw0, xb * mw1], axis=0)

    def blockdiag2(x0, x1):
        z = jnp.zeros(x0.shape, BF16)
        return jnp.concatenate([jnp.concatenate([x0.astype(BF16), z], axis=1),
                                jnp.concatenate([z, x1.astype(BF16)], axis=1)], axis=0)

    def pair(xs, j):
        return jnp.concatenate([xs[2 * j], xs[2 * j + 1]], axis=1)

    PR = range(RW_PAIRS)
    HD = range(GDN_HEADS)
    GP = range(GDN_HEADS // 2)
    psl = [slice(p * LANES, (p + 1) * LANES) for p in PR]

    def load_conv(lo, hi, c0, width):
        return (conv_ref[lo:hi, D_MODEL + c0:D_MODEL + c0 + width].astype(F32)
                * conv_ref[lo:hi, 2 * D_MODEL + c0:2 * D_MODEL + c0 + width].astype(F32))

    loaders = {
        "conv": load_conv,
        "rkv": lambda lo, hi, c0, width: rkv_ref[lo:hi, c0:c0 + width].astype(F32),
        "sm": lambda lo, hi, c0, width: small_ref[lo:hi, c0:c0 + width].astype(F32),
        "qkv": lambda lo, hi, c0, width: qkv_ref[lo:hi, c0:c0 + width].astype(F32),
    }
    exts = {"conv": ext_conv, "rkv": ext_rkv, "sm": ext_sm, "qkv": ext_qkv}
    for name, ext in exts.items():
        ext[HALO:HALO + C, :] = loaders[name](0, C, 0, ext.shape[1])

    def prep(sub, pre):
        r0 = sub * C
        rows = slice(r0, r0 + C)

        def back(name, k, c0, width):
            if sub == 0:
                return pltpu.roll(exts[name][0:HALO + C, c0:c0 + width], k, 0)[HALO:HALO + C]
            blk = loaders[name](r0 - BF16_ROWS, r0 + C, c0, width)
            return pltpu.roll(blk, k, 0)[BF16_ROWS:BF16_ROWS + C]

        def cur(name, c0, width):
            if sub == 0:
                return exts[name][HALO:HALO + C, c0:c0 + width]
            return loaders[name](r0, r0 + C, c0, width)

        pre["ya"] = conv_ref[rows, 0:D_MODEL].astype(F32) * (
            convw_ref[0:1, :] * back("conv", 2, 0, D_MODEL)
            + convw_ref[1:2, :] * back("conv", 1, 0, D_MODEL)
            + convw_ref[2:3, :] * cur("conv", 0, D_MODEL)) * _sigmoid(gates_ref[rows, 0:D_MODEL].astype(F32))
        yield

        p_sm = cur("sm", 0, R_SMALL)
        ps_sm = p_sm + (back("sm", 1, 0, R_SMALL) - p_sm) * mu_sm_ref[...]
        xw = ps_sm[:, 0:R_DECAY]
        xa = ps_sm[:, R_DECAY:R_DECAY + R_AAA]
        xg = ps_sm[:, R_DECAY + R_AAA:R_SMALL]
        w_log = -_softplus(-(w0_ref[...] + _mm(jnp.tanh(xw), w2_ref[...]))) - 0.5
        logd = -jnp.exp(w_log)
        yield
        cum = _mm_exact_lhs(tri_incl, logd)
        a = _sigmoid(a0_ref[...] + _mm(xa, a2_ref[...]))
        pre["g"] = _mm(_sigmoid(xg), g2_ref[...])
        yield

        def shifted(c0):
            p = cur("rkv", c0, D_MODEL)
            return p + (back("rkv", 1, c0, D_MODEL) - p) * mu_rkv_ref[:, c0:c0 + D_MODEL]

        k = shifted(D_MODEL)
        yield
        kk0 = k * kk_ref[...]
        kk = kk0 * lax.rsqrt(seg_sum(kk0 * kk0) + L2_EPS)
        yield
        k2 = k * (1.0 + (a - 1.0) * ka_ref[...])
        kka = kk * a
        e_inv = jnp.exp(-cum)
        yield
        kk_g = (kk * jnp.exp(cum - logd)).astype(BF16)
        pre["kk_g"] = [kk_g[:, s] for s in psl]
        pre["kk_g_st"] = [stack_h(kk_g[:, s]) for s in psl]
        yield
        k_inv = (k2 * e_inv).astype(BF16)
        pre["k_inv"] = [k_inv[:, s] for s in psl]
        pre["k_inv_st"] = [stack_h(k_inv[:, s]) for s in psl]
        yield
        kka_inv = (kka * e_inv).astype(BF16)
        pre["kka_inv"] = [kka_inv[:, s] for s in psl]
        pre["kka_inv_st"] = [stack_h(kka_inv[:, s]) for s in psl]
        yield
        r = shifted(0)
        r_g = (r * jnp.exp(cum)).astype(BF16)
        pre["r_g"] = [r_g[:, s] for s in psl]
        yield
        v = shifted(2 * D_MODEL)
        pre["v"] = [v[:, s].astype(BF16) for s in psl]
        pre["v_st"] = [stack_h(v[:, s]) for s in psl]
        pre["gam_end"] = jnp.exp(cum[C - 1:C, :])
        yield
        pre["bonus"] = seg_sum(r * k2 * rk_ref[...]) * v
        yield

        gab = gab_ref[rows, :].astype(F32)
        g_log = -jnp.exp(alog_ref[...]) * _softplus(gab + dtb_ref[...])
        beta_all = _sigmoid(gab)
        g_cum = _mm_exact_lhs(tri_incl, g_log)
        g_cum_t = g_cum.T
        yield

        def conv_silu(c0):
            cq = (gconvw_ref[0:1, c0:c0 + LANES] * back("qkv", 3, c0, LANES)
                  + gconvw_ref[1:2, c0:c0 + LANES] * back("qkv", 2, c0, LANES)
                  + gconvw_ref[2:3, c0:c0 + LANES] * back("qkv", 1, c0, LANES)
                  + gconvw_ref[3:4, c0:c0 + LANES] * cur("qkv", c0, LANES))
            return _silu(cq)

        qn, kn, vb, kb, kbe, qe, k_dec, blk_dec = [], [], [], [], [], [], [], []
        pre.update(qn=qn, kn=kn, vb=vb, kb=kb, kbe=kbe, qe=qe, k_dec=k_dec, blk_dec=blk_dec, decay_w=[])
        for h in HD:
            qh = conv_silu(h * LANES)
            kh = conv_silu(D_MODEL + h * LANES)
            vh = conv_silu(2 * D_MODEL + h * LANES)
            qn_h = qh * lax.rsqrt(jnp.sum(qh * qh, axis=-1, keepdims=True) + L2_EPS) * (GDN_HEAD_DIM ** -0.5)
            kn_h = kh * lax.rsqrt(jnp.sum(kh * kh, axis=-1, keepdims=True) + L2_EPS)
            gc = g_cum[:, h:h + 1]
            beta = beta_all[:, GDN_HEADS + h:GDN_HEADS + h + 1]
            e_g = jnp.exp(gc)
            g_last = g_cum[C - 1:C, h:h + 1]
            kb_h = kn_h * beta
            qn.append(qn_h.astype(BF16))
            kn.append(kn_h.astype(BF16))
            kb.append(kb_h.astype(BF16))
            vb.append((vh * beta).astype(BF16))
            kbe.append((kb_h * e_g).astype(BF16))
            qe.append((qn_h * e_g).astype(BF16))
            k_dec.append((kn_h * jnp.exp(g_last - gc)).astype(BF16))
            blk_dec.append(jnp.exp(g_last))
            if h % 2 == 1:
                gc_w = jnp.where(lane_w < C, g_cum[:, h - 1:h], gc)
                gr_w = jnp.concatenate([g_cum_t[h - 1:h, :], g_cum_t[h:h + 1, :]], axis=1)
                pre["decay_w"].append(jnp.where(incl_w, jnp.exp(jnp.where(incl_w, gc_w - gr_w, 0.0)), 0.0))
            yield
        pre["zg"] = (_silu(z_ref[rows, :].astype(F32))
                     * _sigmoid(gates_ref[rows, 2 * D_MODEL:3 * D_MODEL].astype(F32)))
        yield
        pre["gate_b"] = _sigmoid(gates_ref[rows, D_MODEL:2 * D_MODEL].astype(F32))
        yield

    def independent(pre, mid):
        kn_bd = [blockdiag2(pre["kn"][2 * j], pre["kn"][2 * j + 1]) for j in GP]
        a1 = [jnp.where(strict_w, _mm(pre["kk_g"][p], pre["k_inv_st"][p], _NT), 0.0) for p in PR]
        yield
        a2 = [jnp.where(strict_w, _mm(pre["kk_g"][p], pre["kka_inv_st"][p], _NT), 0.0) for p in PR]
        yield
        a_g = [jnp.where(strict_w, _mm(pair(pre["kb"], j), kn_bd[j], _NT) * pre["decay_w"][j], 0.0) for j in GP]
        yield
        mid["b1"] = [jnp.where(incl_w, _mm(pre["r_g"][p], pre["k_inv_st"][p], _NT), 0.0) for p in PR]
        yield
        mid["b2"] = [jnp.where(incl_w, _mm(pre["r_g"][p], pre["kka_inv_st"][p], _NT), 0.0) for p in PR]
        yield
        mid["attn"] = [_mm(pair(pre["qn"], j), kn_bd[j], _NT) * pre["decay_w"][j] for j in GP]
        yield
        t_all = []
        yield from _wide_unit_lower_inverse(a2 + a_g, C, stack_w, t_all)
        t_rw, t_g = t_all[:RW_PAIRS], t_all[RW_PAIRS:]
        mid["w_rw"] = [_mm(t_rw[p], pre["kk_g_st"][p]) for p in PR]
        yield
        av = [_mm(a1[p], pre["v_st"][p]) for p in PR]
        yield
        mid["u_g"] = [_mm(t_g[j], blockdiag2(pre["vb"][2 * j], pre["vb"][2 * j + 1])) for j in GP]
        yield
        mid["w_g"] = [_mm(t_g[j], blockdiag2(pre["kbe"][2 * j], pre["kbe"][2 * j + 1])) for j in GP]
        yield
        mid["u0"] = [_mm(t_rw[p], stack_h(av[p])) for p in PR]
        yield

    def dependent(pre, mid, res):
        st_rw = [s_wkv[p] for p in PR]
        st_g = [s_ssm[h] for h in HD]
        st_gbd = [blockdiag2(st_g[2 * j], st_g[2 * j + 1]) for j in GP]
        u = [_mm(mid["w_rw"][p], st_rw[p], _NT) + mid["u0"][p] for p in PR]
        yield
        rs = [_mm(pre["r_g"][p], st_rw[p], _NT) for p in PR]
        yield
        v_new = [mid["u_g"][j] - _mm(mid["w_g"][j], st_gbd[j]) for j in GP]
        yield
        qs = [_mm(pair(pre["qe"], j), st_gbd[j]) for j in GP]
        yield
        res["y"] = [rs[p] + _mm(mid["b1"][p], pre["v_st"][p]) - _mm(mid["b2"][p], stack_h(u[p])) for p in PR]
        yield
        res["o"] = [qs[j] + _mm(mid["attn"][j], blockdiag2(v_new[j][:, 0:LANES], v_new[j][:, LANES:2 * LANES]))
                    for j in GP]
        yield
        for p in PR:
            upd = _mm(jnp.concatenate([pre["v"][p], (-u[p]).astype(BF16)], axis=0),
                      jnp.concatenate([pre["k_inv"][p], pre["kka_inv"][p]], axis=0), _TN)
            s_wkv[p] = (st_rw[p] + upd * bd128) * pre["gam_end"][:, psl[p]]
        yield
        for h in HD:
            vn = v_new[h // 2][:, (h % 2) * LANES:(h % 2 + 1) * LANES]
            s_ssm[h] = st_g[h] * pre["blk_dec"][h] + _mm(pre["k_dec"][h], vn, _TN)
        yield

    def post(sub, pre, res):
        y_rw = jnp.concatenate(res["y"], axis=1)
        inv_n = 1.0 / RW_HEAD_DIM
        y_c = y_rw - seg_sum(y_rw) * inv_n
        yield
        y_n = y_c * lax.rsqrt(seg_sum(y_c * y_c) * inv_n + RW_GN_EPS)
        yield
        y_n = y_n * lnw_ref[...] + lnb_ref[...]
        yb = (y_n + pre["bonus"]) * pre["g"]
        yield
        o_all = jnp.concatenate(res["o"], axis=1)
        yc = jnp.concatenate(
            [o_all[:, s] * lax.rsqrt(jnp.mean(o_all[:, s] * o_all[:, s], axis=-1, keepdims=True) + NORM_EPS)
             * gnw_ref[...] for s in psl], axis=1)
        yield
        m = pre["ya"] + pre["gate_b"] * yb + yc * pre["zg"]
        m_ref[sub * C:(sub + 1) * C, :] = m.astype(m_ref.dtype)
        yield

    def weave(*gens):
        gens = [g for g in gens if g is not None]
        while gens:
            for g in list(gens):
                try:
                    next(g)
                except StopIteration:
                    gens.remove(g)

    pres = [dict() for _ in range(n_sub)]
    mids = [dict() for _ in range(n_sub)]
    ress = [dict() for _ in range(n_sub)]

    def chain(*gens):
        for g in gens:
            yield from g

    groups = [list(range(i, min(i + 2, n_sub))) for i in range(0, n_sub, 2)]
    for s in groups[0]:
        weave(prep(s, pres[s]))
    for gi, grp in enumerate(groups):
        fill = []
        if gi + 1 < len(groups):
            fill += [prep(s, pres[s]) for s in groups[gi + 1]]
        if gi:
            fill += [post(s, pres[s], ress[s]) for s in groups[gi - 1]]
        weave(*[independent(pres[s], mids[s]) for s in grp], chain(*fill))
        weave(chain(*[dependent(pres[s], mids[s], ress[s]) for s in grp]))
    weave(chain(*[post(s, pres[s], ress[s]) for s in groups[-1]]))

    for name, ext in exts.items():
        ext[0:HALO, :] = loaders[name](R - BF16_ROWS, R, 0, ext.shape[1])[BF16_ROWS - HALO:BF16_ROWS]

    @pl.when(t_idx == n_t - 1)
    def _finish():
        o_conva_ref[0] = ext_conv[HALO - 2:HALO, :]
        o_gconv_ref[0] = ext_qkv[HALO - 3:HALO, :]
        o_shrkv_ref[0] = ext_rkv[HALO - 1:HALO, :]
        o_shsm_ref[0] = ext_sm[HALO - 1:HALO, :]
        o_wkv_ref[0] = s_wkv[...]
        o_ssm_ref[0] = s_ssm[...]


def _mixer(proj_main, proj_gates, states, prm, nb, t_len, chunk):
    n_sub = MIXER_SUB if t_len % (MIXER_SUB * chunk) == 0 else 1
    blk_rows = n_sub * chunk
    n_t = t_len // blk_rows
    rows = nb * t_len

    def pspec(width, col):
        blk = col // width
        return pl.BlockSpec((blk_rows, width), lambda b, t: (b * n_t + t, blk))

    def sspec(shape):
        nd = len(shape)
        return pl.BlockSpec((1,) + tuple(shape[1:]), lambda b, t: (b,) + (0,) * (nd - 1))

    def wspec(arr):
        return pl.BlockSpec(arr.shape, lambda b, t: (0,) * arr.ndim)

    st_names = ("conva", "shrkv", "shsm", "wkv", "gconv", "ssm")
    st_list = [states[n] for n in st_names]
    w_names = ("conv_a_w", "mu_rkv", "mu_sm", "w0", "w2", "a0", "a2", "g2", "k_k", "k_a", "r_k",
               "ln_w", "ln_b", "gdn_conv_w", "a_log", "dt_bias", "gdn_norm_w")
    w_list = [prm[n] for n in w_names]
    in_specs = [pspec(COL_GAB, 0), pspec(LANES, COL_GAB), pspec(3 * D_MODEL, 0)] + [
        sspec(s.shape) for s in st_list] + [wspec(w) for w in w_list]
    out_shapes = [jax.ShapeDtypeStruct((rows, D_MODEL), BF16)] + [
        jax.ShapeDtypeStruct(s.shape, F32) for s in st_list]
    out_specs = [pl.BlockSpec((blk_rows, D_MODEL), lambda b, t: (b * n_t + t, 0))] + [
        sspec(s.shape) for s in st_list]
    scratch = [
        pltpu.VMEM((HALO + chunk, D_MODEL), F32),
        pltpu.VMEM((HALO + chunk, 3 * D_MODEL), F32),
        pltpu.VMEM((HALO + chunk, 3 * D_MODEL), F32),
        pltpu.VMEM((HALO + chunk, R_SMALL), F32),
        pltpu.VMEM((RW_PAIRS, LANES, LANES), F32),
        pltpu.VMEM((GDN_HEADS, GDN_HEAD_DIM, GDN_HEAD_DIM), F32),
    ]
    outs = pl.pallas_call(
        functools.partial(_mixer_kernel, chunk, n_sub),
        grid=(nb, n_t),
        in_specs=in_specs,
        out_specs=out_specs,
        out_shape=out_shapes,
        scratch_shapes=scratch,
        compiler_params=pltpu.CompilerParams(
            dimension_semantics=("arbitrary", "arbitrary"), vmem_limit_bytes=VMEM_LIMIT),
        name="mixer",
    )(proj_main, proj_main, proj_gates, *st_list, *w_list)
    return outs[0], dict(zip(st_names, outs[1:]))


def _prep_layer(l, norm_mix_pre, norm_mix_post, norm_mlp_pre, norm_mlp_post, w_in, conv_a_w, rwkv_mu, rwkv_w0,
                rwkv_w2, rwkv_a0, rwkv_a2, rwkv_g2, rwkv_k_k, rwkv_k_a, rwkv_r_k, rwkv_ln_w, rwkv_ln_b,
                gdn_conv_w, gdn_a_log, gdn_dt_bias, gdn_norm_w, w_o, w_ff1, w_ff2):
    d = D_MODEL
    row = lambda a: a.reshape(1, -1).astype(F32)
    pad_heads = lambda a: jnp.pad(a.astype(F32), (0, LANES - GDN_HEADS)).reshape(1, LANES)
    mu = rwkv_mu[l]
    return {
        "g_pre": row(norm_mix_pre[l]), "g_post": row(norm_mix_post[l]),
        "g_pre2": row(norm_mlp_pre[l]), "g_post2": row(norm_mlp_post[l]),
        "conv_a_w": conv_a_w[l], "mu_rkv": row(mu[0:3 * d]), "mu_sm": row(mu[3 * d:]),
        "w0": row(rwkv_w0[l]), "w2": rwkv_w2[l].astype(BF16), "a0": row(rwkv_a0[l]),
        "a2": rwkv_a2[l].astype(BF16), "g2": rwkv_g2[l].astype(BF16),
        "k_k": row(rwkv_k_k[l]), "k_a": row(rwkv_k_a[l]), "r_k": row(rwkv_r_k[l]),
        "ln_w": row(rwkv_ln_w[l]), "ln_b": row(rwkv_ln_b[l]), "gdn_conv_w": gdn_conv_w[l],
        "a_log": pad_heads(gdn_a_log[l]), "dt_bias": pad_heads(gdn_dt_bias[l]), "gdn_norm_w": row(gdn_norm_w[l]),
    }


def _pack_wkv(s):
    b = s.shape[0]
    s = s.reshape(b, RW_PAIRS, 2, RW_HEAD_DIM, RW_HEAD_DIM)
    zero = jnp.zeros_like(s[:, :, 0])
    top = jnp.concatenate([s[:, :, 0], zero], axis=-1)
    bot = jnp.concatenate([zero, s[:, :, 1]], axis=-1)
    return jnp.concatenate([top, bot], axis=-2)


def _unpack_wkv(s):
    b = s.shape[0]
    h = RW_HEAD_DIM
    return jnp.stack([s[:, :, 0:h, 0:h], s[:, :, h:, h:]], axis=2).reshape(b, RW_HEADS, h, h)


def _layer(x, l, nb, t_len, chunk, st, prm, w_main, w_gates, w_out, tm_proj, tm_ffn):
    conv_a, rw_shift, rw_wkv, gdn_conv, gdn_ssm = st
    states = {
        "conva": conv_a, "shrkv": rw_shift[:, None, 0:3 * D_MODEL], "shsm": rw_shift[:, None, 3 * D_MODEL:],
        "wkv": _pack_wkv(rw_wkv), "gconv": gdn_conv, "ssm": gdn_ssm,
    }
    proj_main = _proj(x, prm["g_pre"], w_main, l, P_MAIN, MAIN_TN, tm_proj)
    proj_gates = _proj(x, prm["g_pre"], w_gates, l, 3 * D_MODEL, GATES_TN, tm_proj)
    m, new = _mixer(proj_main, proj_gates, states, prm, nb, t_len, chunk)
    x = _outffn(m, x, l, *w_out, prm["g_post"], prm["g_pre2"], prm["g_post2"], tm_ffn)
    new_st = (new["conva"], jnp.concatenate([new["shrkv"][:, 0], new["shsm"][:, 0]], axis=-1),
              _unpack_wkv(new["wkv"]), new["gconv"], new["ssm"])
    return x, new_st


def _row_tile(rows, want):
    t = min(rows, want)
    while rows % t:
        t //= 2
    return t


def _forward(x_prompt, x_sample, sample_states, weights):
    bp, tp, d = x_prompt.shape
    bs, ts, _ = x_sample.shape
    zero_states = (jnp.zeros((bp, 2, d), F32), jnp.zeros((bp, 3 * d + R_SMALL), F32),
                   jnp.zeros((bp, RW_HEADS, RW_HEAD_DIM, RW_HEAD_DIM), F32),
                   jnp.zeros((bp, 3, 3 * d), F32), jnp.zeros((bp, GDN_HEADS, GDN_HEAD_DIM, GDN_HEAD_DIM), F32))
    y_p = x_prompt.reshape(bp * tp, d)
    y_s = x_sample.reshape(bs * ts, d)
    new_p = [[] for _ in range(5)]
    new_s = [[] for _ in range(5)]
    n_layers = weights[0].shape[0]
    w_in = weights[4]
    w_main = jnp.swapaxes(w_in, 1, 2)
    w_gates = w_main[:, COL_GATES:P_IN, :]
    w_out = tuple(w.astype(BF16) for w in weights[-3:])
    for l in range(n_layers):
        prm = _prep_layer(l, *weights)
        y_p, st_p = _layer(y_p, l, bp, tp, min(CHUNK, tp), zero_states, prm, w_main, w_gates, w_out,
                           _row_tile(bp * tp, 1024), _row_tile(bp * tp, 512))
        y_s, st_s = _layer(y_s, l, bs, ts, min(CHUNK, ts), tuple(s[l] for s in sample_states), prm, w_main, w_gates,
                           w_out, _row_tile(bs * ts, 1024), _row_tile(bs * ts, 256))
        for i in range(5):
            new_p[i].append(st_p[i])
            new_s[i].append(st_s[i])
    outs_p = [jnp.stack(t, axis=0) for t in new_p]
    outs_s = [jnp.stack(t, axis=0) for t in new_s]
    return (y_p.reshape(bp, tp, d), y_s.reshape(bs, ts, d), *outs_p, *outs_s)


def kernel(x_prompt, x_sample, state_conv_a, state_rwkv_shift, state_rwkv_wkv, state_gdn_conv, state_gdn_ssm,
           norm_mix_pre, norm_mix_post, norm_mlp_pre, norm_mlp_post, w_in, conv_a_w, rwkv_mu, rwkv_w0, rwkv_w2,
           rwkv_a0, rwkv_a2, rwkv_g2, rwkv_k_k, rwkv_k_a, rwkv_r_k, rwkv_ln_w, rwkv_ln_b, gdn_conv_w, gdn_a_log,
           gdn_dt_bias, gdn_norm_w, w_o, w_ff1, w_ff2):
    sample_states = (state_conv_a, state_rwkv_shift, state_rwkv_wkv, state_gdn_conv, state_gdn_ssm)
    weights = (norm_mix_pre, norm_mix_post, norm_mlp_pre, norm_mlp_post, w_in, conv_a_w, rwkv_mu, rwkv_w0, rwkv_w2,
               rwkv_a0, rwkv_a2, rwkv_g2, rwkv_k_k, rwkv_k_a, rwkv_r_k, rwkv_ln_w, rwkv_ln_b, gdn_conv_w,
               gdn_a_log, gdn_dt_bias, gdn_norm_w, w_o, w_ff1, w_ff2)
    return _forward(x_prompt, x_sample, sample_states, weights)
```
